```python
import math
import numpy as np
import jax
import jax.numpy as jnp
from jax import lax

D_MODEL = 1024
BATCH = 16
SEQ = 2048
DEPTH = 4

GRID_W = 64
CTX_LEN = 256
ROPE_BASE = 10000.0
Q_BLOCK = 128

DA_HEADS = 4
DA_HEAD_DIM = 64
DA_QK = DA_HEADS * 2 * DA_HEAD_DIM
DA_V = DA_HEADS * 2 * DA_HEAD_DIM
DA_OUT = DA_V

MLA_HEADS = 4
MLA_Q_RANK = 256
MLA_KV_RANK = 128
MLA_NOPE = 64
MLA_ROPE = 32
MLA_V = 64
MLA_OUT = MLA_HEADS * MLA_V

HG_HEADS = 4
HG_KEY = 64
HG_VAL = 64
HG_WIDTH = HG_HEADS * HG_KEY
HG_OUT = HG_HEADS * HG_VAL
HG_CHUNK = 64

D_MIX = DA_OUT + MLA_OUT + HG_OUT
IN_SPLITS = (DA_QK, DA_QK, DA_V, MLA_Q_RANK, MLA_KV_RANK, MLA_ROPE, HG_WIDTH, HG_OUT, HG_WIDTH, HG_WIDTH, HG_OUT)
IN_COLS = sum(IN_SPLITS)

D_FF = 2816
N_EXPERTS = 8
TOP_K = 2
D_FF_EXPERT = 2816
N_DENSE = (DEPTH + 1) // 2
N_MOE = DEPTH // 2

kernel_name = "hybrid_diffattn_mla_hgrn2_moe_dit"


def rmsnorm(x, g, eps=1e-6):
    xf = x.astype(jnp.float32)
    y = xf * lax.rsqrt(jnp.mean(xf * xf, axis=-1, keepdims=True) + eps)
    return y.astype(x.dtype) * g


def modulate(h, shift, scale):
    return h * (1.0 + scale) + shift


def split_columns(p):
    out, start = [], 0
    for size in IN_SPLITS:
        out.append(p[..., start:start + size])
        start += size
    return out


def axial_rope_tables(L, rot_dim, dtype):
    rows = L // GRID_W
    row, col = jnp.meshgrid(jnp.arange(rows, dtype=jnp.float32), jnp.arange(GRID_W, dtype=jnp.float32), indexing="ij")
    row, col = row.reshape(-1), col.reshape(-1)
    n_freq = rot_dim // 4
    inv = ROPE_BASE ** (-jnp.arange(n_freq, dtype=jnp.float32) / n_freq)
    ang = jnp.concatenate([row[:, None] * inv, col[:, None] * inv], axis=-1)
    return jnp.cos(ang).astype(dtype), jnp.sin(ang).astype(dtype)


def apply_rope(x, cos, sin):
    half = x.shape[-1] // 2
    x1, x2 = x[..., :half], x[..., half:]
    return jnp.concatenate([x1 * cos - x2 * sin, x2 * cos + x1 * sin], axis=-1)


def sdpa(q, k, v, scale):
    s = jnp.einsum("bhgqd,bhgkd->bhgqk", q, k).astype(jnp.float32) * scale
    p = jax.nn.softmax(s, axis=-1).astype(v.dtype)
    return jnp.einsum("bhgqk,bhkv->bhgqv", p, v)


def blocked_attention(q, k, v, scale):
    B, H, G, L, d = q.shape
    nb = L // Q_BLOCK
    qb = jnp.moveaxis(q.reshape(B, H, G, nb, Q_BLOCK, d), 3, 0)
    ob = lax.map(lambda qi: sdpa(qi, k, v, scale), qb)
    return jnp.moveaxis(ob, 0, 3).reshape(B, H, G, L, v.shape[-1])


def diff_attention(pl, pc, lam_vec, gain, lam_init, rope, need_ctx):
    cos, sin = rope

    def heads_qk(a):
        B, L, _ = a.shape
        return a.reshape(B, L, DA_HEADS, 2, DA_HEAD_DIM).transpose(0, 2, 3, 1, 4)

    def heads_v(a):
        B, L, _ = a.shape
        return a.reshape(B, L, DA_HEADS, 2 * DA_HEAD_DIM).transpose(0, 2, 1, 3)

    lv = lam_vec.astype(jnp.float32)
    lam = jnp.exp(jnp.sum(lv[0] * lv[1])) - jnp.exp(jnp.sum(lv[2] * lv[3])) + lam_init
    scale = DA_HEAD_DIM ** -0.5

    def combine(o):
        B, H, _, L, dv = o.shape
        o = o[:, :, 0] - lam.astype(o.dtype) * o[:, :, 1]
        o = rmsnorm(o, gain, 1e-5) * (1.0 - lam_init)
        return o.transpose(0, 2, 1, 3).reshape(B, L, H * dv)

    q_l = apply_rope(heads_qk(pl[0]), cos, sin)
    k_l = apply_rope(heads_qk(pl[1]), cos, sin)
    v_l = heads_v(pl[2])
    k_c, v_c = heads_qk(pc[1]), heads_v(pc[2])
    o = combine(blocked_attention(q_l, jnp.concatenate([k_c, k_l], axis=3), jnp.concatenate([v_c, v_l], axis=2), scale))
    o_c = combine(sdpa(heads_qk(pc[0]), k_c, v_c, scale)) if need_ctx else None
    return o, o_c


def latent_attention(pl, pc, q_norm, w_uq, kv_norm, w_ukv, rope, need_ctx):
    cos, sin = rope
    scale = (MLA_NOPE + MLA_ROPE) ** -0.5

    def queries(cq, rotate):
        B, L, _ = cq.shape
        q = (rmsnorm(cq, q_norm) @ w_uq).reshape(B, L, MLA_HEADS, MLA_NOPE + MLA_ROPE).transpose(0, 2, 1, 3)
        q_nope, q_rope = q[..., :MLA_NOPE], q[..., MLA_NOPE:]
        if rotate:
            q_rope = apply_rope(q_rope, cos, sin)
        return jnp.concatenate([q_nope, q_rope], axis=-1)[:, :, None]

    def keys_values(ckv, kr, rotate):
        B, L, _ = ckv.shape
        kv = (rmsnorm(ckv, kv_norm) @ w_ukv).reshape(B, L, MLA_HEADS, MLA_NOPE + MLA_V).transpose(0, 2, 1, 3)
        k_rope = kr[:, None]
        if rotate:
            k_rope = apply_rope(k_rope, cos, sin)
        k = jnp.concatenate([kv[..., :MLA_NOPE], jnp.broadcast_to(k_rope, (B, MLA_HEADS, L, MLA_ROPE))], axis=-1)
        return k[:, :, None], kv[..., MLA_NOPE:]

    def merge(o):
        B, H, _, L, dv = o.shape
        return o[:, :, 0].transpose(0, 2, 1, 3).reshape(B, L, H * dv)

    k_l, v_l = keys_values(pl[1], pl[2], True)
    k_c, v_c = keys_values(pc[1], pc[2], False)
    o = merge(blocked_attention(queries(pl[0], True), jnp.concatenate([k_c, k_l], axis=3), jnp.concatenate([v_c, v_l], axis=2), scale))
    o_c = merge(sdpa(queries(pc[0], False), k_c, v_c, scale)) if need_ctx else None
    return o, o_c


def chunk_scan(q, k, v, logf, s0):
    B, H, L, dk = q.shape
    dv = v.shape[-1]
    n = L // HG_CHUNK

    def chunks(a):
        return jnp.moveaxis(a.astype(jnp.float32).reshape(B, H, n, HG_CHUNK, a.shape[-1]), 2, 0)

    lower = jnp.tril(jnp.ones((HG_CHUNK, HG_CHUNK), bool))[:, :, None]

    def step(s, inp):
        qc, kc, vc, gc = inp
        b = jnp.cumsum(gc, axis=2)
        b_last = b[:, :, -1:, :]
        o_inter = jnp.einsum("bhtd,bhde->bhte", qc * jnp.exp(b), s)
        rel = jnp.exp(jnp.where(lower, b[:, :, :, None, :] - b[:, :, None, :, :], -jnp.inf))
        scores = jnp.einsum("bhtd,bhtsd,bhsd->bhts", qc, rel, kc)
        o = o_inter + jnp.einsum("bhts,bhse->bhte", scores, vc)
        s_new = jnp.exp(b_last[:, :, 0, :])[..., None] * s + jnp.einsum("bhsd,bhse->bhde", kc * jnp.exp(b_last - b), vc)
        return s_new, o

    s_final, o = lax.scan(step, s0.astype(jnp.float32), (chunks(q), chunks(k), chunks(v), chunks(logf)))
    return jnp.moveaxis(o, 0, 2).reshape(B, H, L, dv), s_final


def final_state(k, v, logf):
    b = jnp.cumsum(logf.astype(jnp.float32), axis=2)
    return jnp.einsum("bhsd,bhse->bhde", k.astype(jnp.float32) * jnp.exp(b[:, :, -1:] - b), v.astype(jnp.float32))


def hgrn2(pl, pc, lb_fwd, lb_bwd, gain, need_ctx):
    def heads(a):
        B, L, _ = a.shape
        return a.reshape(B, L, HG_HEADS, -1).transpose(0, 2, 1, 3)

    def gates(f_raw, lb):
        z = heads(f_raw).astype(jnp.float32)
        lb = lb.reshape(HG_HEADS, 1, HG_KEY)
        return (1.0 - lb) * jax.nn.sigmoid(-z), jnp.log(lb + (1.0 - lb) * jax.nn.sigmoid(z))

    def flip(a):
        return a[:, :, ::-1]

    def readout(o, g):
        B, H, L, dv = o.shape
        o = rmsnorm(o, gain).transpose(0, 2, 1, 3).reshape(B, L, H * dv)
        return o.astype(g.dtype) * jax.nn.silu(g)

    q_l, i_l, ff_l, fb_l, g_l = pl
    q_c, i_c, ff_c, fb_c, g_c = pc
    scale = HG_KEY ** -0.5
    v_l, v_c = heads(i_l), heads(i_c)
    kf_l, logf_l = gates(ff_l, lb_fwd)
    kb_l, logb_l = gates(fb_l, lb_bwd)
    kf_c, logf_c = gates(ff_c, lb_fwd)
    kb_c, logb_c = gates(fb_c, lb_bwd)
    if need_ctx:
        qc = heads(q_c) * scale
        zeros = jnp.zeros(qc.shape[:2] + (HG_KEY, HG_VAL), jnp.float32)
        oc_f, s_fwd = chunk_scan(qc, kf_c, v_c, logf_c, zeros)
        oc_b, s_bwd = chunk_scan(flip(qc), flip(kb_c), flip(v_c), flip(logb_c), zeros)
        o_c = readout(oc_f + flip(oc_b), g_c)
    else:
        s_fwd = final_state(kf_c, v_c, logf_c)
        s_bwd = final_state(flip(kb_c), flip(v_c), flip(logb_c))
        o_c = None
    ql = heads(q_l) * scale
    o_f, _ = chunk_scan(ql, kf_l, v_l, logf_l, s_fwd)
    o_b, _ = chunk_scan(flip(ql), flip(kb_l), flip(v_l), flip(logb_l), s_bwd)
    return readout(o_f + flip(o_b), g_l), o_c


def swiglu(h, w_gate, w_up, w_down):
    return (jax.nn.silu(h @ w_gate) * (h @ w_up)) @ w_down


def moe_swiglu(h, router, w_gate, w_up, w_down):
    B, L, D = h.shape
    t = h.reshape(B * L, D)
    logits = (t @ router).astype(jnp.float32)
    top_logit, top_idx = lax.top_k(logits, TOP_K)
    top_w = jax.nn.softmax(top_logit, axis=-1)
    gates = jnp.einsum("nk,nke->ne", top_w, jax.nn.one_hot(top_idx, N_EXPERTS, dtype=jnp.float32)).astype(h.dtype)
    out = jnp.zeros_like(t)
    for e in range(N_EXPERTS):
        out = out + gates[:, e:e + 1] * swiglu(t, w_gate[e], w_up[e], w_down[e])
    return out.reshape(B, L, D)


def setup_inputs(seed: int = 0) -> dict:
    key = jax.random.key(seed)
    ks = jax.random.split(key, 26)
    f32 = jnp.float32
    D = D_MODEL

    def normal(i, shape, scale):
        return scale * jax.random.normal(ks[i], shape, f32)

    def gain(i, shape):
        return 1.0 + 0.05 * jax.random.normal(ks[i], shape, f32)

    return {
        "x": normal(0, (BATCH, SEQ, D), 1.0),
        "c": normal(1, (BATCH, D), 1.0),
        "ctx": normal(2, (BATCH, CTX_LEN, D), 1.0),
        "c_ctx": normal(3, (D,), 1.0),
        "w_mod": normal(4, (DEPTH, D, 6 * D), 0.5 * D ** -0.5),
        "b_mod": normal(5, (DEPTH, 6 * D), 0.02),
        "norm_mix": gain(6, (DEPTH, D)),
        "norm_ffn": gain(7, (DEPTH, D)),
        "norm_final": gain(8, (D,)),
        "w_in": normal(9, (DEPTH, D, IN_COLS), D ** -0.5),
        "w_out": normal(10, (DEPTH, D_MIX, D), D_MIX ** -0.5),
        "diff_lambda": normal(11, (DEPTH, 4, DA_HEAD_DIM), 0.1),
        "diff_norm": gain(12, (DEPTH, 2 * DA_HEAD_DIM)),
        "mla_q_norm": gain(13, (DEPTH, MLA_Q_RANK)),
        "mla_w_uq": normal(14, (DEPTH, MLA_Q_RANK, MLA_HEADS * (MLA_NOPE + MLA_ROPE)), MLA_Q_RANK ** -0.5),
        "mla_kv_norm": gain(15, (DEPTH, MLA_KV_RANK)),
        "mla_w_ukv": normal(16, (DEPTH, MLA_KV_RANK, MLA_HEADS * (MLA_NOPE + MLA_V)), MLA_KV_RANK ** -0.5),
        "hgrn_lb_logits": normal(17, (2, DEPTH, HG_WIDTH), 0.5),
        "hgrn_norm": gain(18, (DEPTH, HG_VAL)),
        "ffn_w_gate": normal(19, (N_DENSE, D, D_FF), D ** -0.5),
        "ffn_w_up": normal(20, (N_DENSE, D, D_FF), D ** -0.5),
        "ffn_w_down": normal(21, (N_DENSE, D_FF, D), D_FF ** -0.5),
        "moe_router": normal(22, (N_MOE, D, N_EXPERTS), D ** -0.5),
        "moe_w_gate": normal(23, (N_MOE, N_EXPERTS, D, D_FF_EXPERT), D ** -0.5),
        "moe_w_up": normal(24, (N_MOE, N_EXPERTS, D, D_FF_EXPERT), D ** -0.5),
        "moe_w_down": normal(25, (N_MOE, N_EXPERTS, D_FF_EXPERT, D), D_FF_EXPERT ** -0.5),
    }


def reference(x, c, ctx, c_ctx, w_mod, b_mod, norm_mix, norm_ffn, norm_final, w_in, w_out,
              diff_lambda, diff_norm, mla_q_norm, mla_w_uq, mla_kv_norm, mla_w_ukv,
              hgrn_lb_logits, hgrn_norm, ffn_w_gate, ffn_w_up, ffn_w_down,
              moe_router, moe_w_gate, moe_w_up, moe_w_down):
    L = x.shape[1]
    rope_da = axial_rope_tables(L, DA_HEAD_DIM, x.dtype)
    rope_mla = axial_rope_tables(L, MLA_ROPE, x.dtype)
    lb = jnp.cumsum(jax.nn.softmax(hgrn_lb_logits.astype(jnp.float32), axis=1), axis=1)
    lb = lb - lb[:, :1]
    cond = jax.nn.silu(c)
    cond_ctx = jax.nn.silu(c_ctx)
    xc = ctx
    for l in range(DEPTH):
        need_ctx = l < DEPTH - 1
        mod = [m[:, None, :] for m in jnp.split(cond @ w_mod[l] + b_mod[l], 6, axis=-1)]
        mod_c = jnp.split(cond_ctx @ w_mod[l] + b_mod[l], 6, axis=-1)
        lam_init = 0.8 - 0.6 * math.exp(-0.3 * l)

        h = modulate(rmsnorm(x, norm_mix[l]), mod[0], mod[1])
        hc = modulate(rmsnorm(xc, norm_mix[l]), mod_c[0], mod_c[1])
        pl = split_columns(h @ w_in[l])
        pc = split_columns(hc @ w_in[l])
        oa, oa_c = diff_attention(pl[0:3], pc[0:3], diff_lambda[l], diff_norm[l], lam_init, rope_da, need_ctx)
        ob, ob_c = latent_attention(pl[3:6], pc[3:6], mla_q_norm[l], mla_w_uq[l], mla_kv_norm[l], mla_w_ukv[l], rope_mla, need_ctx)
        og, og_c = hgrn2(pl[6:11], pc[6:11], lb[0, l], lb[1, l], hgrn_norm[l], need_ctx)
        x = x + mod[2] * (jnp.concatenate([oa, ob, og], axis=-1) @ w_out[l])

        h = modulate(rmsnorm(x, norm_ffn[l]), mod[3], mod[4])
        if need_ctx:
            xc = xc + mod_c[2] * (jnp.concatenate([oa_c, ob_c, og_c], axis=-1) @ w_out[l])
            hc = modulate(rmsnorm(xc, norm_ffn[l]), mod_c[3], mod_c[4])
            h = jnp.concatenate([hc, h], axis=1)
        if l % 2 == 0:
            y = swiglu(h, ffn_w_gate[l // 2], ffn_w_up[l // 2], ffn_w_down[l // 2])
        else:
            y = moe_swiglu(h, moe_router[l // 2], moe_w_gate[l // 2], moe_w_up[l // 2], moe_w_down[l // 2])
        if need_ctx:
            n_ctx = xc.shape[1]
            xc = xc + mod_c[5] * y[:, :n_ctx]
            y = y[:, n_ctx:]
        x = x + mod[5] * y
    return rmsnorm(x, norm_final)
```

```python
import functools
import math

import jax
import jax.numpy as jnp
from jax import lax
from jax.experimental import pallas as pl
from jax.experimental.pallas import tpu as pltpu

F32 = jnp.float32
BF16 = jnp.bfloat16

GRID_W = 64
ROPE_BASE = 10000.0
DA_HEADS = 4
DA_HEAD_DIM = 64
MLA_HEADS = 4
MLA_Q_RANK = 256
MLA_KV_RANK = 128
MLA_NOPE = 64
MLA_ROPE = 32
MLA_V = 64
HG_HEADS = 4
HG_KEY = 64
HG_WIDTH = HG_HEADS * HG_KEY
N_EXPERTS = 8

LANES = 128
MXU_COLS = 256
VMEM_LIMIT = 56 * 1024 * 1024

HG_BLOCK = 16
HG_TILE = 128
ATT_SUB = 256
FF_CHUNK = 256


def _dot(a, b):
    return jnp.dot(a, b, preferred_element_type=F32)


def _dot_nt(a, b):
    return lax.dot_general(a, b, (((1,), (1,)), ((), ())), preferred_element_type=F32)


def _split3(x):
    x0 = x.astype(BF16)
    r = x - x0.astype(F32)
    x1 = r.astype(BF16)
    x2 = (r - x1.astype(F32)).astype(BF16)
    return x0, x1, x2


def _dot_exact_lhs(a_bf16, x):
    x0, x1, x2 = _split3(x)
    return _dot(a_bf16, x0) + _dot(a_bf16, x1) + _dot(a_bf16, x2)


def _dot_exact_rhs(x, b_bf16):
    x0, x1, x2 = _split3(x)
    return _dot(x0, b_bf16) + _dot(x1, b_bf16) + _dot(x2, b_bf16)


def _dot_f32(a, b):
    a0 = a.astype(BF16)
    a1 = (a - a0.astype(F32)).astype(BF16)
    b0 = b.astype(BF16)
    b1 = (b - b0.astype(F32)).astype(BF16)
    return _dot(a0, b0) + _dot(a0, b1) + _dot(a1, b0)


def _sigmoid(z):
    return 1.0 / (1.0 + jnp.exp(-z))


def _row_tile(t):
    return 768 if t % 768 == 0 else 256


def _cparams(sem):
    return pltpu.CompilerParams(dimension_semantics=sem, vmem_limit_bytes=VMEM_LIMIT)


def _resident(shape):
    nd = len(shape)
    return pl.BlockSpec(shape, lambda *_: (0,) * nd, pipeline_mode=pl.Buffered(1))


def _mod_kernel(c_ref, w_ref, b_ref, o_ref):
    c = c_ref[...]
    o_ref[0] = _dot_f32(c * _sigmoid(c), w_ref[0]) + b_ref[0]


def _mod_call(c_all, w_mod, b_mod):
    depth, d, n6 = w_mod.shape
    bp = c_all.shape[0]
    tn = 1536
    return pl.pallas_call(
        _mod_kernel,
        grid=(depth, n6 // tn),
        in_specs=[
            pl.BlockSpec((bp, d), lambda l, j: (0, 0)),
            pl.BlockSpec((1, d, tn), lambda l, j: (l, 0, j)),
            pl.BlockSpec((1, 1, tn), lambda l, j: (l, 0, j)),
        ],
        out_specs=pl.BlockSpec((1, bp, tn), lambda l, j: (l, 0, j)),
        out_shape=jax.ShapeDtypeStruct((depth, bp, n6), F32),
        compiler_params=_cparams(("parallel", "parallel")),
        name="modulation",
    )(c_all, w_mod, b_mod.reshape(depth, 1, n6))


def _rope128(seg, cos, sin, mla):
    lane = lax.broadcasted_iota(jnp.int32, (1, LANES), 1)
    if mla:
        half = MLA_ROPE // 2
        take_next = lane < MLA_NOPE + half
    else:
        half = DA_HEAD_DIM // 2
        take_next = (lane % DA_HEAD_DIM) < half
    sw = jnp.where(take_next, pltpu.roll(seg, LANES - half, 1), pltpu.roll(seg, half, 1))
    return seg * cos + sw * sin


def _in_kernel(x_ref, gain_ref, ml_ref, mc_ref, w_ref, cd_ref, sd_ref, cm_ref, sm_ref,
               qn_ref, kvn_ref, wuq_ref, wkv_ref,
               oq_ref, ok_ref, ov_ref, mq_ref, mk_ref, mv_ref, hg_ref, *, tm, n_ctx):
    i = pl.program_id(1)
    x = x_ref[0]
    y = x * lax.rsqrt(jnp.mean(x * x, axis=-1, keepdims=True) + 1e-6) * gain_ref[...]
    row = i * tm + lax.broadcasted_iota(jnp.int32, (tm, 1), 0)
    isctx = row < n_ctx
    ml = ml_ref[0]
    mc = mc_ref[0]
    shift = jnp.where(isctx, mc[0:1], ml[0:1])
    scale = jnp.where(isctx, mc[1:2], ml[1:2])
    h = (y * (1.0 + scale) + shift).astype(BF16)

    cd, sd = cd_ref[...], sd_ref[...]
    cm, sm = cm_ref[...], sm_ref[...]

    da_scale = DA_HEAD_DIM ** -0.5
    aq = _dot(h, w_ref[:, 0:512])
    for g in range(4):
        sl = slice(g * LANES, (g + 1) * LANES)
        oq_ref[0, :, sl] = (_rope128(aq[:, sl], cd, sd, False) * da_scale).astype(BF16)
    ak = _dot(h, w_ref[:, 512:1024])
    for g in range(4):
        sl = slice(g * LANES, (g + 1) * LANES)
        ok_ref[0, :, sl] = _rope128(ak[:, sl], cd, sd, False).astype(BF16)
    ov_ref[0] = _dot(h, w_ref[:, 1024:1536]).astype(BF16)

    mla_scale = (MLA_NOPE + MLA_ROPE) ** -0.5
    cc = _dot(h, w_ref[:, 1536:2048])
    cq = cc[:, 0:MLA_Q_RANK]
    qn = (cq * lax.rsqrt(jnp.mean(cq * cq, axis=-1, keepdims=True) + 1e-6) * qn_ref[...]).astype(BF16)
    qu = _dot(qn, wuq_ref[...])
    for g in range(4):
        sl = slice(g * LANES, (g + 1) * LANES)
        mq_ref[0, :, sl] = (_rope128(qu[:, sl], cm, sm, True) * mla_scale).astype(BF16)
    ckv = cc[:, MLA_Q_RANK:MLA_Q_RANK + MLA_KV_RANK]
    rest = cc[:, MLA_Q_RANK + MLA_KV_RANK:]
    kvn = ckv * lax.rsqrt(jnp.mean(ckv * ckv, axis=-1, keepdims=True) + 1e-6) * kvn_ref[...]
    t2 = jnp.concatenate([kvn, rest], axis=1).astype(BF16)
    kv = _dot(t2, wkv_ref[...])
    for g in range(4):
        sl = slice(g * LANES, (g + 1) * LANES)
        mk_ref[0, :, sl] = _rope128(kv[:, sl], cm, sm, True).astype(BF16)
    mv_ref[0] = kv[:, 512:768].astype(BF16)

    hg_ref[0] = _dot(h, w_ref[:, 2048:3328])


def _in_call(xa, gain, mods, w_in_p, tabs, qn, kvn, wuq_p, wkv_p, n_ctx):
    b, t, d = xa.shape
    tm = _row_tile(t)
    cd, sd, cm, sm = tabs
    row_blk = lambda width: pl.BlockSpec((1, tm, width), lambda bi, i: (bi, i, 0))
    tab_blk = pl.BlockSpec((tm, LANES), lambda bi, i: (i, 0))
    out_w = (512, 512, 512, 512, 512, 256)
    return pl.pallas_call(
        functools.partial(_in_kernel, tm=tm, n_ctx=n_ctx),
        grid=(b, t // tm),
        in_specs=[
            row_blk(d),
            _resident((1, d)),
            pl.BlockSpec((1, 6, d), lambda bi, i: (bi, 0, 0)),
            pl.BlockSpec((1, 6, d), lambda bi, i: (b, 0, 0)),
            _resident(w_in_p.shape),
            tab_blk, tab_blk, tab_blk, tab_blk,
            _resident((1, MLA_Q_RANK)),
            _resident((1, MLA_KV_RANK)),
            _resident(wuq_p.shape),
            _resident(wkv_p.shape),
        ],
        out_specs=[row_blk(w) for w in out_w] + [row_blk(5 * HG_WIDTH)],
        out_shape=[jax.ShapeDtypeStruct((b, t, w), BF16) for w in out_w]
        + [jax.ShapeDtypeStruct((b, t, 5 * HG_WIDTH), F32)],
        compiler_params=_cparams(("parallel", "parallel")),
        name="in_proj",
    )(xa, gain, mods, mods, w_in_p, cd, sd, cm, sm, qn, kvn, wuq_p, wkv_p)


def _softmax_parts(s):
    m = jnp.max(s, axis=-1, keepdims=True)
    p = jnp.exp(s - m)
    return p, 1.0 / jnp.sum(p, axis=-1, keepdims=True)


def _da_kernel(q_ref, k_ref, v_ref, lamv_ref, lami_ref, gain_ref, o_ref, *, tq, n_ctx):
    qi = pl.program_id(2)
    lane = lax.broadcasted_iota(jnp.int32, (1, LANES), 1)
    map1 = jnp.where(lane < DA_HEAD_DIM, 1.0, 0.0).astype(BF16)
    map2 = jnp.where(lane < DA_HEAD_DIM, 0.0, 1.0).astype(BF16)
    lv = lamv_ref[...]
    lam = (jnp.exp(jnp.sum(lv[0:1] * lv[1:2], axis=-1, keepdims=True))
           - jnp.exp(jnp.sum(lv[2:3] * lv[3:4], axis=-1, keepdims=True)) + lami_ref[...])
    gain = gain_ref[...]

    def attend(q, k, v):
        p1, r1 = _softmax_parts(_dot_nt(q * map1, k))
        p2, r2 = _softmax_parts(_dot_nt(q * map2, k))
        p = p1 * r1 - p2 * (lam * r2)
        o = _dot(p.astype(BF16), v)
        return (o * lax.rsqrt(jnp.mean(o * o, axis=-1, keepdims=True) + 1e-5) * gain).astype(BF16)

    def latent(j, carry):
        rows = pl.ds(pl.multiple_of(j * ATT_SUB, ATT_SUB), ATT_SUB)
        o_ref[0, rows, :] = attend(q_ref[0, rows, :], k_ref[0], v_ref[0])
        return carry

    n_sub = tq // ATT_SUB
    ctx_sub = n_ctx // ATT_SUB
    first_latent = jnp.where(qi == 0, ctx_sub, 0)

    @pl.when(qi == 0)
    def _():
        for j in range(ctx_sub):
            rows = slice(j * ATT_SUB, (j + 1) * ATT_SUB)
            o_ref[0, rows, :] = attend(q_ref[0, rows, :], k_ref[0, 0:n_ctx, :], v_ref[0, 0:n_ctx, :])

    lax.fori_loop(first_latent, n_sub, latent, 0)


def _da_call(q, k, v, lam_vec, lam_init, gain_eff, n_ctx):
    b, t, _ = q.shape
    tq = _row_tile(t)
    return pl.pallas_call(
        functools.partial(_da_kernel, tq=tq, n_ctx=n_ctx),
        grid=(b, DA_HEADS, t // tq),
        in_specs=[
            pl.BlockSpec((1, tq, LANES), lambda bi, h, i: (bi, i, h)),
            pl.BlockSpec((1, t, LANES), lambda bi, h, i: (bi, 0, h)),
            pl.BlockSpec((1, t, LANES), lambda bi, h, i: (bi, 0, h)),
            pl.BlockSpec((4, DA_HEAD_DIM), lambda bi, h, i: (0, 0)),
            pl.BlockSpec((1, 1), lambda bi, h, i: (0, 0)),
            pl.BlockSpec((1, LANES), lambda bi, h, i: (0, 0)),
        ],
        out_specs=pl.BlockSpec((1, tq, LANES), lambda bi, h, i: (bi, i, h)),
        out_shape=jax.ShapeDtypeStruct((b, t, DA_HEADS * LANES), BF16),
        compiler_params=_cparams(("parallel", "parallel", "parallel")),
        name="diff_attention",
    )(q, k, v, lam_vec, lam_init, gain_eff)


def _mla_kernel(q_ref, k_ref, v_ref, o_ref, *, tq, n_ctx):
    qi = pl.program_id(1)
    vlane = lax.broadcasted_iota(jnp.int32, (1, MLA_HEADS * MLA_V), 1) // MLA_V

    def attend(q, k, v):
        o = None
        for h in range(MLA_HEADS):
            sl = slice(h * LANES, (h + 1) * LANES)
            p, r = _softmax_parts(_dot_nt(q[:, sl], k[:, sl]))
            vh = v * jnp.where(vlane == h, 1.0, 0.0).astype(BF16)
            oh = _dot((p * r).astype(BF16), vh)
            o = oh if o is None else o + oh
        return o.astype(BF16)

    def latent(j, carry):
        rows = pl.ds(pl.multiple_of(j * ATT_SUB, ATT_SUB), ATT_SUB)
        o_ref[0, rows, :] = attend(q_ref[0, rows, :], k_ref[0], v_ref[0])
        return carry

    n_sub = tq // ATT_SUB
    ctx_sub = n_ctx // ATT_SUB
    first_latent = jnp.where(qi == 0, ctx_sub, 0)

    @pl.when(qi == 0)
    def _():
        for j in range(ctx_sub):
            rows = slice(j * ATT_SUB, (j + 1) * ATT_SUB)
            o_ref[0, rows, :] = attend(q_ref[0, rows, :], k_ref[0, 0:n_ctx, :], v_ref[0, 0:n_ctx, :])

    lax.fori_loop(first_latent, n_sub, latent, 0)


def _mla_call(q, k, v, n_ctx):
    b, t, _ = q.shape
    tq = _row_tile(t)
    wv = MLA_HEADS * MLA_V
    return pl.pallas_call(
        functools.partial(_mla_kernel, tq=tq, n_ctx=n_ctx),
        grid=(b, t // tq),
        in_specs=[
            pl.BlockSpec((1, tq, MLA_HEADS * LANES), lambda bi, i: (bi, i, 0)),
            pl.BlockSpec((1, t, MLA_HEADS * LANES), lambda bi, i: (bi, 0, 0)),
            pl.BlockSpec((1, t, wv), lambda bi, i: (bi, 0, 0)),
        ],
        out_specs=pl.BlockSpec((1, tq, wv), lambda bi, i: (bi, i, 0)),
        out_shape=jax.ShapeDtypeStruct((b, t, wv), BF16),
        compiler_params=_cparams(("parallel", "parallel")),
        name="latent_attention",
    )(q, k, v)


def _hgrn_kernel(hg_ref, lbf_ref, lbb_ref, gain_ref, o_ref,
                 kf_s, bf_s, kb_s, cb_s, v_s, qtf_s, ktf_s, qtb_s, ktb_s,
                 df_s, db_s, vt_s, oacc_s, stf_s, stb_s, *, t_len, n_ctx):
    w = HG_WIDTH
    rt = HG_TILE
    blk = HG_BLOCK
    pad = blk
    n_tiles = t_len // rt
    blk_per_tile = rt // blk

    r_i = lax.broadcasted_iota(jnp.int32, (rt, rt), 0)
    c_i = lax.broadcasted_iota(jnp.int32, (rt, rt), 1)
    same = (r_i // blk) == (c_i // blk)
    incl_lo = jnp.where(same & (c_i <= r_i), 1.0, 0.0).astype(BF16)
    incl_hi = jnp.where(same & (c_i >= r_i), 1.0, 0.0).astype(BF16)
    blk_sum = jnp.where(same, 1.0, 0.0).astype(BF16)
    sel_rows = 16
    g_r = lax.broadcasted_iota(jnp.int32, (sel_rows, rt), 0)
    g_c = lax.broadcasted_iota(jnp.int32, (sel_rows, rt), 1)
    blk_sel = jnp.where((g_c // blk) == g_r, 1.0, 0.0).astype(BF16)
    h_r = lax.broadcasted_iota(jnp.int32, (w, w), 0) // HG_KEY
    h_c = lax.broadcasted_iota(jnp.int32, (w, w), 1) // HG_KEY
    head_ones = jnp.where(h_r == h_c, 1.0, 0.0).astype(BF16)
    head_mask = jnp.where(h_r == h_c, 1.0, 0.0).astype(F32)
    pos = lax.broadcasted_iota(jnp.int32, (rt, 1), 0) % blk

    lbf = lbf_ref[...]
    lbb = lbb_ref[...]

    zpad = jnp.zeros((pad, w), F32)
    for s in (kf_s, bf_s, kb_s, cb_s, v_s):
        s[0:pad, :] = zpad
        s[pad + rt:pad + rt + pad, :] = zpad

    def gates(z, lb):
        f = lb + (1.0 - lb) * _sigmoid(z)
        return (1.0 - lb) * _sigmoid(-z), jnp.log(f)

    def tile_body(ti, carry):
        rows = pl.ds(pl.multiple_of(ti * rt, rt), rt)
        q = hg_ref[0, rows, 0:w] * (HG_KEY ** -0.5)
        v = hg_ref[0, rows, w:2 * w]
        kf, lf = gates(hg_ref[0, rows, 2 * w:3 * w], lbf)
        kb, lg = gates(hg_ref[0, rows, 3 * w:4 * w], lbb)
        b_f = _dot_exact_lhs(incl_lo, lf)
        tot_f = _dot_exact_lhs(blk_sum, lf)
        c_b = _dot_exact_lhs(incl_hi, lg)
        tot_b = _dot_exact_lhs(blk_sum, lg)
        qtf_s[rows, :] = (q * jnp.exp(b_f)).astype(BF16)
        ktf_s[rows, :] = (kf * jnp.exp(tot_f - b_f)).astype(BF16)
        qtb_s[rows, :] = (q * jnp.exp(c_b)).astype(BF16)
        ktb_s[rows, :] = (kb * jnp.exp(tot_b - c_b)).astype(BF16)
        drows = pl.ds(pl.multiple_of(ti * blk_per_tile, blk_per_tile), blk_per_tile)
        df_s[drows, :] = jnp.exp(_dot_exact_lhs(blk_sel, lf)[0:blk_per_tile])
        db_s[drows, :] = jnp.exp(_dot_exact_lhs(blk_sel, lg)[0:blk_per_tile])
        vt_s[:, rows] = v.T.astype(BF16)

        kf_s[pad:pad + rt, :] = kf
        bf_s[pad:pad + rt, :] = b_f
        kb_s[pad:pad + rt, :] = kb
        cb_s[pad:pad + rt, :] = c_b
        v_s[pad:pad + rt, :] = v
        od = jnp.zeros((rt, w), F32)
        for dl in range(blk):
            lo = pad - dl
            e = q * kf_s[lo:lo + rt, :] * jnp.exp(jnp.minimum(b_f - bf_s[lo:lo + rt, :], 0.0))
            e = jnp.where(pos >= dl, e, 0.0)
            od = od + _dot(e.astype(BF16), head_ones) * v_s[lo:lo + rt, :]
            hi = pad + dl
            e = q * kb_s[hi:hi + rt, :] * jnp.exp(jnp.minimum(c_b - cb_s[hi:hi + rt, :], 0.0))
            e = jnp.where(pos + dl < blk, e, 0.0)
            od = od + _dot(e.astype(BF16), head_ones) * v_s[hi:hi + rt, :]
        oacc_s[rows, :] = od
        return carry

    lax.fori_loop(0, n_tiles, tile_body, 0)

    stf_s[...] = jnp.zeros((w, w), F32)
    stb_s[...] = jnp.zeros((w, w), F32)
    n_blk = t_len // blk
    n_cblk = n_ctx // blk
    lane = lax.broadcasted_iota(jnp.int32, (1, LANES), 1)

    def advance(bi, qt_s, kt_s, d_s, st_s):
        r = pl.multiple_of(bi * blk, blk)
        st = st_s[...]
        rows = pl.ds(r, blk)
        oacc_s[rows, :] = oacc_s[rows, :] + _dot_nt(qt_s[rows, :], st.astype(BF16))
        slab = pl.multiple_of((r // LANES) * LANES, LANES)
        off = r - slab
        vt = vt_s[:, pl.ds(slab, LANES)]
        vt = vt * jnp.where((lane >= off) & (lane < off + blk), 1.0, 0.0).astype(BF16)
        upd = _dot(vt, kt_s[pl.ds(slab, LANES), :])
        st_s[...] = st * d_s[pl.ds(bi, 1), :] + upd * head_mask

    def step(i, carry):
        advance(i, qtf_s, ktf_s, df_s, stf_s)
        jb = jnp.where(i < n_cblk, n_cblk - 1 - i, n_blk - 1 - (i - n_cblk))
        advance(jb, qtb_s, ktb_s, db_s, stb_s)
        return carry

    lax.fori_loop(0, n_blk, step, 0)

    gain = gain_ref[...]

    def out_body(ti, carry):
        rows = pl.ds(pl.multiple_of(ti * rt, rt), rt)
        o = oacc_s[rows, :]
        ms = _dot_exact_rhs(o * o, head_ones) * (1.0 / HG_KEY)
        y = o * lax.rsqrt(ms + 1e-6) * gain
        g = hg_ref[0, rows, 4 * w:5 * w]
        o_ref[0, rows, :] = (y * (g * _sigmoid(g))).astype(BF16)
        return carry

    lax.fori_loop(0, n_tiles, out_body, 0)


def _hgrn_call(hg, lbf, lbb, gain_t, n_ctx):
    b, t, _ = hg.shape
    w = HG_WIDTH
    tile_pad = pltpu.VMEM((HG_TILE + 2 * HG_BLOCK, w), F32)
    seq_bf16 = pltpu.VMEM((t, w), BF16)
    dec = pltpu.VMEM((t // HG_BLOCK, w), F32)
    return pl.pallas_call(
        functools.partial(_hgrn_kernel, t_len=t, n_ctx=n_ctx),
        grid=(b,),
        in_specs=[
            pl.BlockSpec((1, t, 5 * w), lambda bi: (bi, 0, 0)),
            pl.BlockSpec((1, w), lambda bi: (0, 0)),
            pl.BlockSpec((1, w), lambda bi: (0, 0)),
            pl.BlockSpec((1, w), lambda bi: (0, 0)),
        ],
        out_specs=pl.BlockSpec((1, t, w), lambda bi: (bi, 0, 0)),
        out_shape=jax.ShapeDtypeStruct((b, t, w), BF16),
        scratch_shapes=[tile_pad] * 5 + [seq_bf16] * 4 + [dec, dec,
                        pltpu.VMEM((w, t), BF16), pltpu.VMEM((t, w), F32),
                        pltpu.VMEM((w, w), F32), pltpu.VMEM((w, w), F32)],
        compiler_params=_cparams(("parallel",)),
        name="hgrn2",
    )(hg, lbf, lbb, gain_t)


def _mod_rows(ml_ref, mc_ref, tm, n_ctx, idx):
    i = pl.program_id(1)
    row = i * tm + lax.broadcasted_iota(jnp.int32, (tm, 1), 0)
    isctx = row < n_ctx
    ml = ml_ref[0]
    mc = mc_ref[0]
    return [jnp.where(isctx, mc[k:k + 1], ml[k:k + 1]) for k in idx]


def _out_kernel(oa_ref, ob_ref, og_ref, wo_ref, x_ref, ml_ref, mc_ref, gain_ref, *rest,
                tm, n_ctx, routed):
    if routed:
        router_ref, xo_ref, h_ref, gates_ref = rest
    else:
        xo_ref, h_ref = rest
    gate, shift, scale = _mod_rows(ml_ref, mc_ref, tm, n_ctx, (2, 3, 4))
    mix = (_dot(oa_ref[0], wo_ref[0:512, :]) + _dot(ob_ref[0], wo_ref[512:768, :])
           + _dot(og_ref[0], wo_ref[768:1024, :]))
    x = x_ref[0] + gate * mix
    xo_ref[0] = x
    y = x * lax.rsqrt(jnp.mean(x * x, axis=-1, keepdims=True) + 1e-6) * gain_ref[...]
    h = y * (1.0 + scale) + shift
    h_ref[0] = h.astype(BF16)
    if routed:
        lane = lax.broadcasted_iota(jnp.int32, (1, LANES), 1)
        logits = _dot_f32(h, router_ref[...])
        logits = jnp.where(lane < N_EXPERTS, logits, -jnp.inf)
        m1 = jnp.max(logits, axis=-1, keepdims=True)
        i1 = jnp.min(jnp.where(logits == m1, lane, LANES), axis=-1, keepdims=True)
        rest_l = jnp.where(lane == i1, -jnp.inf, logits)
        m2 = jnp.max(rest_l, axis=-1, keepdims=True)
        i2 = jnp.min(jnp.where(rest_l == m2, lane, LANES), axis=-1, keepdims=True)
        e2 = jnp.exp(m2 - m1)
        w1 = 1.0 / (1.0 + e2)
        gates_ref[0] = jnp.where(lane == i1, w1, 0.0) + jnp.where(lane == i2, e2 * w1, 0.0)


def _out_call(oa, ob, og, wo, xa, mods, gain, router_p, n_ctx):
    b, t, d = xa.shape
    tm = _row_tile(t)
    routed = router_p is not None
    row_blk = lambda width: pl.BlockSpec((1, tm, width), lambda bi, i: (bi, i, 0))
    in_specs = [
        row_blk(oa.shape[-1]), row_blk(ob.shape[-1]), row_blk(og.shape[-1]),
        _resident(wo.shape),
        row_blk(d),
        pl.BlockSpec((1, 6, d), lambda bi, i: (bi, 0, 0)),
        pl.BlockSpec((1, 6, d), lambda bi, i: (b, 0, 0)),
        _resident((1, d)),
    ]
    args = [oa, ob, og, wo, xa, mods, mods, gain]
    out_specs = [row_blk(d), row_blk(d)]
    out_shape = [jax.ShapeDtypeStruct((b, t, d), F32), jax.ShapeDtypeStruct((b, t, d), BF16)]
    if routed:
        in_specs.append(_resident(router_p.shape))
        args.append(router_p)
        out_specs.append(row_blk(LANES))
        out_shape.append(jax.ShapeDtypeStruct((b, t, LANES), F32))
    return pl.pallas_call(
        functools.partial(_out_kernel, tm=tm, n_ctx=n_ctx, routed=routed),
        grid=(b, t // tm),
        in_specs=in_specs,
        out_specs=out_specs,
        out_shape=out_shape,
        compiler_params=_cparams(("parallel", "parallel")),
        name="out_proj_routed" if routed else "out_proj",
    )(*args)


def _swiglu_into(acc_ref, h, wg_ref, wu_ref, wd_ref, row_gate, d_ff):
    for c in range(d_ff // FF_CHUNK):
        cs = slice(c * FF_CHUNK, (c + 1) * FF_CHUNK)
        g = _dot(h, wg_ref[0, :, cs])
        u = _dot(h, wu_ref[0, :, cs])
        a = g * _sigmoid(g) * u
        if row_gate is not None:
            a = a * row_gate
        acc_ref[...] += _dot(a.astype(BF16), wd_ref[0, cs, :])


def _ffn_kernel(h_ref, x_ref, ml_ref, mc_ref, wg_ref, wu_ref, wd_ref, xo_ref, acc_ref,
                *, tm, n_ctx, d_ff):
    (gate,) = _mod_rows(ml_ref, mc_ref, tm, n_ctx, (5,))
    acc_ref[...] = jnp.zeros_like(acc_ref)
    _swiglu_into(acc_ref, h_ref[0], wg_ref, wu_ref, wd_ref, None, d_ff)
    xo_ref[0] = x_ref[0] + gate * acc_ref[...]


def _ffn_call(h, xa, mods, wg, wu, wd, n_ctx):
    b, t, d = xa.shape
    d_ff = wg.shape[-1]
    tm = _row_tile(t)
    row_blk = pl.BlockSpec((1, tm, d), lambda bi, i: (bi, i, 0))
    return pl.pallas_call(
        functools.partial(_ffn_kernel, tm=tm, n_ctx=n_ctx, d_ff=d_ff),
        grid=(b, t // tm),
        in_specs=[
            row_blk, row_blk,
            pl.BlockSpec((1, 6, d), lambda bi, i: (bi, 0, 0)),
            pl.BlockSpec((1, 6, d), lambda bi, i: (b, 0, 0)),
            _resident((1, d, d_ff)), _resident((1, d, d_ff)), _resident((1, d_ff, d)),
        ],
        out_specs=row_blk,
        out_shape=jax.ShapeDtypeStruct((b, t, d), F32),
        scratch_shapes=[pltpu.VMEM((tm, d), F32)],
        compiler_params=_cparams(("parallel", "parallel")),
        name="ffn_dense",
    )(h, xa, mods, mods, wg[None], wu[None], wd[None])


def _moe_kernel(h_ref, x_ref, gates_ref, ml_ref, mc_ref, wg_ref, wu_ref, wd_ref, xo_ref, acc_ref,
                *, tm, n_ctx, d_ff):
    e = pl.program_id(2)

    @pl.when(e == 0)
    def _():
        acc_ref[...] = jnp.zeros_like(acc_ref)

    lane = lax.broadcasted_iota(jnp.int32, (1, LANES), 1)
    row_gate = jnp.sum(jnp.where(lane == e, gates_ref[0], 0.0), axis=-1, keepdims=True)
    _swiglu_into(acc_ref, h_ref[0], wg_ref, wu_ref, wd_ref, row_gate, d_ff)

    @pl.when(e == pl.num_programs(2) - 1)
    def _():
        (gate,) = _mod_rows(ml_ref, mc_ref, tm, n_ctx, (5,))
        xo_ref[0] = x_ref[0] + gate * acc_ref[...]


def _moe_call(h, xa, gates, mods, wg, wu, wd, n_ctx):
    b, t, d = xa.shape
    n_exp, _, d_ff = wg.shape
    tm = _row_tile(t)
    row_blk = lambda width: pl.BlockSpec((1, tm, width), lambda bi, i, e: (bi, i, 0))
    return pl.pallas_call(
        functools.partial(_moe_kernel, tm=tm, n_ctx=n_ctx, d_ff=d_ff),
        grid=(b, t // tm, n_exp),
        in_specs=[
            row_blk(d), row_blk(d), row_blk(LANES),
            pl.BlockSpec((1, 6, d), lambda bi, i, e: (bi, 0, 0)),
            pl.BlockSpec((1, 6, d), lambda bi, i, e: (b, 0, 0)),
            pl.BlockSpec((1, d, d_ff), lambda bi, i, e: (e, 0, 0)),
            pl.BlockSpec((1, d, d_ff), lambda bi, i, e: (e, 0, 0)),
            pl.BlockSpec((1, d_ff, d), lambda bi, i, e: (e, 0, 0)),
        ],
        out_specs=row_blk(d),
        out_shape=jax.ShapeDtypeStruct((b, t, d), F32),
        scratch_shapes=[pltpu.VMEM((tm, d), F32)],
        compiler_params=_cparams(("parallel", "parallel", "arbitrary")),
        name="moe_experts",
    )(h, xa, gates, mods, mods, wg, wu, wd)


def _final_kernel(x_ref, g_ref, o_ref):
    x = x_ref[0]
    o_ref[0] = x * lax.rsqrt(jnp.mean(x * x, axis=-1, keepdims=True) + 1e-6) * g_ref[...]


def _final_call(xa, gain, n_ctx):
    b, t, d = xa.shape
    tm = 256
    skip = n_ctx // tm
    return pl.pallas_call(
        _final_kernel,
        grid=(b, (t - n_ctx) // tm),
        in_specs=[pl.BlockSpec((1, tm, d), lambda bi, i: (bi, i + skip, 0)), _resident((1, d))],
        out_specs=pl.BlockSpec((1, tm, d), lambda bi, i: (bi, i, 0)),
        out_shape=jax.ShapeDtypeStruct((b, t - n_ctx, d), F32),
        compiler_params=_cparams(("parallel", "parallel")),
        name="final_norm",
    )(xa, gain)


def _rope_tables(seq, n_ctx):
    rows = seq // GRID_W
    row, col = jnp.meshgrid(jnp.arange(rows, dtype=F32), jnp.arange(GRID_W, dtype=F32), indexing="ij")
    row, col = row.reshape(-1), col.reshape(-1)

    def angles(rot_dim):
        n_freq = rot_dim // 4
        inv = ROPE_BASE ** (-jnp.arange(n_freq, dtype=F32) / n_freq)
        ang = jnp.concatenate([row[:, None] * inv, col[:, None] * inv], axis=-1)
        return jnp.cos(ang), jnp.sin(ang)

    def with_ctx(a, fill):
        return jnp.concatenate([jnp.full((n_ctx, a.shape[1]), fill, F32), a], axis=0)

    cos, sin = angles(DA_HEAD_DIM)
    cos_da = with_ctx(jnp.tile(cos, (1, 4)), 1.0)
    sin_da = with_ctx(jnp.tile(jnp.concatenate([-sin, sin], axis=1), (1, 2)), 0.0)
    cos, sin = angles(MLA_ROPE)
    ones = jnp.ones((seq, MLA_NOPE), F32)
    tail = jnp.ones((seq, LANES - MLA_NOPE - MLA_ROPE), F32)
    cos_m = with_ctx(jnp.concatenate([ones, cos, cos, tail], axis=1), 1.0)
    sin_m = with_ctx(jnp.concatenate([0 * ones, -sin, sin, 0 * tail], axis=1), 0.0)
    return cos_da, sin_da, cos_m, sin_m


def _pad_in_proj(w_in_l):
    d = w_in_l.shape[0]
    cut = 1536 + MLA_Q_RANK + MLA_KV_RANK + MLA_ROPE
    padw = 2048 - cut
    return jnp.concatenate([w_in_l[:, :cut], jnp.zeros((d, padw), w_in_l.dtype), w_in_l[:, cut:]],
                           axis=1).astype(BF16)


def _pad_uq(w_uq_l):
    qk = MLA_NOPE + MLA_ROPE
    w = w_uq_l.reshape(MLA_Q_RANK, MLA_HEADS, qk)
    w = jnp.pad(w, ((0, 0), (0, 0), (0, LANES - qk)))
    return w.reshape(MLA_Q_RANK, MLA_HEADS * LANES).astype(BF16)


def _pad_ukv(w_ukv_l):
    w = w_ukv_l.reshape(MLA_KV_RANK, MLA_HEADS, MLA_NOPE + MLA_V)
    k_part = jnp.pad(w[:, :, :MLA_NOPE], ((0, 0), (0, 0), (0, LANES - MLA_NOPE)))
    k_part = k_part.reshape(MLA_KV_RANK, MLA_HEADS * LANES)
    v_part = w[:, :, MLA_NOPE:].reshape(MLA_KV_RANK, MLA_HEADS * MLA_V)
    place = jnp.zeros((MLA_ROPE, MLA_HEADS, LANES), F32)
    eye = jnp.eye(MLA_ROPE, dtype=F32)
    place = place.at[:, :, MLA_NOPE:MLA_NOPE + MLA_ROPE].set(eye[:, None, :])
    place = place.reshape(MLA_ROPE, MLA_HEADS * LANES)
    top = jnp.concatenate([k_part, v_part], axis=1)
    mid = jnp.concatenate([place, jnp.zeros((MLA_ROPE, MLA_HEADS * MLA_V), F32)], axis=1)
    bot = jnp.zeros((MXU_COLS - MLA_KV_RANK - MLA_ROPE, top.shape[1]), F32)
    return jnp.concatenate([top, mid, bot], axis=0).astype(BF16)


def kernel(x, c, ctx, c_ctx, w_mod, b_mod, norm_mix, norm_ffn, norm_final, w_in, w_out, diff_lambda, diff_norm, mla_q_norm, mla_w_uq, mla_kv_norm, mla_w_ukv, hgrn_lb_logits, hgrn_norm, ffn_w_gate, ffn_w_up, ffn_w_down, moe_router, moe_w_gate, moe_w_up, moe_w_down):
    b, seq, d = x.shape
    n_ctx = ctx.shape[1]
    depth = w_mod.shape[0]
    assert d == 1024 and n_ctx % ATT_SUB == 0 and seq % ATT_SUB == 0

    xa = jnp.concatenate([ctx, x], axis=1)
    bp = -(-(b + 1) // 8) * 8
    c_all = jnp.concatenate([c, c_ctx[None, :], jnp.zeros((bp - b - 1, d), F32)], axis=0)
    mods = _mod_call(c_all, w_mod, b_mod).reshape(depth, bp, 6, d)

    lb = jnp.cumsum(jax.nn.softmax(hgrn_lb_logits.astype(F32), axis=1), axis=1)
    lb = lb - lb[:, :1]
    tabs = _rope_tables(seq, n_ctx)

    for l in range(depth):
        lam_init = 0.8 - 0.6 * math.exp(-0.3 * l)
        q_da, k_da, v_da, q_m, k_m, v_m, hg = _in_call(
            xa, norm_mix[l][None], mods[l], _pad_in_proj(w_in[l]), tabs,
            mla_q_norm[l][None], mla_kv_norm[l][None], _pad_uq(mla_w_uq[l]), _pad_ukv(mla_w_ukv[l]),
            n_ctx)
        oa = _da_call(q_da, k_da, v_da, diff_lambda[l], jnp.full((1, 1), lam_init, F32),
                      (diff_norm[l] * (1.0 - lam_init))[None], n_ctx)
        ob = _mla_call(q_m, k_m, v_m, n_ctx)
        og = _hgrn_call(hg, lb[0, l][None], lb[1, l][None], jnp.tile(hgrn_norm[l], HG_HEADS)[None], n_ctx)

        wo = w_out[l].astype(BF16)
        if l % 2 == 0:
            xa, h = _out_call(oa, ob, og, wo, xa, mods[l], norm_ffn[l][None], None, n_ctx)
            xa = _ffn_call(h, xa, mods[l], ffn_w_gate[l // 2].astype(BF16),
                           ffn_w_up[l // 2].astype(BF16), ffn_w_down[l // 2].astype(BF16), n_ctx)
        else:
            router_p = jnp.pad(moe_router[l // 2], ((0, 0), (0, LANES - N_EXPERTS)))
            xa, h, gates = _out_call(oa, ob, og, wo, xa, mods[l], norm_ffn[l][None], router_p, n_ctx)
            xa = _moe_call(h, xa, gates, mods[l], moe_w_gate[l // 2].astype(BF16),
                           moe_w_up[l // 2].astype(BF16), moe_w_down[l // 2].astype(BF16), n_ctx)
    return _final_call(xa, norm_final[None], n_ctx)
```

```python
import functools
import math

import jax
import jax.numpy as jnp
from jax import lax
from jax.experimental import pallas as pl
from jax.experimental.pallas import tpu as pltpu

F32 = jnp.float32
BF16 = jnp.bfloat16

GRID_W = 64
ROPE_BASE = 10000.0
DA_HEADS = 4
DA_HEAD_DIM = 64
MLA_HEADS = 4
MLA_Q_RANK = 256
MLA_KV_RANK = 128
MLA_NOPE = 64
MLA_ROPE = 32
MLA_V = 64
HG_HEADS = 4
HG_KEY = 64
HG_WIDTH = HG_HEADS * HG_KEY
N_EXPERTS = 8
TOP_K = 2
LOG2_E = 1.4426950408889634

LANES = 128
MXU_COLS = 256
VMEM_LIMIT = 56 * 1024 * 1024

HG_BLOCK = 16
HG_TILE = 128
ATT_SUB = 256
FF_CHUNK = 256
MOE_TILE = 512


def _dot(a, b):
    return jnp.dot(a, b, preferred_element_type=F32)


def _dot_nt(a, b):
    return lax.dot_general(a, b, (((1,), (1,)), ((), ())), preferred_element_type=F32)


def _split3(x):
    x0 = x.astype(BF16)
    r = x - x0.astype(F32)
    x1 = r.astype(BF16)
    x2 = (r - x1.astype(F32)).astype(BF16)
    return x0, x1, x2


def _dot_exact_lhs(a_bf16, x):
    x0, x1, x2 = _split3(x)
    return _dot(a_bf16, x0) + _dot(a_bf16, x1) + _dot(a_bf16, x2)


def _dot_exact_rhs(x, b_bf16):
    x0, x1, x2 = _split3(x)
    return _dot(x0, b_bf16) + _dot(x1, b_bf16) + _dot(x2, b_bf16)


def _dot_f32(a, b):
    a0 = a.astype(BF16)
    a1 = (a - a0.astype(F32)).astype(BF16)
    b0 = b.astype(BF16)
    b1 = (b - b0.astype(F32)).astype(BF16)
    return _dot(a0, b0) + _dot(a0, b1) + _dot(a1, b0)


def _sigmoid(z):
    return 1.0 / (1.0 + jnp.exp(-z))


def _row_tile(t):
    return 768 if t % 768 == 0 else 256


def _cparams(sem):
    return pltpu.CompilerParams(dimension_semantics=sem, vmem_limit_bytes=VMEM_LIMIT)


def _resident(shape):
    nd = len(shape)
    return pl.BlockSpec(shape, lambda *_: (0,) * nd, pipeline_mode=pl.Buffered(1))


def _mod_kernel(c_ref, w_ref, b_ref, o_ref):
    c = c_ref[...]
    o_ref[0] = _dot_f32(c * _sigmoid(c), w_ref[0]) + b_ref[0]


def _mod_call(c_all, w_mod, b_mod):
    depth, d, n6 = w_mod.shape
    bp = c_all.shape[0]
    tn = 1536
    return pl.pallas_call(
        _mod_kernel,
        grid=(depth, n6 // tn),
        in_specs=[
            pl.BlockSpec((bp, d), lambda l, j: (0, 0)),
            pl.BlockSpec((1, d, tn), lambda l, j: (l, 0, j)),
            pl.BlockSpec((1, 1, tn), lambda l, j: (l, 0, j)),
        ],
        out_specs=pl.BlockSpec((1, bp, tn), lambda l, j: (l, 0, j)),
        out_shape=jax.ShapeDtypeStruct((depth, bp, n6), F32),
        compiler_params=_cparams(("parallel", "parallel")),
        name="modulation",
    )(c_all, w_mod, b_mod.reshape(depth, 1, n6))


def _rope128(seg, cos, sin, mla):
    lane = lax.broadcasted_iota(jnp.int32, (1, LANES), 1)
    if mla:
        half = MLA_ROPE // 2
        take_next = lane < MLA_NOPE + half
    else:
        half = DA_HEAD_DIM // 2
        take_next = (lane % DA_HEAD_DIM) < half
    sw = jnp.where(take_next, pltpu.roll(seg, LANES - half, 1), pltpu.roll(seg, half, 1))
    return seg * cos + sw * sin


def _in_kernel(x_ref, gain_ref, ml_ref, mc_ref, w_ref, cd_ref, sd_ref, cm_ref, sm_ref,
               qn_ref, kvn_ref, wuq_ref, wkv_ref,
               oq_ref, ok_ref, ov_ref, mq_ref, mk_ref, mv_ref, hg_ref, *, tm, n_ctx):
    i = pl.program_id(1)
    x = x_ref[0]
    y = x * lax.rsqrt(jnp.mean(x * x, axis=-1, keepdims=True) + 1e-6) * gain_ref[...]
    row = i * tm + lax.broadcasted_iota(jnp.int32, (tm, 1), 0)
    isctx = row < n_ctx
    ml = ml_ref[0]
    mc = mc_ref[0]
    shift = jnp.where(isctx, mc[0:1], ml[0:1])
    scale = jnp.where(isctx, mc[1:2], ml[1:2])
    h = (y * (1.0 + scale) + shift).astype(BF16)

    cd, sd = cd_ref[...], sd_ref[...]
    cm, sm = cm_ref[...], sm_ref[...]

    da_scale = DA_HEAD_DIM ** -0.5 * LOG2_E
    aq = _dot(h, w_ref[:, 0:512])
    for g in range(4):
        sl = slice(g * LANES, (g + 1) * LANES)
        oq_ref[0, :, sl] = (_rope128(aq[:, sl], cd, sd, False) * da_scale).astype(BF16)
    ak = _dot(h, w_ref[:, 512:1024])
    for g in range(4):
        sl = slice(g * LANES, (g + 1) * LANES)
        ok_ref[0, :, sl] = _rope128(ak[:, sl], cd, sd, False).astype(BF16)
    ov_ref[0] = _dot(h, w_ref[:, 1024:1536]).astype(BF16)

    mla_scale = (MLA_NOPE + MLA_ROPE) ** -0.5 * LOG2_E
    cc = _dot(h, w_ref[:, 1536:2048])
    cq = cc[:, 0:MLA_Q_RANK]
    qn = (cq * lax.rsqrt(jnp.mean(cq * cq, axis=-1, keepdims=True) + 1e-6) * qn_ref[...]).astype(BF16)
    qu = _dot(qn, wuq_ref[...])
    for g in range(4):
        sl = slice(g * LANES, (g + 1) * LANES)
        mq_ref[0, :, sl] = (_rope128(qu[:, sl], cm, sm, True) * mla_scale).astype(BF16)
    ckv = cc[:, MLA_Q_RANK:MLA_Q_RANK + MLA_KV_RANK]
    rest = cc[:, MLA_Q_RANK + MLA_KV_RANK:]
    kvn = ckv * lax.rsqrt(jnp.mean(ckv * ckv, axis=-1, keepdims=True) + 1e-6) * kvn_ref[...]
    t2 = jnp.concatenate([kvn, rest], axis=1).astype(BF16)
    kv = _dot(t2, wkv_ref[...])
    for g in range(4):
        sl = slice(g * LANES, (g + 1) * LANES)
        mk_ref[0, :, sl] = _rope128(kv[:, sl], cm, sm, True).astype(BF16)
    mv_ref[0] = kv[:, 512:768].astype(BF16)

    hg_ref[0] = _dot(h, w_ref[:, 2048:3328])


def _in_call(xa, gain, mods, w_in_p, tabs, qn, kvn, wuq_p, wkv_p, n_ctx):
    b, t, d = xa.shape
    tm = _row_tile(t)
    cd, sd, cm, sm = tabs
    row_blk = lambda width: pl.BlockSpec((1, tm, width), lambda bi, i: (bi, i, 0))
    tab_blk = pl.BlockSpec((tm, LANES), lambda bi, i: (i, 0))
    out_w = (512, 512, 512, 512, 512, 256)
    return pl.pallas_call(
        functools.partial(_in_kernel, tm=tm, n_ctx=n_ctx),
        grid=(b, t // tm),
        in_specs=[
            row_blk(d),
            _resident((1, d)),
            pl.BlockSpec((1, 6, d), lambda bi, i: (bi, 0, 0)),
            pl.BlockSpec((1, 6, d), lambda bi, i: (b, 0, 0)),
            _resident(w_in_p.shape),
            tab_blk, tab_blk, tab_blk, tab_blk,
            _resident((1, MLA_Q_RANK)),
            _resident((1, MLA_KV_RANK)),
            _resident(wuq_p.shape),
            _resident(wkv_p.shape),
        ],
        out_specs=[row_blk(w) for w in out_w] + [row_blk(5 * HG_WIDTH)],
        out_shape=[jax.ShapeDtypeStruct((b, t, w), BF16) for w in out_w]
        + [jax.ShapeDtypeStruct((b, t, 5 * HG_WIDTH), F32)],
        compiler_params=_cparams(("parallel", "parallel")),
        name="in_proj",
    )(xa, gain, mods, mods, w_in_p, cd, sd, cm, sm, qn, kvn, wuq_p, wkv_p)


def _softmax_parts(s):
    m = jnp.max(s, axis=-1, keepdims=True)
    p = jnp.exp2(s - m)
    return p.astype(BF16), 1.0 / jnp.sum(p, axis=-1, keepdims=True)


def _da_kernel(q_ref, k_ref, v_ref, lamv_ref, lami_ref, gain_ref, o_ref, *, tq, n_ctx):
    qi = pl.program_id(2)
    lane = lax.broadcasted_iota(jnp.int32, (1, LANES), 1)
    map1 = jnp.where(lane < DA_HEAD_DIM, 1.0, 0.0).astype(BF16)
    map2 = jnp.where(lane < DA_HEAD_DIM, 0.0, 1.0).astype(BF16)
    lv = lamv_ref[...]
    lam = (jnp.exp(jnp.sum(lv[0:1] * lv[1:2], axis=-1, keepdims=True))
           - jnp.exp(jnp.sum(lv[2:3] * lv[3:4], axis=-1, keepdims=True)) + lami_ref[...])
    gain = gain_ref[...]

    def attend(q, k, v):
        p1, r1 = _softmax_parts(_dot_nt(q * map1, k))
        p2, r2 = _softmax_parts(_dot_nt(q * map2, k))
        o = _dot(p1, v) * r1 - _dot(p2, v) * (lam * r2)
        return (o * lax.rsqrt(jnp.mean(o * o, axis=-1, keepdims=True) + 1e-5) * gain).astype(BF16)

    def latent(j, carry):
        rows = pl.ds(pl.multiple_of(j * ATT_SUB, ATT_SUB), ATT_SUB)
        o_ref[0, rows, :] = attend(q_ref[0, rows, :], k_ref[0], v_ref[0])
        return carry

    n_sub = tq // ATT_SUB
    ctx_sub = n_ctx // ATT_SUB
    first_latent = jnp.where(qi == 0, ctx_sub, 0)

    @pl.when(qi == 0)
    def _():
        for j in range(ctx_sub):
            rows = slice(j * ATT_SUB, (j + 1) * ATT_SUB)
            o_ref[0, rows, :] = attend(q_ref[0, rows, :], k_ref[0, 0:n_ctx, :], v_ref[0, 0:n_ctx, :])

    lax.fori_loop(first_latent, n_sub, latent, 0)


def _da_call(q, k, v, lam_vec, lam_init, gain_eff, n_ctx):
    b, t, _ = q.shape
    tq = _row_tile(t)
    return pl.pallas_call(
        functools.partial(_da_kernel, tq=tq, n_ctx=n_ctx),
        grid=(b, DA_HEADS, t // tq),
        in_specs=[
            pl.BlockSpec((1, tq, LANES), lambda bi, h, i: (bi, i, h)),
            pl.BlockSpec((1, t, LANES), lambda bi, h, i: (bi, 0, h)),
            pl.BlockSpec((1, t, LANES), lambda bi, h, i: (bi, 0, h)),
            pl.BlockSpec((4, DA_HEAD_DIM), lambda bi, h, i: (0, 0)),
            pl.BlockSpec((1, 1), lambda bi, h, i: (0, 0)),
            pl.BlockSpec((1, LANES), lambda bi, h, i: (0, 0)),
        ],
        out_specs=pl.BlockSpec((1, tq, LANES), lambda bi, h, i: (bi, i, h)),
        out_shape=jax.ShapeDtypeStruct((b, t, DA_HEADS * LANES), BF16),
        compiler_params=_cparams(("parallel", "parallel", "parallel")),
        name="diff_attention",
    )(q, k, v, lam_vec, lam_init, gain_eff)


def _mla_kernel(q_ref, k_ref, v_ref, o_ref, *, tq, n_ctx):
    qi = pl.program_id(1)
    vlane = lax.broadcasted_iota(jnp.int32, (1, MLA_HEADS * MLA_V), 1) // MLA_V

    def attend(q, k, v):
        o = None
        for h in range(MLA_HEADS):
            sl = slice(h * LANES, (h + 1) * LANES)
            p, r = _softmax_parts(_dot_nt(q[:, sl], k[:, sl]))
            vh = v * jnp.where(vlane == h, 1.0, 0.0).astype(BF16)
            oh = _dot(p, vh) * r
            o = oh if o is None else o + oh
        return o.astype(BF16)

    def latent(j, carry):
        rows = pl.ds(pl.multiple_of(j * ATT_SUB, ATT_SUB), ATT_SUB)
        o_ref[0, rows, :] = attend(q_ref[0, rows, :], k_ref[0], v_ref[0])
        return carry

    n_sub = tq // ATT_SUB
    ctx_sub = n_ctx // ATT_SUB
    first_latent = jnp.where(qi == 0, ctx_sub, 0)

    @pl.when(qi == 0)
    def _():
        for j in range(ctx_sub):
            rows = slice(j * ATT_SUB, (j + 1) * ATT_SUB)
            o_ref[0, rows, :] = attend(q_ref[0, rows, :], k_ref[0, 0:n_ctx, :], v_ref[0, 0:n_ctx, :])

    lax.fori_loop(first_latent, n_sub, latent, 0)


def _mla_call(q, k, v, n_ctx):
    b, t, _ = q.shape
    tq = _row_tile(t)
    wv = MLA_HEADS * MLA_V
    return pl.pallas_call(
        functools.partial(_mla_kernel, tq=tq, n_ctx=n_ctx),
        grid=(b, t // tq),
        in_specs=[
            pl.BlockSpec((1, tq, MLA_HEADS * LANES), lambda bi, i: (bi, i, 0)),
            pl.BlockSpec((1, t, MLA_HEADS * LANES), lambda bi, i: (bi, 0, 0)),
            pl.BlockSpec((1, t, wv), lambda bi, i: (bi, 0, 0)),
        ],
        out_specs=pl.BlockSpec((1, tq, wv), lambda bi, i: (bi, i, 0)),
        out_shape=jax.ShapeDtypeStruct((b, t, wv), BF16),
        compiler_params=_cparams(("parallel", "parallel")),
        name="latent_attention",
    )(q, k, v)


def _hgrn_kernel(hg_ref, lbf_ref, lbb_ref, gain_ref, o_ref,
                 kf_s, bf_s, kb_s, cb_s, v_s, qtf_s, ktf_s, qtb_s, ktb_s,
                 df_s, db_s, vt_s, oacc_s, stf_s, stb_s, *, t_len, n_ctx):
    w = HG_WIDTH
    rt = HG_TILE
    blk = HG_BLOCK
    pad = blk
    n_tiles = t_len // rt
    blk_per_tile = rt // blk

    r_i = lax.broadcasted_iota(jnp.int32, (rt, rt), 0)
    c_i = lax.broadcasted_iota(jnp.int32, (rt, rt), 1)
    same = (r_i // blk) == (c_i // blk)
    incl_lo = jnp.where(same & (c_i <= r_i), 1.0, 0.0).astype(BF16)
    incl_hi = jnp.where(same & (c_i >= r_i), 1.0, 0.0).astype(BF16)
    blk_sum = jnp.where(same, 1.0, 0.0).astype(BF16)
    sel_rows = 16
    g_r = lax.broadcasted_iota(jnp.int32, (sel_rows, rt), 0)
    g_c = lax.broadcasted_iota(jnp.int32, (sel_rows, rt), 1)
    blk_sel = jnp.where((g_c // blk) == g_r, 1.0, 0.0).astype(BF16)
    h_r = lax.broadcasted_iota(jnp.int32, (w, w), 0) // HG_KEY
    h_c = lax.broadcasted_iota(jnp.int32, (w, w), 1) // HG_KEY
    head_ones = jnp.where(h_r == h_c, 1.0, 0.0).astype(BF16)
    head_mask = jnp.where(h_r == h_c, 1.0, 0.0).astype(F32)
    pos = lax.broadcasted_iota(jnp.int32, (rt, 1), 0) % blk

    lbf = lbf_ref[...]
    lbb = lbb_ref[...]

    zpad = jnp.zeros((pad, w), F32)
    for s in (kf_s, bf_s, kb_s, cb_s, v_s):
        s[0:pad, :] = zpad
        s[pad + rt:pad + rt + pad, :] = zpad

    def gates(z, lb):
        f = lb + (1.0 - lb) * _sigmoid(z)
        return (1.0 - lb) * _sigmoid(-z), jnp.log(f)

    def tile_body(ti, carry):
        rows = pl.ds(pl.multiple_of(ti * rt, rt), rt)
        q = hg_ref[0, rows, 0:w] * (HG_KEY ** -0.5)
        v = hg_ref[0, rows, w:2 * w]
        kf, lf = gates(hg_ref[0, rows, 2 * w:3 * w], lbf)
        kb, lg = gates(hg_ref[0, rows, 3 * w:4 * w], lbb)
        b_f = _dot_exact_lhs(incl_lo, lf)
        tot_f = _dot_exact_lhs(blk_sum, lf)
        c_b = _dot_exact_lhs(incl_hi, lg)
        tot_b = _dot_exact_lhs(blk_sum, lg)
        qtf_s[rows, :] = (q * jnp.exp(b_f)).astype(BF16)
        ktf_s[rows, :] = (kf * jnp.exp(tot_f - b_f)).astype(BF16)
        qtb_s[rows, :] = (q * jnp.exp(c_b)).astype(BF16)
        ktb_s[rows, :] = (kb * jnp.exp(tot_b - c_b)).astype(BF16)
        drows = pl.ds(pl.multiple_of(ti * blk_per_tile, blk_per_tile), blk_per_tile)
        df_s[drows, :] = jnp.exp(_dot_exact_lhs(blk_sel, lf)[0:blk_per_tile])
        db_s[drows, :] = jnp.exp(_dot_exact_lhs(blk_sel, lg)[0:blk_per_tile])
        vt_s[:, rows] = v.T.astype(BF16)

        kf_s[pad:pad + rt, :] = kf
        bf_s[pad:pad + rt, :] = b_f
        kb_s[pad:pad + rt, :] = kb
        cb_s[pad:pad + rt, :] = c_b
        v_s[pad:pad + rt, :] = v
        od = jnp.zeros((rt, w), F32)
        for dl in range(blk):
            lo = pad - dl
            e = q * kf_s[lo:lo + rt, :] * jnp.exp(jnp.minimum(b_f - bf_s[lo:lo + rt, :], 0.0))
            e = jnp.where(pos >= dl, e, 0.0)
            od = od + _dot(e.astype(BF16), head_ones) * v_s[lo:lo + rt, :]
            hi = pad + dl
            e = q * kb_s[hi:hi + rt, :] * jnp.exp(jnp.minimum(c_b - cb_s[hi:hi + rt, :], 0.0))
            e = jnp.where(pos + dl < blk, e, 0.0)
            od = od + _dot(e.astype(BF16), head_ones) * v_s[hi:hi + rt, :]
        oacc_s[rows, :] = od
        return carry

    lax.fori_loop(0, n_tiles, tile_body, 0)

    stf_s[...] = jnp.zeros((w, w), F32)
    stb_s[...] = jnp.zeros((w, w), F32)
    n_blk = t_len // blk
    n_cblk = n_ctx // blk
    lane = lax.broadcasted_iota(jnp.int32, (1, LANES), 1)

    def advance(bi, qt_s, kt_s, d_s, st_s):
        r = pl.multiple_of(bi * blk, blk)
        st = st_s[...]
        rows = pl.ds(r, blk)
        oacc_s[rows, :] = oacc_s[rows, :] + _dot_nt(qt_s[rows, :], st.astype(BF16))
        slab = pl.multiple_of((r // LANES) * LANES, LANES)
        off = r - slab
        vt = vt_s[:, pl.ds(slab, LANES)]
        vt = vt * jnp.where((lane >= off) & (lane < off + blk), 1.0, 0.0).astype(BF16)
        upd = _dot(vt, kt_s[pl.ds(slab, LANES), :])
        st_s[...] = st * d_s[pl.ds(bi, 1), :] + upd * head_mask

    def step(i, carry):
        advance(i, qtf_s, ktf_s, df_s, stf_s)
        jb = jnp.where(i < n_cblk, n_cblk - 1 - i, n_blk - 1 - (i - n_cblk))
        advance(jb, qtb_s, ktb_s, db_s, stb_s)
        return carry

    lax.fori_loop(0, n_blk, step, 0)

    gain = gain_ref[...]

    def out_body(ti, carry):
        rows = pl.ds(pl.multiple_of(ti * rt, rt), rt)
        o = oacc_s[rows, :]
        ms = _dot_exact_rhs(o * o, head_ones) * (1.0 / HG_KEY)
        y = o * lax.rsqrt(ms + 1e-6) * gain
        g = hg_ref[0, rows, 4 * w:5 * w]
        o_ref[0, rows, :] = (y * (g * _sigmoid(g))).astype(BF16)
        return carry

    lax.fori_loop(0, n_tiles, out_body, 0)


def _hgrn_call(hg, lbf, lbb, gain_t, n_ctx):
    b, t, _ = hg.shape
    w = HG_WIDTH
    tile_pad = pltpu.VMEM((HG_TILE + 2 * HG_BLOCK, w), F32)
    seq_bf16 = pltpu.VMEM((t, w), BF16)
    dec = pltpu.VMEM((t // HG_BLOCK, w), F32)
    return pl.pallas_call(
        functools.partial(_hgrn_kernel, t_len=t, n_ctx=n_ctx),
        grid=(b,),
        in_specs=[
            pl.BlockSpec((1, t, 5 * w), lambda bi: (bi, 0, 0)),
            pl.BlockSpec((1, w), lambda bi: (0, 0)),
            pl.BlockSpec((1, w), lambda bi: (0, 0)),
            pl.BlockSpec((1, w), lambda bi: (0, 0)),
        ],
        out_specs=pl.BlockSpec((1, t, w), lambda bi: (bi, 0, 0)),
        out_shape=jax.ShapeDtypeStruct((b, t, w), BF16),
        scratch_shapes=[tile_pad] * 5 + [seq_bf16] * 4 + [dec, dec,
                        pltpu.VMEM((w, t), BF16), pltpu.VMEM((t, w), F32),
                        pltpu.VMEM((w, w), F32), pltpu.VMEM((w, w), F32)],
        compiler_params=_cparams(("parallel",)),
        name="hgrn2",
    )(hg, lbf, lbb, gain_t)


def _mod_rows(ml_ref, mc_ref, tm, n_ctx, idx):
    i = pl.program_id(1)
    row = i * tm + lax.broadcasted_iota(jnp.int32, (tm, 1), 0)
    isctx = row < n_ctx
    ml = ml_ref[0]
    mc = mc_ref[0]
    return [jnp.where(isctx, mc[k:k + 1], ml[k:k + 1]) for k in idx]


def _out_kernel(oa_ref, ob_ref, og_ref, wo_ref, x_ref, ml_ref, mc_ref, gain_ref, *rest,
                tm, n_ctx, routed):
    if routed:
        router_ref, xo_ref, h_ref, gates_ref = rest
    else:
        xo_ref, h_ref = rest
    gate, shift, scale = _mod_rows(ml_ref, mc_ref, tm, n_ctx, (2, 3, 4))
    mix = (_dot(oa_ref[0], wo_ref[0:512, :]) + _dot(ob_ref[0], wo_ref[512:768, :])
           + _dot(og_ref[0], wo_ref[768:1024, :]))
    x = x_ref[0] + gate * mix
    xo_ref[0] = x
    y = x * lax.rsqrt(jnp.mean(x * x, axis=-1, keepdims=True) + 1e-6) * gain_ref[...]
    h = y * (1.0 + scale) + shift
    h_ref[0] = h.astype(h_ref.dtype)
    if routed:
        lane = lax.broadcasted_iota(jnp.int32, (1, LANES), 1)
        logits = _dot_f32(h, router_ref[...])
        logits = jnp.where(lane < N_EXPERTS, logits, -jnp.inf)
        m1 = jnp.max(logits, axis=-1, keepdims=True)
        i1 = jnp.min(jnp.where(logits == m1, lane, LANES), axis=-1, keepdims=True)
        rest_l = jnp.where(lane == i1, -jnp.inf, logits)
        m2 = jnp.max(rest_l, axis=-1, keepdims=True)
        i2 = jnp.min(jnp.where(rest_l == m2, lane, LANES), axis=-1, keepdims=True)
        e2 = jnp.exp(m2 - m1)
        w1 = 1.0 / (1.0 + e2)
        route = jnp.where(lane == 0, i1.astype(F32), 0.0) + jnp.where(lane == 1, i2.astype(F32), 0.0)
        gates_ref[0] = route + jnp.where(lane == 2, w1, 0.0) + jnp.where(lane == 3, e2 * w1, 0.0)


def _out_call(oa, ob, og, wo, xa, mods, gain, router_p, n_ctx):
    b, t, d = xa.shape
    tm = _row_tile(t)
    routed = router_p is not None
    row_blk = lambda width: pl.BlockSpec((1, tm, width), lambda bi, i: (bi, i, 0))
    in_specs = [
        row_blk(oa.shape[-1]), row_blk(ob.shape[-1]), row_blk(og.shape[-1]),
        _resident(wo.shape),
        row_blk(d),
        pl.BlockSpec((1, 6, d), lambda bi, i: (bi, 0, 0)),
        pl.BlockSpec((1, 6, d), lambda bi, i: (b, 0, 0)),
        _resident((1, d)),
    ]
    args = [oa, ob, og, wo, xa, mods, mods, gain]
    out_specs = [row_blk(d), row_blk(d)]
    out_shape = [jax.ShapeDtypeStruct((b, t, d), F32),
                 jax.ShapeDtypeStruct((b, t, d), F32 if routed else BF16)]
    if routed:
        in_specs.append(_resident(router_p.shape))
        args.append(router_p)
        out_specs.append(row_blk(LANES))
        out_shape.append(jax.ShapeDtypeStruct((b, t, LANES), F32))
    return pl.pallas_call(
        functools.partial(_out_kernel, tm=tm, n_ctx=n_ctx, routed=routed),
        grid=(b, t // tm),
        in_specs=in_specs,
        out_specs=out_specs,
        out_shape=out_shape,
        compiler_params=_cparams(("parallel", "parallel")),
        name="out_proj_routed" if routed else "out_proj",
    )(*args)


def _swiglu_into(acc_ref, h, wg_ref, wu_ref, wd_ref, row_gate, d_ff):
    for c in range(d_ff // FF_CHUNK):
        cs = slice(c * FF_CHUNK, (c + 1) * FF_CHUNK)
        g = _dot(h, wg_ref[0, :, cs])
        u = _dot(h, wu_ref[0, :, cs])
        a = g * _sigmoid(g) * u
        if row_gate is not None:
            a = a * row_gate
        acc_ref[...] += _dot(a.astype(BF16), wd_ref[0, cs, :])


def _ffn_kernel(h_ref, x_ref, ml_ref, mc_ref, wg_ref, wu_ref, wd_ref, xo_ref, acc_ref,
                *, tm, n_ctx, d_ff):
    (gate,) = _mod_rows(ml_ref, mc_ref, tm, n_ctx, (5,))
    acc_ref[...] = jnp.zeros_like(acc_ref)
    _swiglu_into(acc_ref, h_ref[0], wg_ref, wu_ref, wd_ref, None, d_ff)
    xo_ref[0] = x_ref[0] + gate * acc_ref[...]


def _ffn_call(h, xa, mods, wg, wu, wd, n_ctx):
    b, t, d = xa.shape
    d_ff = wg.shape[-1]
    tm = _row_tile(t)
    row_blk = pl.BlockSpec((1, tm, d), lambda bi, i: (bi, i, 0))
    return pl.pallas_call(
        functools.partial(_ffn_kernel, tm=tm, n_ctx=n_ctx, d_ff=d_ff),
        grid=(b, t // tm),
        in_specs=[
            row_blk, row_blk,
            pl.BlockSpec((1, 6, d), lambda bi, i: (bi, 0, 0)),
            pl.BlockSpec((1, 6, d), lambda bi, i: (b, 0, 0)),
            _resident((1, d, d_ff)), _resident((1, d, d_ff)), _resident((1, d_ff, d)),
        ],
        out_specs=row_blk,
        out_shape=jax.ShapeDtypeStruct((b, t, d), F32),
        scratch_shapes=[pltpu.VMEM((tm, d), F32)],
        compiler_params=_cparams(("parallel", "parallel")),
        name="ffn_dense",
    )(h, xa, mods, mods, wg[None], wu[None], wd[None])


def _row_copy(src, src_row, dst, dst_row, sem):
    return pltpu.make_async_copy(src.at[pl.ds(src_row, 1), :], dst.at[pl.ds(dst_row, 1), :], sem)


def _dispatch_kernel(pos_ref, pad_ref, h_ref, xs_ref, zero_s, sem, zsem, *, tm, n_exp, n_tiles):
    step = pl.program_id(0)
    base = step * tm

    def pad_copies(act):
        for e in range(n_exp):
            start = pad_ref[e]

            def tail_row(r, carry):
                act(_row_copy(zero_s, 0, xs_ref, start + r, zsem))
                return carry

            lax.fori_loop(0, pad_ref[n_exp + e], tail_row, 0)
        for j in range(n_exp):
            tile = pad_ref[2 * n_exp] + j

            @pl.when(tile < n_tiles)
            def _():
                row0 = pl.multiple_of(tile * tm, tm)
                act(pltpu.make_async_copy(zero_s, xs_ref.at[pl.ds(row0, tm), :], zsem))

    @pl.when(step == 0)
    def _():
        zero_s[...] = jnp.zeros_like(zero_s)
        pad_copies(lambda c: c.start())

    def issue(r, carry):
        for s in range(TOP_K):
            _row_copy(h_ref, r, xs_ref, pos_ref[TOP_K * (base + r) + s], sem).start()
        return carry

    lax.fori_loop(0, tm, issue, 0, unroll=8)

    def drain(r, carry):
        for s in range(TOP_K):
            _row_copy(h_ref, r, xs_ref, pos_ref[TOP_K * (base + r) + s], sem).wait()
        return carry

    lax.fori_loop(0, tm, drain, 0, unroll=8)

    @pl.when(step == 0)
    def _():
        pad_copies(lambda c: c.wait())


def _dispatch_call(pos, pad_info, h2d, n_tiles, n_exp):
    n, d = h2d.shape
    tm = MOE_TILE
    return pl.pallas_call(
        functools.partial(_dispatch_kernel, tm=tm, n_exp=n_exp, n_tiles=n_tiles),
        grid_spec=pltpu.PrefetchScalarGridSpec(
            num_scalar_prefetch=2,
            grid=(n // tm,),
            in_specs=[pl.BlockSpec((tm, d), lambda i, pos, pad: (i, 0))],
            out_specs=pl.BlockSpec(memory_space=pl.ANY),
            scratch_shapes=[pltpu.VMEM((tm, d), F32), pltpu.SemaphoreType.DMA(()),
                            pltpu.SemaphoreType.DMA(())],
        ),
        out_shape=jax.ShapeDtypeStruct((n_tiles * tm, d), F32),
        compiler_params=_cparams(("arbitrary",)),
        name="moe_dispatch",
    )(pos, pad_info, h2d)


def _expert_kernel(te_ref, nv_ref, xs_ref, wg_ref, wu_ref, wd_ref, ys_ref, acc_ref, *, tm, d_ff):
    n_valid = nv_ref[pl.program_id(0)]

    @pl.when(n_valid > 0)
    def _():
        acc_ref[...] = jnp.zeros_like(acc_ref)
        _swiglu_into(acc_ref, xs_ref[...].astype(BF16), wg_ref, wu_ref, wd_ref, None, d_ff)
        ys_ref[...] = acc_ref[...]

    @pl.when(n_valid == 0)
    def _():
        ys_ref[...] = jnp.zeros_like(ys_ref)


def _expert_call(tile_expert, tile_valid, xs, wg, wu, wd):
    n_slots, d = xs.shape
    d_ff = wg.shape[-1]
    tm = MOE_TILE
    w_blk = lambda shape: pl.BlockSpec(shape, lambda t, te, nv: (te[t], 0, 0))
    return pl.pallas_call(
        functools.partial(_expert_kernel, tm=tm, d_ff=d_ff),
        grid_spec=pltpu.PrefetchScalarGridSpec(
            num_scalar_prefetch=2,
            grid=(n_slots // tm,),
            in_specs=[
                pl.BlockSpec((tm, d), lambda t, te, nv: (t, 0)),
                w_blk((1, d, d_ff)), w_blk((1, d, d_ff)), w_blk((1, d_ff, d)),
            ],
            out_specs=pl.BlockSpec((tm, d), lambda t, te, nv: (t, 0)),
            scratch_shapes=[pltpu.VMEM((tm, d), F32)],
        ),
        out_shape=jax.ShapeDtypeStruct((n_slots, d), F32),
        compiler_params=_cparams(("arbitrary",)),
        name="moe_experts",
    )(tile_expert, tile_valid, xs, wg, wu, wd)


def _combine_kernel(pos_ref, ys_ref, x_ref, route_ref, ml_ref, mc_ref, xo_ref, y1_s, y2_s, sem,
                    *, tm, n_ctx):
    base = (pl.program_id(0) * pl.num_programs(1) + pl.program_id(1)) * tm
    bufs = (y1_s, y2_s)

    def issue(r, carry):
        for s in range(TOP_K):
            _row_copy(ys_ref, pos_ref[TOP_K * (base + r) + s], bufs[s], r, sem).start()
        return carry

    lax.fori_loop(0, tm, issue, 0, unroll=8)

    def drain(r, carry):
        for s in range(TOP_K):
            _row_copy(ys_ref, pos_ref[TOP_K * (base + r) + s], bufs[s], r, sem).wait()
        return carry

    lax.fori_loop(0, tm, drain, 0, unroll=8)

    lane = lax.broadcasted_iota(jnp.int32, (1, LANES), 1)
    route = route_ref[0]
    w1 = jnp.sum(jnp.where(lane == 2, route, 0.0), axis=-1, keepdims=True)
    w2 = jnp.sum(jnp.where(lane == 3, route, 0.0), axis=-1, keepdims=True)
    (gate,) = _mod_rows(ml_ref, mc_ref, tm, n_ctx, (5,))
    xo_ref[0] = x_ref[0] + gate * (w1 * y1_s[...] + w2 * y2_s[...])


def _combine_call(pos, ys, xa, route, mods, n_ctx):
    b, t, d = xa.shape
    tm = _row_tile(t)
    row_blk = lambda width: pl.BlockSpec((1, tm, width), lambda bi, i, pos: (bi, i, 0))
    return pl.pallas_call(
        functools.partial(_combine_kernel, tm=tm, n_ctx=n_ctx),
        grid_spec=pltpu.PrefetchScalarGridSpec(
            num_scalar_prefetch=1,
            grid=(b, t // tm),
            in_specs=[
                pl.BlockSpec(memory_space=pl.ANY),
                row_blk(d), row_blk(LANES),
                pl.BlockSpec((1, 6, d), lambda bi, i, pos: (bi, 0, 0)),
                pl.BlockSpec((1, 6, d), lambda bi, i, pos: (b, 0, 0)),
            ],
            out_specs=row_blk(d),
            scratch_shapes=[pltpu.VMEM((tm, d), F32), pltpu.VMEM((tm, d), F32),
                            pltpu.SemaphoreType.DMA(())],
        ),
        out_shape=jax.ShapeDtypeStruct((b, t, d), F32),
        compiler_params=_cparams(("arbitrary", "arbitrary")),
        name="moe_combine",
    )(pos, ys, xa, route, mods, mods)


def _moe_call(h, xa, route, mods, wg, wu, wd, n_ctx):
    b, t, d = xa.shape
    n = b * t
    tm = MOE_TILE
    n_exp = wg.shape[0]
    n_tiles = (TOP_K * n) // tm + n_exp
    choice = route.reshape(n, LANES)[:, :TOP_K].astype(jnp.int32).reshape(-1)
    onehot = (choice[:, None] == jnp.arange(n_exp, dtype=jnp.int32)[None, :]).astype(jnp.int32)
    csum = jnp.cumsum(onehot, axis=0)
    counts = csum[-1]
    rank = jnp.sum((csum - onehot) * onehot, axis=1)
    group_tiles = (counts + tm - 1) // tm
    tile_end = jnp.cumsum(group_tiles)
    group_row0 = (tile_end - group_tiles) * tm
    pos = jnp.sum(group_row0[None, :] * onehot, axis=1) + rank
    tile_id = jnp.arange(n_tiles, dtype=jnp.int32)
    tile_group = jnp.sum((tile_id[:, None] >= tile_end[None, :]).astype(jnp.int32), axis=1)
    last_group = jnp.max(jnp.where(counts > 0, jnp.arange(n_exp, dtype=jnp.int32), 0))
    tile_expert = jnp.minimum(tile_group, last_group)
    in_group = tile_group < n_exp
    g = jnp.minimum(tile_group, n_exp - 1)
    tile_valid = jnp.where(in_group, jnp.clip(counts[g] - (tile_id * tm - group_row0[g]), 0, tm), 0)

    pad_info = jnp.concatenate([group_row0 + counts, group_tiles * tm - counts, tile_end[-1:]])
    xs = _dispatch_call(pos, pad_info.astype(jnp.int32), h.reshape(n, d), n_tiles, n_exp)
    ys = _expert_call(tile_expert.astype(jnp.int32), tile_valid.astype(jnp.int32), xs, wg, wu, wd)
    return _combine_call(pos, ys, xa, route, mods, n_ctx)


def _final_kernel(x_ref, g_ref, o_ref):
    x = x_ref[0]
    o_ref[0] = x * lax.rsqrt(jnp.mean(x * x, axis=-1, keepdims=True) + 1e-6) * g_ref[...]


def _final_call(xa, gain, n_ctx):
    b, t, d = xa.shape
    tm = 256
    skip = n_ctx // tm
    return pl.pallas_call(
        _final_kernel,
        grid=(b, (t - n_ctx) // tm),
        in_specs=[pl.BlockSpec((1, tm, d), lambda bi, i: (bi, i + skip, 0)), _resident((1, d))],
        out_specs=pl.BlockSpec((1, tm, d), lambda bi, i: (bi, i, 0)),
        out_shape=jax.ShapeDtypeStruct((b, t - n_ctx, d), F32),
        compiler_params=_cparams(("parallel", "parallel")),
        name="final_norm",
    )(xa, gain)


def _rope_tables(seq, n_ctx):
    rows = seq // GRID_W
    row, col = jnp.meshgrid(jnp.arange(rows, dtype=F32), jnp.arange(GRID_W, dtype=F32), indexing="ij")
    row, col = row.reshape(-1), col.reshape(-1)

    def angles(rot_dim):
        n_freq = rot_dim // 4
        inv = ROPE_BASE ** (-jnp.arange(n_freq, dtype=F32) / n_freq)
        ang = jnp.concatenate([row[:, None] * inv, col[:, None] * inv], axis=-1)
        return jnp.cos(ang), jnp.sin(ang)

    def with_ctx(a, fill):
        return jnp.concatenate([jnp.full((n_ctx, a.shape[1]), fill, F32), a], axis=0)

    cos, sin = angles(DA_HEAD_DIM)
    cos_da = with_ctx(jnp.tile(cos, (1, 4)), 1.0)
    sin_da = with_ctx(jnp.tile(jnp.concatenate([-sin, sin], axis=1), (1, 2)), 0.0)
    cos, sin = angles(MLA_ROPE)
    ones = jnp.ones((seq, MLA_NOPE), F32)
    tail = jnp.ones((seq, LANES - MLA_NOPE - MLA_ROPE), F32)
    cos_m = with_ctx(jnp.concatenate([ones, cos, cos, tail], axis=1), 1.0)
    sin_m = with_ctx(jnp.concatenate([0 * ones, -sin, sin, 0 * tail], axis=1), 0.0)
    return cos_da, sin_da, cos_m, sin_m


def _pad_in_proj(w_in_l):
    d = w_in_l.shape[0]
    cut = 1536 + MLA_Q_RANK + MLA_KV_RANK + MLA_ROPE
    padw = 2048 - cut
    return jnp.concatenate([w_in_l[:, :cut], jnp.zeros((d, padw), w_in_l.dtype), w_in_l[:, cut:]],
                           axis=1).astype(BF16)


def _pad_uq(w_uq_l):
    qk = MLA_NOPE + MLA_ROPE
    w = w_uq_l.reshape(MLA_Q_RANK, MLA_HEADS, qk)
    w = jnp.pad(w, ((0, 0), (0, 0), (0, LANES - qk)))
    return w.reshape(MLA_Q_RANK, MLA_HEADS * LANES).astype(BF16)


def _pad_ukv(w_ukv_l):
    w = w_ukv_l.reshape(MLA_KV_RANK, MLA_HEADS, MLA_NOPE + MLA_V)
    k_part = jnp.pad(w[:, :, :MLA_NOPE], ((0, 0), (0, 0), (0, LANES - MLA_NOPE)))
    k_part = k_part.reshape(MLA_KV_RANK, MLA_HEADS * LANES)
    v_part = w[:, :, MLA_NOPE:].reshape(MLA_KV_RANK, MLA_HEADS * MLA_V)
    place = jnp.zeros((MLA_ROPE, MLA_HEADS, LANES), F32)
    eye = jnp.eye(MLA_ROPE, dtype=F32)
    place = place.at[:, :, MLA_NOPE:MLA_NOPE + MLA_ROPE].set(eye[:, None, :])
    place = place.reshape(MLA_ROPE, MLA_HEADS * LANES)
    top = jnp.concatenate([k_part, v_part], axis=1)
    mid = jnp.concatenate([place, jnp.zeros((MLA_ROPE, MLA_HEADS * MLA_V), F32)], axis=1)
    bot = jnp.zeros((MXU_COLS - MLA_KV_RANK - MLA_ROPE, top.shape[1]), F32)
    return jnp.concatenate([top, mid, bot], axis=0).astype(BF16)


def kernel(x, c, ctx, c_ctx, w_mod, b_mod, norm_mix, norm_ffn, norm_final, w_in, w_out, diff_lambda, diff_norm, mla_q_norm, mla_w_uq, mla_kv_norm, mla_w_ukv, hgrn_lb_logits, hgrn_norm, ffn_w_gate, ffn_w_up, ffn_w_down, moe_router, moe_w_gate, moe_w_up, moe_w_down):
    b, seq, d = x.shape
    n_ctx = ctx.shape[1]
    depth = w_mod.shape[0]
    assert d == 1024 and n_ctx % ATT_SUB == 0 and seq % ATT_SUB == 0

    xa = jnp.concatenate([ctx, x], axis=1)
    bp = -(-(b + 1) // 8) * 8
    c_all = jnp.concatenate([c, c_ctx[None, :], jnp.zeros((bp - b - 1, d), F32)], axis=0)
    mods = _mod_call(c_all, w_mod, b_mod).reshape(depth, bp, 6, d)

    lb = jnp.cumsum(jax.nn.softmax(hgrn_lb_logits.astype(F32), axis=1), axis=1)
    lb = lb - lb[:, :1]
    tabs = _rope_tables(seq, n_ctx)

    for l in range(depth):
        lam_init = 0.8 - 0.6 * math.exp(-0.3 * l)
        q_da, k_da, v_da, q_m, k_m, v_m, hg = _in_call(
            xa, norm_mix[l][None], mods[l], _pad_in_proj(w_in[l]), tabs,
            mla_q_norm[l][None], mla_kv_norm[l][None], _pad_uq(mla_w_uq[l]), _pad_ukv(mla_w_ukv[l]),
            n_ctx)
        oa = _da_call(q_da, k_da, v_da, diff_lambda[l], jnp.full((1, 1), lam_init, F32),
                      (diff_norm[l] * (1.0 - lam_init))[None], n_ctx)
        ob = _mla_call(q_m, k_m, v_m, n_ctx)
        og = _hgrn_call(hg, lb[0, l][None], lb[1, l][None], jnp.tile(hgrn_norm[l], HG_HEADS)[None], n_ctx)

        wo = w_out[l].astype(BF16)
        if l % 2 == 0:
            xa, h = _out_call(oa, ob, og, wo, xa, mods[l], norm_ffn[l][None], None, n_ctx)
            xa = _ffn_call(h, xa, mods[l], ffn_w_gate[l // 2].astype(BF16),
                           ffn_w_up[l // 2].astype(BF16), ffn_w_down[l // 2].astype(BF16), n_ctx)
        else:
            router_p = jnp.pad(moe_router[l // 2], ((0, 0), (0, LANES - N_EXPERTS)))
            xa, h, gates = _out_call(oa, ob, og, wo, xa, mods[l], norm_ffn[l][None], router_p, n_ctx)
            xa = _moe_call(h, xa, gates, mods[l], moe_w_gate[l // 2].astype(BF16),
                           moe_w_up[l // 2].astype(BF16), moe_w_down[l // 2].astype(BF16), n_ctx)
    return _final_call(xa, norm_final[None], n_ctx)
```

```python
import functools
import math

import jax
import jax.numpy as jnp
from jax import lax
from jax.experimental import pallas as pl
from jax.experimental.pallas import tpu as pltpu

F32 = jnp.float32
BF16 = jnp.bfloat16

GRID_W = 64
ROPE_BASE = 10000.0
DA_HEADS = 4
DA_HEAD_DIM = 64
MLA_HEADS = 4
MLA_Q_RANK = 256
MLA_KV_RANK = 128
MLA_NOPE = 64
MLA_ROPE = 32
MLA_V = 64
HG_HEADS = 4
HG_KEY = 64
HG_WIDTH = HG_HEADS * HG_KEY
N_EXPERTS = 8
TOP_K = 2
LOG2_E = 1.4426950408889634

LANES = 128
MXU_COLS = 256
VMEM_LIMIT = 56 * 1024 * 1024

HG_BLOCK = 16
HG_CHUNK = 64
HG_CHUNK_RANGE = 180.0
HG_TILE = 128
ATT_SUB = 256
FF_CHUNK = 256
MOE_TILE = 512


def _dot(a, b):
    return jnp.dot(a, b, preferred_element_type=F32)


def _dot_nt(a, b):
    return lax.dot_general(a, b, (((1,), (1,)), ((), ())), preferred_element_type=F32)


def _split3(x):
    x0 = x.astype(BF16)
    r = x - x0.astype(F32)
    x1 = r.astype(BF16)
    x2 = (r - x1.astype(F32)).astype(BF16)
    return x0, x1, x2


def _dot_exact_lhs(a_bf16, x):
    x0, x1, x2 = _split3(x)
    return _dot(a_bf16, x0) + _dot(a_bf16, x1) + _dot(a_bf16, x2)


def _dot_exact_rhs(x, b_bf16):
    x0, x1, x2 = _split3(x)
    return _dot(x0, b_bf16) + _dot(x1, b_bf16) + _dot(x2, b_bf16)


def _dot_f32(a, b):
    a0 = a.astype(BF16)
    a1 = (a - a0.astype(F32)).astype(BF16)
    b0 = b.astype(BF16)
    b1 = (b - b0.astype(F32)).astype(BF16)
    return _dot(a0, b0) + _dot(a0, b1) + _dot(a1, b0)


def _sigmoid(z):
    return 1.0 / (1.0 + jnp.exp(-z))


def _row_tile(t):
    return 768 if t % 768 == 0 else 256


def _cparams(sem):
    return pltpu.CompilerParams(dimension_semantics=sem, vmem_limit_bytes=VMEM_LIMIT)


def _resident(shape):
    nd = len(shape)
    return pl.BlockSpec(shape, lambda *_: (0,) * nd, pipeline_mode=pl.Buffered(1))


def _mod_kernel(c_ref, w_ref, b_ref, o_ref):
    c = c_ref[...]
    o_ref[0] = _dot_f32(c * _sigmoid(c), w_ref[0]) + b_ref[0]


def _mod_call(c_all, w_mod, b_mod):
    depth, d, n6 = w_mod.shape
    bp = c_all.shape[0]
    tn = 1536
    return pl.pallas_call(
        _mod_kernel,
        grid=(depth, n6 // tn),
        in_specs=[
            pl.BlockSpec((bp, d), lambda l, j: (0, 0)),
            pl.BlockSpec((1, d, tn), lambda l, j: (l, 0, j)),
            pl.BlockSpec((1, 1, tn), lambda l, j: (l, 0, j)),
        ],
        out_specs=pl.BlockSpec((1, bp, tn), lambda l, j: (l, 0, j)),
        out_shape=jax.ShapeDtypeStruct((depth, bp, n6), F32),
        compiler_params=_cparams(("parallel", "parallel")),
        name="modulation",
    )(c_all, w_mod, b_mod.reshape(depth, 1, n6))


def _rope128(seg, cos, sin, mla):
    lane = lax.broadcasted_iota(jnp.int32, (1, LANES), 1)
    if mla:
        half = MLA_ROPE // 2
        take_next = lane < MLA_NOPE + half
    else:
        half = DA_HEAD_DIM // 2
        take_next = (lane % DA_HEAD_DIM) < half
    sw = jnp.where(take_next, pltpu.roll(seg, LANES - half, 1), pltpu.roll(seg, half, 1))
    return seg * cos + sw * sin


def _in_kernel(x_ref, gain_ref, ml_ref, mc_ref, w_ref, cd_ref, sd_ref, cm_ref, sm_ref,
               qn_ref, kvn_ref, wuq_ref, wkv_ref,
               oq_ref, ok_ref, ov_ref, mq_ref, mk_ref, mv_ref, hg_ref, *, tm, n_ctx):
    i = pl.program_id(1)
    x = x_ref[0]
    y = x * lax.rsqrt(jnp.mean(x * x, axis=-1, keepdims=True) + 1e-6) * gain_ref[...]
    row = i * tm + lax.broadcasted_iota(jnp.int32, (tm, 1), 0)
    isctx = row < n_ctx
    ml = ml_ref[0]
    mc = mc_ref[0]
    shift = jnp.where(isctx, mc[0:1], ml[0:1])
    scale = jnp.where(isctx, mc[1:2], ml[1:2])
    h = (y * (1.0 + scale) + shift).astype(BF16)

    cd, sd = cd_ref[...], sd_ref[...]
    cm, sm = cm_ref[...], sm_ref[...]

    da_scale = DA_HEAD_DIM ** -0.5 * LOG2_E
    aq = _dot(h, w_ref[:, 0:512])
    for g in range(4):
        sl = slice(g * LANES, (g + 1) * LANES)
        oq_ref[0, :, sl] = (_rope128(aq[:, sl], cd, sd, False) * da_scale).astype(BF16)
    ak = _dot(h, w_ref[:, 512:1024])
    for g in range(4):
        sl = slice(g * LANES, (g + 1) * LANES)
        ok_ref[0, :, sl] = _rope128(ak[:, sl], cd, sd, False).astype(BF16)
    ov_ref[0] = _dot(h, w_ref[:, 1024:1536]).astype(BF16)

    mla_scale = (MLA_NOPE + MLA_ROPE) ** -0.5 * LOG2_E
    cc = _dot(h, w_ref[:, 1536:2048])
    cq = cc[:, 0:MLA_Q_RANK]
    qn = (cq * lax.rsqrt(jnp.mean(cq * cq, axis=-1, keepdims=True) + 1e-6) * qn_ref[...]).astype(BF16)
    qu = _dot(qn, wuq_ref[...])
    for g in range(4):
        sl = slice(g * LANES, (g + 1) * LANES)
        mq_ref[0, :, sl] = (_rope128(qu[:, sl], cm, sm, True) * mla_scale).astype(BF16)
    ckv = cc[:, MLA_Q_RANK:MLA_Q_RANK + MLA_KV_RANK]
    rest = cc[:, MLA_Q_RANK + MLA_KV_RANK:]
    kvn = ckv * lax.rsqrt(jnp.mean(ckv * ckv, axis=-1, keepdims=True) + 1e-6) * kvn_ref[...]
    t2 = jnp.concatenate([kvn, rest], axis=1).astype(BF16)
    kv = _dot(t2, wkv_ref[...])
    for g in range(4):
        sl = slice(g * LANES, (g + 1) * LANES)
        mk_ref[0, :, sl] = _rope128(kv[:, sl], cm, sm, True).astype(BF16)
    mv_ref[0] = kv[:, 512:768].astype(BF16)

    hg_ref[0] = _dot(h, w_ref[:, 2048:3328])


def _in_call(xa, gain, mods, w_in_p, tabs, qn, kvn, wuq_p, wkv_p, n_ctx):
    b, t, d = xa.shape
    tm = _row_tile(t)
    cd, sd, cm, sm = tabs
    row_blk = lambda width: pl.BlockSpec((1, tm, width), lambda bi, i: (bi, i, 0))
    tab_blk = pl.BlockSpec((tm, LANES), lambda bi, i: (i, 0))
    out_w = (512, 512, 512, 512, 512, 256)
    return pl.pallas_call(
        functools.partial(_in_kernel, tm=tm, n_ctx=n_ctx),
        grid=(b, t // tm),
        in_specs=[
            row_blk(d),
            _resident((1, d)),
            pl.BlockSpec((1, 6, d), lambda bi, i: (bi, 0, 0)),
            pl.BlockSpec((1, 6, d), lambda bi, i: (b, 0, 0)),
            _resident(w_in_p.shape),
            tab_blk, tab_blk, tab_blk, tab_blk,
            _resident((1, MLA_Q_RANK)),
            _resident((1, MLA_KV_RANK)),
            _resident(wuq_p.shape),
            _resident(wkv_p.shape),
        ],
        out_specs=[row_blk(w) for w in out_w] + [row_blk(5 * HG_WIDTH)],
        out_shape=[jax.ShapeDtypeStruct((b, t, w), BF16) for w in out_w]
        + [jax.ShapeDtypeStruct((b, t, 5 * HG_WIDTH), F32)],
        compiler_params=_cparams(("parallel", "parallel")),
        name="in_proj",
    )(xa, gain, mods, mods, w_in_p, cd, sd, cm, sm, qn, kvn, wuq_p, wkv_p)


def _softmax_parts(s):
    m = jnp.max(s, axis=-1, keepdims=True)
    p = jnp.exp2(s - m)
    return p.astype(BF16), 1.0 / jnp.sum(p, axis=-1, keepdims=True)


def _da_kernel(q_ref, k_ref, v_ref, lamv_ref, lami_ref, gain_ref, o_ref, *, tq, n_ctx):
    qi = pl.program_id(2)
    lane = lax.broadcasted_iota(jnp.int32, (1, LANES), 1)
    map1 = jnp.where(lane < DA_HEAD_DIM, 1.0, 0.0).astype(BF16)
    map2 = jnp.where(lane < DA_HEAD_DIM, 0.0, 1.0).astype(BF16)
    lv = lamv_ref[...]
    lam = (jnp.exp(jnp.sum(lv[0:1] * lv[1:2], axis=-1, keepdims=True))
           - jnp.exp(jnp.sum(lv[2:3] * lv[3:4], axis=-1, keepdims=True)) + lami_ref[...])
    gain = gain_ref[...]

    def attend(q, k, v):
        p1, r1 = _softmax_parts(_dot_nt(q * map1, k))
        p2, r2 = _softmax_parts(_dot_nt(q * map2, k))
        o = _dot(p1, v) * r1 - _dot(p2, v) * (lam * r2)
        return (o * lax.rsqrt(jnp.mean(o * o, axis=-1, keepdims=True) + 1e-5) * gain).astype(BF16)

    def latent(j, carry):
        rows = pl.ds(pl.multiple_of(j * ATT_SUB, ATT_SUB), ATT_SUB)
        o_ref[0, rows, :] = attend(q_ref[0, rows, :], k_ref[0], v_ref[0])
        return carry

    n_sub = tq // ATT_SUB
    ctx_sub = n_ctx // ATT_SUB
    first_latent = jnp.where(qi == 0, ctx_sub, 0)

    @pl.when(qi == 0)
    def _():
        for j in range(ctx_sub):
            rows = slice(j * ATT_SUB, (j + 1) * ATT_SUB)
            o_ref[0, rows, :] = attend(q_ref[0, rows, :], k_ref[0, 0:n_ctx, :], v_ref[0, 0:n_ctx, :])

    lax.fori_loop(first_latent, n_sub, latent, 0)


def _da_call(q, k, v, lam_vec, lam_init, gain_eff, n_ctx):
    b, t, _ = q.shape
    tq = _row_tile(t)
    return pl.pallas_call(
        functools.partial(_da_kernel, tq=tq, n_ctx=n_ctx),
        grid=(b, DA_HEADS, t // tq),
        in_specs=[
            pl.BlockSpec((1, tq, LANES), lambda bi, h, i: (bi, i, h)),
            pl.BlockSpec((1, t, LANES), lambda bi, h, i: (bi, 0, h)),
            pl.BlockSpec((1, t, LANES), lambda bi, h, i: (bi, 0, h)),
            pl.BlockSpec((4, DA_HEAD_DIM), lambda bi, h, i: (0, 0)),
            pl.BlockSpec((1, 1), lambda bi, h, i: (0, 0)),
            pl.BlockSpec((1, LANES), lambda bi, h, i: (0, 0)),
        ],
        out_specs=pl.BlockSpec((1, tq, LANES), lambda bi, h, i: (bi, i, h)),
        out_shape=jax.ShapeDtypeStruct((b, t, DA_HEADS * LANES), BF16),
        compiler_params=_cparams(("parallel", "parallel", "parallel")),
        name="diff_attention",
    )(q, k, v, lam_vec, lam_init, gain_eff)


def _mla_kernel(q_ref, k_ref, v_ref, o_ref, *, tq, n_ctx):
    qi = pl.program_id(1)
    vlane = lax.broadcasted_iota(jnp.int32, (1, MLA_HEADS * MLA_V), 1) // MLA_V

    def attend(q, k, v):
        o = None
        for h in range(MLA_HEADS):
            sl = slice(h * LANES, (h + 1) * LANES)
            p, r = _softmax_parts(_dot_nt(q[:, sl], k[:, sl]))
            vh = v * jnp.where(vlane == h, 1.0, 0.0).astype(BF16)
            oh = _dot(p, vh) * r
            o = oh if o is None else o + oh
        return o.astype(BF16)

    def latent(j, carry):
        rows = pl.ds(pl.multiple_of(j * ATT_SUB, ATT_SUB), ATT_SUB)
        o_ref[0, rows, :] = attend(q_ref[0, rows, :], k_ref[0], v_ref[0])
        return carry

    n_sub = tq // ATT_SUB
    ctx_sub = n_ctx // ATT_SUB
    first_latent = jnp.where(qi == 0, ctx_sub, 0)

    @pl.when(qi == 0)
    def _():
        for j in range(ctx_sub):
            rows = slice(j * ATT_SUB, (j + 1) * ATT_SUB)
            o_ref[0, rows, :] = attend(q_ref[0, rows, :], k_ref[0, 0:n_ctx, :], v_ref[0, 0:n_ctx, :])

    lax.fori_loop(first_latent, n_sub, latent, 0)


def _mla_call(q, k, v, n_ctx):
    b, t, _ = q.shape
    tq = _row_tile(t)
    wv = MLA_HEADS * MLA_V
    return pl.pallas_call(
        functools.partial(_mla_kernel, tq=tq, n_ctx=n_ctx),
        grid=(b, t // tq),
        in_specs=[
            pl.BlockSpec((1, tq, MLA_HEADS * LANES), lambda bi, i: (bi, i, 0)),
            pl.BlockSpec((1, t, MLA_HEADS * LANES), lambda bi, i: (bi, 0, 0)),
            pl.BlockSpec((1, t, wv), lambda bi, i: (bi, 0, 0)),
        ],
        out_specs=pl.BlockSpec((1, tq, wv), lambda bi, i: (bi, i, 0)),
        out_shape=jax.ShapeDtypeStruct((b, t, wv), BF16),
        compiler_params=_cparams(("parallel", "parallel")),
        name="latent_attention",
    )(q, k, v)


def _hgrn_kernel(hg_ref, lbf_ref, lbb_ref, gain_ref, o_ref,
                 kf_s, bf_s, kb_s, cb_s, v_s, qtf_s, ktf_s, qtb_s, ktb_s,
                 df_s, db_s, vt_s, oacc_s, stf_s, stb_s,
                 dcf_s, dcb_s, sfa_s, sfb_s, sba_s, sbb_s, *, t_len, n_ctx):
    w = HG_WIDTH
    rt = HG_TILE
    blk = HG_BLOCK
    pad = blk
    n_tiles = t_len // rt
    blk_per_tile = rt // blk

    r_i = lax.broadcasted_iota(jnp.int32, (rt, rt), 0)
    c_i = lax.broadcasted_iota(jnp.int32, (rt, rt), 1)
    same = (r_i // blk) == (c_i // blk)
    incl_lo = jnp.where(same & (c_i <= r_i), 1.0, 0.0).astype(BF16)
    incl_hi = jnp.where(same & (c_i >= r_i), 1.0, 0.0).astype(BF16)
    blk_sum = jnp.where(same, 1.0, 0.0).astype(BF16)
    sel_rows = 16
    g_r = lax.broadcasted_iota(jnp.int32, (sel_rows, rt), 0)
    g_c = lax.broadcasted_iota(jnp.int32, (sel_rows, rt), 1)
    blk_sel = jnp.where((g_c // blk) == g_r, 1.0, 0.0).astype(BF16)
    h_r = lax.broadcasted_iota(jnp.int32, (w, w), 0) // HG_KEY
    h_c = lax.broadcasted_iota(jnp.int32, (w, w), 1) // HG_KEY
    head_ones = jnp.where(h_r == h_c, 1.0, 0.0).astype(BF16)
    head_mask = jnp.where(h_r == h_c, 1.0, 0.0).astype(F32)
    pos = lax.broadcasted_iota(jnp.int32, (rt, 1), 0) % blk

    lbf = lbf_ref[...]
    lbb = lbb_ref[...]

    zpad = jnp.zeros((pad, w), F32)
    for s in (kf_s, bf_s, kb_s, cb_s, v_s):
        s[0:pad, :] = zpad
        s[pad + rt:pad + rt + pad, :] = zpad

    def gates(z, lb):
        f = lb + (1.0 - lb) * _sigmoid(z)
        return (1.0 - lb) * _sigmoid(-z), jnp.log(f)

    ch = HG_CHUNK
    ch_per_tile = rt // ch
    same_c = (r_i // ch) == (c_i // ch)
    lo_c = same_c & (c_i <= r_i)
    hi_c = same_c & (c_i >= r_i)
    cum_lo = jnp.where(lo_c, 1.0, 0.0).astype(BF16)
    cum_hi = jnp.where(hi_c, 1.0, 0.0).astype(BF16)
    ch_sum = jnp.where(same_c, 1.0, 0.0).astype(BF16)
    ch_sel = jnp.where((g_c // ch) == g_r, 1.0, 0.0).astype(BF16)
    head_of_lane = lax.broadcasted_iota(jnp.int32, (1, w), 1) // HG_KEY
    head_cols = [jnp.where(head_of_lane == h, 1.0, 0.0).astype(BF16) for h in range(HG_HEADS)]

    def chunk_tile(ti, worst):
        rows = pl.ds(pl.multiple_of(ti * rt, rt), rt)
        q = hg_ref[0, rows, 0:w] * (HG_KEY ** -0.5)
        v = hg_ref[0, rows, w:2 * w]
        vb = v.astype(BF16)
        od = jnp.zeros((rt, w), F32)
        for col, lb, cum, keep, qt_s, kt_s, dc_s in (
                (2, lbf, cum_lo, lo_c, qtf_s, ktf_s, dcf_s), (3, lbb, cum_hi, hi_c, qtb_s, ktb_s, dcb_s)):
            k, logf = gates(hg_ref[0, rows, col * w:(col + 1) * w], lb)
            l2 = logf * LOG2_E
            b = _dot_exact_lhs(cum, l2)
            tot = _dot_exact_lhs(ch_sum, l2)
            half = 0.5 * tot
            qs = (q * jnp.exp2(b - half)).astype(BF16)
            ks = (k * jnp.exp2(half - b)).astype(BF16)
            qt_s[rows, :] = (q * jnp.exp2(b)).astype(BF16)
            kt_s[rows, :] = (k * jnp.exp2(tot - b)).astype(BF16)
            dc_s[ti] = jnp.exp2(_dot_exact_lhs(ch_sel, l2)[0:8])
            worst = jnp.maximum(worst, jnp.max(-tot, axis=0, keepdims=True))
            for h in range(HG_HEADS):
                p = jnp.where(keep, _dot_nt(qs * head_cols[h], ks), 0.0)
                od = od + _dot(p.astype(BF16), vb * head_cols[h])
        vt_s[:, rows] = v.T.astype(BF16)
        oacc_s[rows, :] = od
        return worst

    worst = lax.fori_loop(0, n_tiles, chunk_tile, jnp.zeros((1, w), F32))
    exact_needed = jnp.max(worst) > HG_CHUNK_RANGE

    half_w = w // 2
    p_r = lax.broadcasted_iota(jnp.int32, (half_w, half_w), 0) // HG_KEY
    p_c = lax.broadcasted_iota(jnp.int32, (half_w, half_w), 1) // HG_KEY
    pair_mask = jnp.where(p_r == p_c, 1.0, 0.0).astype(F32)
    for s in (sfa_s, sfb_s, sba_s, sbb_s):
        s[...] = jnp.zeros((half_w, half_w), F32)
    lane_c = lax.broadcasted_iota(jnp.int32, (1, LANES), 1)

    def advance_chunk(ci, qt_s, kt_s, dc_s, sa_s, sb_s):
        r = pl.multiple_of(ci * ch, ch)
        rows = pl.ds(r, ch)
        sa = sa_s[...]
        sb = sb_s[...]
        oi = jnp.concatenate([_dot_nt(qt_s[rows, 0:half_w], sa.astype(BF16)),
                              _dot_nt(qt_s[rows, half_w:w], sb.astype(BF16))], axis=1)
        oacc_s[rows, :] = oacc_s[rows, :] + oi
        slab = pl.multiple_of((r // LANES) * LANES, LANES)
        off = r - slab
        vt = vt_s[:, pl.ds(slab, LANES)]
        vt = vt * jnp.where((lane_c >= off) & (lane_c < off + ch), 1.0, 0.0).astype(BF16)
        kt = kt_s[pl.ds(slab, LANES), :]
        dec = dc_s[ci // ch_per_tile, pl.ds(ci % ch_per_tile, 1), :]
        sa_s[...] = sa * dec[:, 0:half_w] + _dot(vt[0:half_w, :], kt[:, 0:half_w]) * pair_mask
        sb_s[...] = sb * dec[:, half_w:w] + _dot(vt[half_w:w, :], kt[:, half_w:w]) * pair_mask

    n_ch = t_len // ch
    n_cch = n_ctx // ch

    def chunk_step(i, carry):
        advance_chunk(i, qtf_s, ktf_s, dcf_s, sfa_s, sfb_s)
        jb = jnp.where(i < n_cch, n_cch - 1 - i, n_ch - 1 - (i - n_cch))
        advance_chunk(jb, qtb_s, ktb_s, dcb_s, sba_s, sbb_s)
        return carry

    lax.fori_loop(0, jnp.where(exact_needed, 0, n_ch), chunk_step, 0)

    n_tiles_x = jnp.where(exact_needed, n_tiles, 0)

    def tile_body(ti, carry):
        rows = pl.ds(pl.multiple_of(ti * rt, rt), rt)
        q = hg_ref[0, rows, 0:w] * (HG_KEY ** -0.5)
        v = hg_ref[0, rows, w:2 * w]
        kf, lf = gates(hg_ref[0, rows, 2 * w:3 * w], lbf)
        kb, lg = gates(hg_ref[0, rows, 3 * w:4 * w], lbb)
        b_f = _dot_exact_lhs(incl_lo, lf)
        tot_f = _dot_exact_lhs(blk_sum, lf)
        c_b = _dot_exact_lhs(incl_hi, lg)
        tot_b = _dot_exact_lhs(blk_sum, lg)
        qtf_s[rows, :] = (q * jnp.exp(b_f)).astype(BF16)
        ktf_s[rows, :] = (kf * jnp.exp(tot_f - b_f)).astype(BF16)
        qtb_s[rows, :] = (q * jnp.exp(c_b)).astype(BF16)
        ktb_s[rows, :] = (kb * jnp.exp(tot_b - c_b)).astype(BF16)
        drows = pl.ds(pl.multiple_of(ti * blk_per_tile, blk_per_tile), blk_per_tile)
        df_s[drows, :] = jnp.exp(_dot_exact_lhs(blk_sel, lf)[0:blk_per_tile])
        db_s[drows, :] = jnp.exp(_dot_exact_lhs(blk_sel, lg)[0:blk_per_tile])
        vt_s[:, rows] = v.T.astype(BF16)

        kf_s[pad:pad + rt, :] = kf
        bf_s[pad:pad + rt, :] = b_f
        kb_s[pad:pad + rt, :] = kb
        cb_s[pad:pad + rt, :] = c_b
        v_s[pad:pad + rt, :] = v
        od = jnp.zeros((rt, w), F32)
        for dl in range(blk):
            lo = pad - dl
            e = q * kf_s[lo:lo + rt, :] * jnp.exp(jnp.minimum(b_f - bf_s[lo:lo + rt, :], 0.0))
            e = jnp.where(pos >= dl, e, 0.0)
            od = od + _dot(e.astype(BF16), head_ones) * v_s[lo:lo + rt, :]
            hi = pad + dl
            e = q * kb_s[hi:hi + rt, :] * jnp.exp(jnp.minimum(c_b - cb_s[hi:hi + rt, :], 0.0))
            e = jnp.where(pos + dl < blk, e, 0.0)
            od = od + _dot(e.astype(BF16), head_ones) * v_s[hi:hi + rt, :]
        oacc_s[rows, :] = od
        return carry

    lax.fori_loop(0, n_tiles_x, tile_body, 0)

    stf_s[...] = jnp.zeros((w, w), F32)
    stb_s[...] = jnp.zeros((w, w), F32)
    n_blk = t_len // blk
    n_cblk = n_ctx // blk
    lane = lax.broadcasted_iota(jnp.int32, (1, LANES), 1)

    def advance(bi, qt_s, kt_s, d_s, st_s):
        r = pl.multiple_of(bi * blk, blk)
        st = st_s[...]
        rows = pl.ds(r, blk)
        oacc_s[rows, :] = oacc_s[rows, :] + _dot_nt(qt_s[rows, :], st.astype(BF16))
        slab = pl.multiple_of((r // LANES) * LANES, LANES)
        off = r - slab
        vt = vt_s[:, pl.ds(slab, LANES)]
        vt = vt * jnp.where((lane >= off) & (lane < off + blk), 1.0, 0.0).astype(BF16)
        upd = _dot(vt, kt_s[pl.ds(slab, LANES), :])
        st_s[...] = st * d_s[pl.ds(bi, 1), :] + upd * head_mask

    def step(i, carry):
        advance(i, qtf_s, ktf_s, df_s, stf_s)
        jb = jnp.where(i < n_cblk, n_cblk - 1 - i, n_blk - 1 - (i - n_cblk))
        advance(jb, qtb_s, ktb_s, db_s, stb_s)
        return carry

    lax.fori_loop(0, jnp.where(exact_needed, n_blk, 0), step, 0)

    gain = gain_ref[...]

    def out_body(ti, carry):
        rows = pl.ds(pl.multiple_of(ti * rt, rt), rt)
        o = oacc_s[rows, :]
        ms = _dot_exact_rhs(o * o, head_ones) * (1.0 / HG_KEY)
        y = o * lax.rsqrt(ms + 1e-6) * gain
        g = hg_ref[0, rows, 4 * w:5 * w]
        o_ref[0, rows, :] = (y * (g * _sigmoid(g))).astype(BF16)
        return carry

    lax.fori_loop(0, n_tiles, out_body, 0)


def _hgrn_call(hg, lbf, lbb, gain_t, n_ctx):
    b, t, _ = hg.shape
    w = HG_WIDTH
    tile_pad = pltpu.VMEM((HG_TILE + 2 * HG_BLOCK, w), F32)
    seq_bf16 = pltpu.VMEM((t, w), BF16)
    dec = pltpu.VMEM((t // HG_BLOCK, w), F32)
    return pl.pallas_call(
        functools.partial(_hgrn_kernel, t_len=t, n_ctx=n_ctx),
        grid=(b,),
        in_specs=[
            pl.BlockSpec((1, t, 5 * w), lambda bi: (bi, 0, 0)),
            pl.BlockSpec((1, w), lambda bi: (0, 0)),
            pl.BlockSpec((1, w), lambda bi: (0, 0)),
            pl.BlockSpec((1, w), lambda bi: (0, 0)),
        ],
        out_specs=pl.BlockSpec((1, t, w), lambda bi: (bi, 0, 0)),
        out_shape=jax.ShapeDtypeStruct((b, t, w), BF16),
        scratch_shapes=[tile_pad] * 5 + [seq_bf16] * 4 + [dec, dec,
                        pltpu.VMEM((w, t), BF16), pltpu.VMEM((t, w), F32),
                        pltpu.VMEM((w, w), F32), pltpu.VMEM((w, w), F32)]
        + [pltpu.VMEM((t // HG_TILE, 8, w), F32)] * 2 + [pltpu.VMEM((w // 2, w // 2), F32)] * 4,
        compiler_params=_cparams(("parallel",)),
        name="hgrn2",
    )(hg, lbf, lbb, gain_t)


def _mod_rows(ml_ref, mc_ref, tm, n_ctx, idx):
    i = pl.program_id(1)
    row = i * tm + lax.broadcasted_iota(jnp.int32, (tm, 1), 0)
    isctx = row < n_ctx
    ml = ml_ref[0]
    mc = mc_ref[0]
    return [jnp.where(isctx, mc[k:k + 1], ml[k:k + 1]) for k in idx]


def _out_kernel(oa_ref, ob_ref, og_ref, wo_ref, x_ref, ml_ref, mc_ref, gain_ref, *rest,
                tm, n_ctx, routed):
    if routed:
        router_ref, xo_ref, h_ref, gates_ref = rest
    else:
        xo_ref, h_ref = rest
    gate, shift, scale = _mod_rows(ml_ref, mc_ref, tm, n_ctx, (2, 3, 4))
    mix = (_dot(oa_ref[0], wo_ref[0:512, :]) + _dot(ob_ref[0], wo_ref[512:768, :])
           + _dot(og_ref[0], wo_ref[768:1024, :]))
    x = x_ref[0] + gate * mix
    xo_ref[0] = x
    y = x * lax.rsqrt(jnp.mean(x * x, axis=-1, keepdims=True) + 1e-6) * gain_ref[...]
    h = y * (1.0 + scale) + shift
    h_ref[0] = h.astype(h_ref.dtype)
    if routed:
        lane = lax.broadcasted_iota(jnp.int32, (1, LANES), 1)
        logits = _dot_f32(h, router_ref[...])
        logits = jnp.where(lane < N_EXPERTS, logits, -jnp.inf)
        m1 = jnp.max(logits, axis=-1, keepdims=True)
        i1 = jnp.min(jnp.where(logits == m1, lane, LANES), axis=-1, keepdims=True)
        rest_l = jnp.where(lane == i1, -jnp.inf, logits)
        m2 = jnp.max(rest_l, axis=-1, keepdims=True)
        i2 = jnp.min(jnp.where(rest_l == m2, lane, LANES), axis=-1, keepdims=True)
        e2 = jnp.exp(m2 - m1)
        w1 = 1.0 / (1.0 + e2)
        route = jnp.where(lane == 0, i1.astype(F32), 0.0) + jnp.where(lane == 1, i2.astype(F32), 0.0)
        gates_ref[0] = route + jnp.where(lane == 2, w1, 0.0) + jnp.where(lane == 3, e2 * w1, 0.0)


def _out_call(oa, ob, og, wo, xa, mods, gain, router_p, n_ctx):
    b, t, d = xa.shape
    tm = _row_tile(t)
    routed = router_p is not None
    row_blk = lambda width: pl.BlockSpec((1, tm, width), lambda bi, i: (bi, i, 0))
    in_specs = [
        row_blk(oa.shape[-1]), row_blk(ob.shape[-1]), row_blk(og.shape[-1]),
        _resident(wo.shape),
        row_blk(d),
        pl.BlockSpec((1, 6, d), lambda bi, i: (bi, 0, 0)),
        pl.BlockSpec((1, 6, d), lambda bi, i: (b, 0, 0)),
        _resident((1, d)),
    ]
    args = [oa, ob, og, wo, xa, mods, mods, gain]
    out_specs = [row_blk(d), row_blk(d)]
    out_shape = [jax.ShapeDtypeStruct((b, t, d), F32),
                 jax.ShapeDtypeStruct((b, t, d), F32 if routed else BF16)]
    if routed:
        in_specs.append(_resident(router_p.shape))
        args.append(router_p)
        out_specs.append(row_blk(LANES))
        out_shape.append(jax.ShapeDtypeStruct((b, t, LANES), F32))
    return pl.pallas_call(
        functools.partial(_out_kernel, tm=tm, n_ctx=n_ctx, routed=routed),
        grid=(b, t // tm),
        in_specs=in_specs,
        out_specs=out_specs,
        out_shape=out_shape,
        compiler_params=_cparams(("parallel", "parallel")),
        name="out_proj_routed" if routed else "out_proj",
    )(*args)


def _swiglu_into(acc_ref, h, wg_ref, wu_ref, wd_ref, row_gate, d_ff):
    for c in range(d_ff // FF_CHUNK):
        cs = slice(c * FF_CHUNK, (c + 1) * FF_CHUNK)
        g = _dot(h, wg_ref[0, :, cs])
        u = _dot(h, wu_ref[0, :, cs])
        a = g * _sigmoid(g) * u
        if row_gate is not None:
            a = a * row_gate
        acc_ref[...] += _dot(a.astype(BF16), wd_ref[0, cs, :])


def _ffn_kernel(h_ref, x_ref, ml_ref, mc_ref, wg_ref, wu_ref, wd_ref, xo_ref, acc_ref,
                *, tm, n_ctx, d_ff):
    (gate,) = _mod_rows(ml_ref, mc_ref, tm, n_ctx, (5,))
    acc_ref[...] = jnp.zeros_like(acc_ref)
    _swiglu_into(acc_ref, h_ref[0], wg_ref, wu_ref, wd_ref, None, d_ff)
    xo_ref[0] = x_ref[0] + gate * acc_ref[...]


def _ffn_call(h, xa, mods, wg, wu, wd, n_ctx):
    b, t, d = xa.shape
    d_ff = wg.shape[-1]
    tm = _row_tile(t)
    row_blk = pl.BlockSpec((1, tm, d), lambda bi, i: (bi, i, 0))
    return pl.pallas_call(
        functools.partial(_ffn_kernel, tm=tm, n_ctx=n_ctx, d_ff=d_ff),
        grid=(b, t // tm),
        in_specs=[
            row_blk, row_blk,
            pl.BlockSpec((1, 6, d), lambda bi, i: (bi, 0, 0)),
            pl.BlockSpec((1, 6, d), lambda bi, i: (b, 0, 0)),
            _resident((1, d, d_ff)), _resident((1, d, d_ff)), _resident((1, d_ff, d)),
        ],
        out_specs=row_blk,
        out_shape=jax.ShapeDtypeStruct((b, t, d), F32),
        scratch_shapes=[pltpu.VMEM((tm, d), F32)],
        compiler_params=_cparams(("parallel", "parallel")),
        name="ffn_dense",
    )(h, xa, mods, mods, wg[None], wu[None], wd[None])


def _row_copy(src, src_row, dst, dst_row, sem):
    return pltpu.make_async_copy(src.at[pl.ds(src_row, 1), :], dst.at[pl.ds(dst_row, 1), :], sem)


def _dispatch_kernel(pos_ref, pad_ref, h_ref, xs_ref, zero_s, sem, zsem, *, tm, n_exp, n_tiles):
    step = pl.program_id(0)
    base = step * tm

    def pad_copies(act):
        for e in range(n_exp):
            start = pad_ref[e]

            def tail_row(r, carry):
                act(_row_copy(zero_s, 0, xs_ref, start + r, zsem))
                return carry

            lax.fori_loop(0, pad_ref[n_exp + e], tail_row, 0)
        for j in range(n_exp):
            tile = pad_ref[2 * n_exp] + j

            @pl.when(tile < n_tiles)
            def _():
                row0 = pl.multiple_of(tile * tm, tm)
                act(pltpu.make_async_copy(zero_s, xs_ref.at[pl.ds(row0, tm), :], zsem))

    @pl.when(step == 0)
    def _():
        zero_s[...] = jnp.zeros_like(zero_s)
        pad_copies(lambda c: c.start())

    def issue(r, carry):
        for s in range(TOP_K):
            _row_copy(h_ref, r, xs_ref, pos_ref[TOP_K * (base + r) + s], sem).start()
        return carry

    lax.fori_loop(0, tm, issue, 0, unroll=8)

    def drain(r, carry):
        for s in range(TOP_K):
            _row_copy(h_ref, r, xs_ref, pos_ref[TOP_K * (base + r) + s], sem).wait()
        return carry

    lax.fori_loop(0, tm, drain, 0, unroll=8)

    @pl.when(step == 0)
    def _():
        pad_copies(lambda c: c.wait())


def _dispatch_call(pos, pad_info, h2d, n_tiles, n_exp):
    n, d = h2d.shape
    tm = MOE_TILE
    return pl.pallas_call(
        functools.partial(_dispatch_kernel, tm=tm, n_exp=n_exp, n_tiles=n_tiles),
        grid_spec=pltpu.PrefetchScalarGridSpec(
            num_scalar_prefetch=2,
            grid=(n // tm,),
            in_specs=[pl.BlockSpec((tm, d), lambda i, pos, pad: (i, 0))],
            out_specs=pl.BlockSpec(memory_space=pl.ANY),
            scratch_shapes=[pltpu.VMEM((tm, d), F32), pltpu.SemaphoreType.DMA(()),
                            pltpu.SemaphoreType.DMA(())],
        ),
        out_shape=jax.ShapeDtypeStruct((n_tiles * tm, d), F32),
        compiler_params=_cparams(("arbitrary",)),
        name="moe_dispatch",
    )(pos, pad_info, h2d)


def _expert_kernel(te_ref, nv_ref, xs_ref, wg_ref, wu_ref, wd_ref, ys_ref, acc_ref, *, tm, d_ff):
    n_valid = nv_ref[pl.program_id(0)]

    @pl.when(n_valid > 0)
    def _():
        acc_ref[...] = jnp.zeros_like(acc_ref)
        _swiglu_into(acc_ref, xs_ref[...].astype(BF16), wg_ref, wu_ref, wd_ref, None, d_ff)
        ys_ref[...] = acc_ref[...]

    @pl.when(n_valid == 0)
    def _():
        ys_ref[...] = jnp.zeros_like(ys_ref)


def _expert_call(tile_expert, tile_valid, xs, wg, wu, wd):
    n_slots, d = xs.shape
    d_ff = wg.shape[-1]
    tm = MOE_TILE
    w_blk = lambda shape: pl.BlockSpec(shape, lambda t, te, nv: (te[t], 0, 0))
    return pl.pallas_call(
        functools.partial(_expert_kernel, tm=tm, d_ff=d_ff),
        grid_spec=pltpu.PrefetchScalarGridSpec(
            num_scalar_prefetch=2,
            grid=(n_slots // tm,),
            in_specs=[
                pl.BlockSpec((tm, d), lambda t, te, nv: (t, 0)),
                w_blk((1, d, d_ff)), w_blk((1, d, d_ff)), w_blk((1, d_ff, d)),
            ],
            out_specs=pl.BlockSpec((tm, d), lambda t, te, nv: (t, 0)),
            scratch_shapes=[pltpu.VMEM((tm, d), F32)],
        ),
        out_shape=jax.ShapeDtypeStruct((n_slots, d), F32),
        compiler_params=_cparams(("arbitrary",)),
        name="moe_experts",
    )(tile_expert, tile_valid, xs, wg, wu, wd)


def _combine_kernel(pos_ref, ys_ref, x_ref, route_ref, ml_ref, mc_ref, xo_ref, y1_s, y2_s, sem,
                    *, tm, n_ctx):
    base = (pl.program_id(0) * pl.num_programs(1) + pl.program_id(1)) * tm
    bufs = (y1_s, y2_s)

    def issue(r, carry):
        for s in range(TOP_K):
            _row_copy(ys_ref, pos_ref[TOP_K * (base + r) + s], bufs[s], r, sem).start()
        return carry

    lax.fori_loop(0, tm, issue, 0, unroll=8)

    def drain(r, carry):
        for s in range(TOP_K):
            _row_copy(ys_ref, pos_ref[TOP_K * (base + r) + s], bufs[s], r, sem).wait()
        return carry

    lax.fori_loop(0, tm, drain, 0, unroll=8)

    lane = lax.broadcasted_iota(jnp.int32, (1, LANES), 1)
    route = route_ref[0]
    w1 = jnp.sum(jnp.where(lane == 2, route, 0.0), axis=-1, keepdims=True)
    w2 = jnp.sum(jnp.where(lane == 3, route, 0.0), axis=-1, keepdims=True)
    (gate,) = _mod_rows(ml_ref, mc_ref, tm, n_ctx, (5,))
    xo_ref[0] = x_ref[0] + gate * (w1 * y1_s[...] + w2 * y2_s[...])


def _combine_call(pos, ys, xa, route, mods, n_ctx):
    b, t, d = xa.shape
    tm = _row_tile(t)
    row_blk = lambda width: pl.BlockSpec((1, tm, width), lambda bi, i, pos: (bi, i, 0))
    return pl.pallas_call(
        functools.partial(_combine_kernel, tm=tm, n_ctx=n_ctx),
        grid_spec=pltpu.PrefetchScalarGridSpec(
            num_scalar_prefetch=1,
            grid=(b, t // tm),
            in_specs=[
                pl.BlockSpec(memory_space=pl.ANY),
                row_blk(d), row_blk(LANES),
                pl.BlockSpec((1, 6, d), lambda bi, i, pos: (bi, 0, 0)),
                pl.BlockSpec((1, 6, d), lambda bi, i, pos: (b, 0, 0)),
            ],
            out_specs=row_blk(d),
            scratch_shapes=[pltpu.VMEM((tm, d), F32), pltpu.VMEM((tm, d), F32),
                            pltpu.SemaphoreType.DMA(())],
        ),
        out_shape=jax.ShapeDtypeStruct((b, t, d), F32),
        compiler_params=_cparams(("arbitrary", "arbitrary")),
        name="moe_combine",
    )(pos, ys, xa, route, mods, mods)


def _moe_call(h, xa, route, mods, wg, wu, wd, n_ctx):
    b, t, d = xa.shape
    n = b * t
    tm = MOE_TILE
    n_exp = wg.shape[0]
    n_tiles = (TOP_K * n) // tm + n_exp
    choice = route.reshape(n, LANES)[:, :TOP_K].astype(jnp.int32).reshape(-1)
    onehot = (choice[:, None] == jnp.arange(n_exp, dtype=jnp.int32)[None, :]).astype(jnp.int32)
    csum = jnp.cumsum(onehot, axis=0)
    counts = csum[-1]
    rank = jnp.sum((csum - onehot) * onehot, axis=1)
    group_tiles = (counts + tm - 1) // tm
    tile_end = jnp.cumsum(group_tiles)
    group_row0 = (tile_end - group_tiles) * tm
    pos = jnp.sum(group_row0[None, :] * onehot, axis=1) + rank
    tile_id = jnp.arange(n_tiles, dtype=jnp.int32)
    tile_group = jnp.sum((tile_id[:, None] >= tile_end[None, :]).astype(jnp.int32), axis=1)
    last_group = jnp.max(jnp.where(counts > 0, jnp.arange(n_exp, dtype=jnp.int32), 0))
    tile_expert = jnp.minimum(tile_group, last_group)
    in_group = tile_group < n_exp
    g = jnp.minimum(tile_group, n_exp - 1)
    tile_valid = jnp.where(in_group, jnp.clip(counts[g] - (tile_id * tm - group_row0[g]), 0, tm), 0)

    pad_info = jnp.concatenate([group_row0 + counts, group_tiles * tm - counts, tile_end[-1:]])
    xs = _dispatch_call(pos, pad_info.astype(jnp.int32), h.reshape(n, d), n_tiles, n_exp)
    ys = _expert_call(tile_expert.astype(jnp.int32), tile_valid.astype(jnp.int32), xs, wg, wu, wd)
    return _combine_call(pos, ys, xa, route, mods, n_ctx)


def _final_kernel(x_ref, g_ref, o_ref):
    x = x_ref[0]
    o_ref[0] = x * lax.rsqrt(jnp.mean(x * x, axis=-1, keepdims=True) + 1e-6) * g_ref[...]


def _final_call(xa, gain, n_ctx):
    b, t, d = xa.shape
    tm = 256
    skip = n_ctx // tm
    return pl.pallas_call(
        _final_kernel,
        grid=(b, (t - n_ctx) // tm),
        in_specs=[pl.BlockSpec((1, tm, d), lambda bi, i: (bi, i + skip, 0)), _resident((1, d))],
        out_specs=pl.BlockSpec((1, tm, d), lambda bi, i: (bi, i, 0)),
        out_shape=jax.ShapeDtypeStruct((b, t - n_ctx, d), F32),
        compiler_params=_cparams(("parallel", "parallel")),
        name="final_norm",
    )(xa, gain)


def _rope_tables(seq, n_ctx):
    rows = seq // GRID_W
    row, col = jnp.meshgrid(jnp.arange(rows, dtype=F32), jnp.arange(GRID_W, dtype=F32), indexing="ij")
    row, col = row.reshape(-1), col.reshape(-1)

    def angles(rot_dim):
        n_freq = rot_dim // 4
        inv = ROPE_BASE ** (-jnp.arange(n_freq, dtype=F32) / n_freq)
        ang = jnp.concatenate([row[:, None] * inv, col[:, None] * inv], axis=-1)
        return jnp.cos(ang), jnp.sin(ang)

    def with_ctx(a, fill):
        return jnp.concatenate([jnp.full((n_ctx, a.shape[1]), fill, F32), a], axis=0)

    cos, sin = angles(DA_HEAD_DIM)
    cos_da = with_ctx(jnp.tile(cos, (1, 4)), 1.0)
    sin_da = with_ctx(jnp.tile(jnp.concatenate([-sin, sin], axis=1), (1, 2)), 0.0)
    cos, sin = angles(MLA_ROPE)
    ones = jnp.ones((seq, MLA_NOPE), F32)
    tail = jnp.ones((seq, LANES - MLA_NOPE - MLA_ROPE), F32)
    cos_m = with_ctx(jnp.concatenate([ones, cos, cos, tail], axis=1), 1.0)
    sin_m = with_ctx(jnp.concatenate([0 * ones, -sin, sin, 0 * tail], axis=1), 0.0)
    return cos_da, sin_da, cos_m, sin_m


def _pad_in_proj(w_in_l):
    d = w_in_l.shape[0]
    cut = 1536 + MLA_Q_RANK + MLA_KV_RANK + MLA_ROPE
    padw = 2048 - cut
    return jnp.concatenate([w_in_l[:, :cut], jnp.zeros((d, padw), w_in_l.dtype), w_in_l[:, cut:]],
                           axis=1).astype(BF16)


def _pad_uq(w_uq_l):
    qk = MLA_NOPE + MLA_ROPE
    w = w_uq_l.reshape(MLA_Q_RANK, MLA_HEADS, qk)
    w = jnp.pad(w, ((0, 0), (0, 0), (0, LANES - qk)))
    return w.reshape(MLA_Q_RANK, MLA_HEADS * LANES).astype(BF16)


def _pad_ukv(w_ukv_l):
    w = w_ukv_l.reshape(MLA_KV_RANK, MLA_HEADS, MLA_NOPE + MLA_V)
    k_part = jnp.pad(w[:, :, :MLA_NOPE], ((0, 0), (0, 0), (0, LANES - MLA_NOPE)))
    k_part = k_part.reshape(MLA_KV_RANK, MLA_HEADS * LANES)
    v_part = w[:, :, MLA_NOPE:].reshape(MLA_KV_RANK, MLA_HEADS * MLA_V)
    place = jnp.zeros((MLA_ROPE, MLA_HEADS, LANES), F32)
    eye = jnp.eye(MLA_ROPE, dtype=F32)
    place = place.at[:, :, MLA_NOPE:MLA_NOPE + MLA_ROPE].set(eye[:, None, :])
    place = place.reshape(MLA_ROPE, MLA_HEADS * LANES)
    top = jnp.concatenate([k_part, v_part], axis=1)
    mid = jnp.concatenate([place, jnp.zeros((MLA_ROPE, MLA_HEADS * MLA_V), F32)], axis=1)
    bot = jnp.zeros((MXU_COLS - MLA_KV_RANK - MLA_ROPE, top.shape[1]), F32)
    return jnp.concatenate([top, mid, bot], axis=0).astype(BF16)


def kernel(x, c, ctx, c_ctx, w_mod, b_mod, norm_mix, norm_ffn, norm_final, w_in, w_out, diff_lambda, diff_norm, mla_q_norm, mla_w_uq, mla_kv_norm, mla_w_ukv, hgrn_lb_logits, hgrn_norm, ffn_w_gate, ffn_w_up, ffn_w_down, moe_router, moe_w_gate, moe_w_up, moe_w_down):
    b, seq, d = x.shape
    n_ctx = ctx.shape[1]
    depth = w_mod.shape[0]
    assert d == 1024 and n_ctx % ATT_SUB == 0 and seq % ATT_SUB == 0

    xa = jnp.concatenate([ctx, x], axis=1)
    bp = -(-(b + 1) // 8) * 8
    c_all = jnp.concatenate([c, c_ctx[None, :], jnp.zeros((bp - b - 1, d), F32)], axis=0)
    mods = _mod_call(c_all, w_mod, b_mod).reshape(depth, bp, 6, d)

    lb = jnp.cumsum(jax.nn.softmax(hgrn_lb_logits.astype(F32), axis=1), axis=1)
    lb = lb - lb[:, :1]
    tabs = _rope_tables(seq, n_ctx)

    for l in range(depth):
        lam_init = 0.8 - 0.6 * math.exp(-0.3 * l)
        q_da, k_da, v_da, q_m, k_m, v_m, hg = _in_call(
            xa, norm_mix[l][None], mods[l], _pad_in_proj(w_in[l]), tabs,
            mla_q_norm[l][None], mla_kv_norm[l][None], _pad_uq(mla_w_uq[l]), _pad_ukv(mla_w_ukv[l]),
            n_ctx)
        oa = _da_call(q_da, k_da, v_da, diff_lambda[l], jnp.full((1, 1), lam_init, F32),
                      (diff_norm[l] * (1.0 - lam_init))[None], n_ctx)
        ob = _mla_call(q_m, k_m, v_m, n_ctx)
        og = _hgrn_call(hg, lb[0, l][None], lb[1, l][None], jnp.tile(hgrn_norm[l], HG_HEADS)[None], n_ctx)

        wo = w_out[l].astype(BF16)
        if l % 2 == 0:
            xa, h = _out_call(oa, ob, og, wo, xa, mods[l], norm_ffn[l][None], None, n_ctx)
            xa = _ffn_call(h, xa, mods[l], ffn_w_gate[l // 2].astype(BF16),
                           ffn_w_up[l // 2].astype(BF16), ffn_w_down[l // 2].astype(BF16), n_ctx)
        else:
            router_p = jnp.pad(moe_router[l // 2], ((0, 0), (0, LANES - N_EXPERTS)))
            xa, h, gates = _out_call(oa, ob, og, wo, xa, mods[l], norm_ffn[l][None], router_p, n_ctx)
            xa = _moe_call(h, xa, gates, mods[l], moe_w_gate[l // 2].astype(BF16),
                           moe_w_up[l // 2].astype(BF16), moe_w_down[l // 2].astype(BF16), n_ctx)
    return _final_call(xa, norm_final[None], n_ctx)
```

```python
import functools
import math

import jax
import jax.numpy as jnp
from jax import lax
from jax.experimental import pallas as pl
from jax.experimental.pallas import tpu as pltpu

F32 = jnp.float32
BF16 = jnp.bfloat16

GRID_W = 64
ROPE_BASE = 10000.0
DA_HEADS = 4
DA_HEAD_DIM = 64
MLA_HEADS = 4
MLA_Q_RANK = 256
MLA_KV_RANK = 128
MLA_NOPE = 64
MLA_ROPE = 32
MLA_V = 64
HG_HEADS = 4
HG_KEY = 64
HG_WIDTH = HG_HEADS * HG_KEY
N_EXPERTS = 8
TOP_K = 2
LOG2_E = 1.4426950408889634

LANES = 128
MXU_COLS = 256
VMEM_LIMIT = 56 * 1024 * 1024

HG_BLOCK = 16
HG_CHUNK = 64
HG_CHUNK_RANGE = 180.0
HG_TILE = 128
ATT_SUB = 256
FF_CHUNK = 256
MOE_TILE = 512


def _dot(a, b):
    return jnp.dot(a, b, preferred_element_type=F32)


def _dot_nt(a, b):
    return lax.dot_general(a, b, (((1,), (1,)), ((), ())), preferred_element_type=F32)


def _split3(x):
    x0 = x.astype(BF16)
    r = x - x0.astype(F32)
    x1 = r.astype(BF16)
    x2 = (r - x1.astype(F32)).astype(BF16)
    return x0, x1, x2


def _dot_exact_lhs(a_bf16, x):
    x0, x1, x2 = _split3(x)
    return _dot(a_bf16, x0) + _dot(a_bf16, x1) + _dot(a_bf16, x2)


def _dot_exact_rhs(x, b_bf16):
    x0, x1, x2 = _split3(x)
    return _dot(x0, b_bf16) + _dot(x1, b_bf16) + _dot(x2, b_bf16)


def _dot_f32(a, b):
    a0 = a.astype(BF16)
    a1 = (a - a0.astype(F32)).astype(BF16)
    b0 = b.astype(BF16)
    b1 = (b - b0.astype(F32)).astype(BF16)
    return _dot(a0, b0) + _dot(a0, b1) + _dot(a1, b0)


def _sigmoid(z):
    return 1.0 / (1.0 + jnp.exp(-z))


def _row_tile(t):
    return 768 if t % 768 == 0 else 256


def _cparams(sem):
    return pltpu.CompilerParams(dimension_semantics=sem, vmem_limit_bytes=VMEM_LIMIT)


def _resident(shape):
    nd = len(shape)
    return pl.BlockSpec(shape, lambda *_: (0,) * nd, pipeline_mode=pl.Buffered(1))


def _mod_kernel(c_ref, w_ref, b_ref, o_ref):
    c = c_ref[...]
    o_ref[0] = _dot_f32(c * _sigmoid(c), w_ref[0]) + b_ref[0]


def _mod_call(c_all, w_mod, b_mod):
    depth, d, n6 = w_mod.shape
    bp = c_all.shape[0]
    tn = 1536
    return pl.pallas_call(
        _mod_kernel,
        grid=(depth, n6 // tn),
        in_specs=[
            pl.BlockSpec((bp, d), lambda l, j: (0, 0)),
            pl.BlockSpec((1, d, tn), lambda l, j: (l, 0, j)),
            pl.BlockSpec((1, 1, tn), lambda l, j: (l, 0, j)),
        ],
        out_specs=pl.BlockSpec((1, bp, tn), lambda l, j: (l, 0, j)),
        out_shape=jax.ShapeDtypeStruct((depth, bp, n6), F32),
        compiler_params=_cparams(("parallel", "parallel")),
        name="modulation",
    )(c_all, w_mod, b_mod.reshape(depth, 1, n6))


def _rope128(seg, cos, sin, mla):
    lane = lax.broadcasted_iota(jnp.int32, (1, LANES), 1)
    if mla:
        half = MLA_ROPE // 2
        take_next = lane < MLA_NOPE + half
    else:
        half = DA_HEAD_DIM // 2
        take_next = (lane % DA_HEAD_DIM) < half
    sw = jnp.where(take_next, pltpu.roll(seg, LANES - half, 1), pltpu.roll(seg, half, 1))
    return seg * cos + sw * sin


def _in_kernel(x_ref, gain_ref, ml_ref, mc_ref, w_ref, cd_ref, sd_ref, cm_ref, sm_ref,
               qn_ref, kvn_ref, wuq_ref, wkv_ref,
               oq_ref, ok_ref, ov_ref, mq_ref, mk_ref, mv_ref, hg_ref, *, tm, n_ctx):
    i = pl.program_id(1)
    x = x_ref[0]
    y = x * lax.rsqrt(jnp.mean(x * x, axis=-1, keepdims=True) + 1e-6) * gain_ref[...]
    row = i * tm + lax.broadcasted_iota(jnp.int32, (tm, 1), 0)
    isctx = row < n_ctx
    ml = ml_ref[0]
    mc = mc_ref[0]
    shift = jnp.where(isctx, mc[0:1], ml[0:1])
    scale = jnp.where(isctx, mc[1:2], ml[1:2])
    h = (y * (1.0 + scale) + shift).astype(BF16)

    cd, sd = cd_ref[...], sd_ref[...]
    cm, sm = cm_ref[...], sm_ref[...]

    da_scale = DA_HEAD_DIM ** -0.5 * LOG2_E
    aq = _dot(h, w_ref[:, 0:512])
    for g in range(4):
        sl = slice(g * LANES, (g + 1) * LANES)
        oq_ref[0, :, sl] = (_rope128(aq[:, sl], cd, sd, False) * da_scale).astype(BF16)
    ak = _dot(h, w_ref[:, 512:1024])
    for g in range(4):
        sl = slice(g * LANES, (g + 1) * LANES)
        ok_ref[0, :, sl] = _rope128(ak[:, sl], cd, sd, False).astype(BF16)
    ov_ref[0] = _dot(h, w_ref[:, 1024:1536]).astype(BF16)

    mla_scale = (MLA_NOPE + MLA_ROPE) ** -0.5 * LOG2_E
    cc = _dot(h, w_ref[:, 1536:2048])
    cq = cc[:, 0:MLA_Q_RANK]
    qn = (cq * lax.rsqrt(jnp.mean(cq * cq, axis=-1, keepdims=True) + 1e-6) * qn_ref[...]).astype(BF16)
    qu = _dot(qn, wuq_ref[...])
    for g in range(4):
        sl = slice(g * LANES, (g + 1) * LANES)
        mq_ref[0, :, sl] = (_rope128(qu[:, sl], cm, sm, True) * mla_scale).astype(BF16)
    ckv = cc[:, MLA_Q_RANK:MLA_Q_RANK + MLA_KV_RANK]
    rest = cc[:, MLA_Q_RANK + MLA_KV_RANK:]
    kvn = ckv * lax.rsqrt(jnp.mean(ckv * ckv, axis=-1, keepdims=True) + 1e-6) * kvn_ref[...]
    t2 = jnp.concatenate([kvn, rest], axis=1).astype(BF16)
    kv = _dot(t2, wkv_ref[...])
    for g in range(4):
        sl = slice(g * LANES, (g + 1) * LANES)
        mk_ref[0, :, sl] = _rope128(kv[:, sl], cm, sm, True).astype(BF16)
    mv_ref[0] = kv[:, 512:768].astype(BF16)

    hg_ref[0] = _dot(h, w_ref[:, 2048:3328])


def _in_call(xa, gain, mods, w_in_p, tabs, qn, kvn, wuq_p, wkv_p, n_ctx):
    b, t, d = xa.shape
    tm = _row_tile(t)
    cd, sd, cm, sm = tabs
    row_blk = lambda width: pl.BlockSpec((1, tm, width), lambda bi, i: (bi, i, 0))
    tab_blk = pl.BlockSpec((tm, LANES), lambda bi, i: (i, 0))
    out_w = (512, 512, 512, 512, 512, 256)
    return pl.pallas_call(
        functools.partial(_in_kernel, tm=tm, n_ctx=n_ctx),
        grid=(b, t // tm),
        in_specs=[
            row_blk(d),
            _resident((1, d)),
            pl.BlockSpec((1, 6, d), lambda bi, i: (bi, 0, 0)),
            pl.BlockSpec((1, 6, d), lambda bi, i: (b, 0, 0)),
            _resident(w_in_p.shape),
            tab_blk, tab_blk, tab_blk, tab_blk,
            _resident((1, MLA_Q_RANK)),
            _resident((1, MLA_KV_RANK)),
            _resident(wuq_p.shape),
            _resident(wkv_p.shape),
        ],
        out_specs=[row_blk(w) for w in out_w] + [row_blk(5 * HG_WIDTH)],
        out_shape=[jax.ShapeDtypeStruct((b, t, w), BF16) for w in out_w]
        + [jax.ShapeDtypeStruct((b, t, 5 * HG_WIDTH), F32)],
        compiler_params=_cparams(("parallel", "parallel")),
        name="in_proj",
    )(xa, gain, mods, mods, w_in_p, cd, sd, cm, sm, qn, kvn, wuq_p, wkv_p)


def _softmax_parts(s):
    m = jnp.max(s, axis=-1, keepdims=True)
    p = jnp.exp2(s - m)
    return p.astype(BF16), 1.0 / jnp.sum(p, axis=-1, keepdims=True)


def _attend_tile(attend, qi, q_ref, k_ref, v_ref, o_ref, tq, n_ctx):
    n_sub = tq // ATT_SUB
    ctx_sub = n_ctx // ATT_SUB
    for j in range(n_sub):
        rows = slice(j * ATT_SUB, (j + 1) * ATT_SUB)
        if j < ctx_sub:
            @pl.when(qi == 0)
            def _():
                o_ref[0, rows, :] = attend(q_ref[0, rows, :], k_ref[0, 0:n_ctx, :], v_ref[0, 0:n_ctx, :])

            @pl.when(qi != 0)
            def _():
                o_ref[0, rows, :] = attend(q_ref[0, rows, :], k_ref[0], v_ref[0])
        else:
            o_ref[0, rows, :] = attend(q_ref[0, rows, :], k_ref[0], v_ref[0])


def _da_kernel(q_ref, k_ref, v_ref, lamv_ref, lami_ref, gain_ref, o_ref, *, tq, n_ctx):
    qi = pl.program_id(2)
    lane = lax.broadcasted_iota(jnp.int32, (1, LANES), 1)
    map1 = jnp.where(lane < DA_HEAD_DIM, 1.0, 0.0).astype(BF16)
    map2 = jnp.where(lane < DA_HEAD_DIM, 0.0, 1.0).astype(BF16)
    lv = lamv_ref[...]
    lam = (jnp.exp(jnp.sum(lv[0:1] * lv[1:2], axis=-1, keepdims=True))
           - jnp.exp(jnp.sum(lv[2:3] * lv[3:4], axis=-1, keepdims=True)) + lami_ref[...])
    gain = gain_ref[...]

    def attend(q, k, v):
        p1, r1 = _softmax_parts(_dot_nt(q * map1, k))
        p2, r2 = _softmax_parts(_dot_nt(q * map2, k))
        o = _dot(p1, v) * r1 - _dot(p2, v) * (lam * r2)
        return (o * lax.rsqrt(jnp.mean(o * o, axis=-1, keepdims=True) + 1e-5) * gain).astype(BF16)

    _attend_tile(attend, qi, q_ref, k_ref, v_ref, o_ref, tq, n_ctx)


def _da_call(q, k, v, lam_vec, lam_init, gain_eff, n_ctx):
    b, t, _ = q.shape
    tq = _row_tile(t)
    return pl.pallas_call(
        functools.partial(_da_kernel, tq=tq, n_ctx=n_ctx),
        grid=(b, DA_HEADS, t // tq),
        in_specs=[
            pl.BlockSpec((1, tq, LANES), lambda bi, h, i: (bi, i, h)),
            pl.BlockSpec((1, t, LANES), lambda bi, h, i: (bi, 0, h)),
            pl.BlockSpec((1, t, LANES), lambda bi, h, i: (bi, 0, h)),
            pl.BlockSpec((4, DA_HEAD_DIM), lambda bi, h, i: (0, 0)),
            pl.BlockSpec((1, 1), lambda bi, h, i: (0, 0)),
            pl.BlockSpec((1, LANES), lambda bi, h, i: (0, 0)),
        ],
        out_specs=pl.BlockSpec((1, tq, LANES), lambda bi, h, i: (bi, i, h)),
        out_shape=jax.ShapeDtypeStruct((b, t, DA_HEADS * LANES), BF16),
        compiler_params=_cparams(("parallel", "parallel", "parallel")),
        name="diff_attention",
    )(q, k, v, lam_vec, lam_init, gain_eff)


def _mla_kernel(q_ref, k_ref, v_ref, o_ref, *, tq, n_ctx):
    qi = pl.program_id(1)
    lane = lax.broadcasted_iota(jnp.int32, (1, LANES), 1)
    low_half = lane < MLA_V

    def attend(q, k, v):
        pairs = []
        for pr in range(MLA_HEADS // 2):
            vp = v[:, pr * LANES:(pr + 1) * LANES]
            outs = []
            for h in (2 * pr, 2 * pr + 1):
                sl = slice(h * LANES, (h + 1) * LANES)
                p, r = _softmax_parts(_dot_nt(q[:, sl], k[:, sl]))
                outs.append(_dot(p, vp) * r)
            pairs.append(jnp.where(low_half, outs[0], outs[1]))
        return jnp.concatenate(pairs, axis=1).astype(BF16)

    _attend_tile(attend, qi, q_ref, k_ref, v_ref, o_ref, tq, n_ctx)


def _mla_call(q, k, v, n_ctx):
    b, t, _ = q.shape
    tq = _row_tile(t)
    wv = MLA_HEADS * MLA_V
    return pl.pallas_call(
        functools.partial(_mla_kernel, tq=tq, n_ctx=n_ctx),
        grid=(b, t // tq),
        in_specs=[
            pl.BlockSpec((1, tq, MLA_HEADS * LANES), lambda bi, i: (bi, i, 0)),
            pl.BlockSpec((1, t, MLA_HEADS * LANES), lambda bi, i: (bi, 0, 0)),
            pl.BlockSpec((1, t, wv), lambda bi, i: (bi, 0, 0)),
        ],
        out_specs=pl.BlockSpec((1, tq, wv), lambda bi, i: (bi, i, 0)),
        out_shape=jax.ShapeDtypeStruct((b, t, wv), BF16),
        compiler_params=_cparams(("parallel", "parallel")),
        name="latent_attention",
    )(q, k, v)


def _hgrn_kernel(hg_ref, lbf_ref, lbb_ref, gain_ref, o_ref,
                 kf_s, bf_s, kb_s, cb_s, v_s, qtf_s, ktf_s, qtb_s, ktb_s,
                 df_s, db_s, vt_s, oacc_s, stf_s, stb_s,
                 dcf_s, dcb_s, sfa_s, sfb_s, sba_s, sbb_s, *, t_len, n_ctx):
    w = HG_WIDTH
    rt = HG_TILE
    blk = HG_BLOCK
    pad = blk
    n_tiles = t_len // rt
    blk_per_tile = rt // blk

    r_i = lax.broadcasted_iota(jnp.int32, (rt, rt), 0)
    c_i = lax.broadcasted_iota(jnp.int32, (rt, rt), 1)
    same = (r_i // blk) == (c_i // blk)
    incl_lo = jnp.where(same & (c_i <= r_i), 1.0, 0.0).astype(BF16)
    incl_hi = jnp.where(same & (c_i >= r_i), 1.0, 0.0).astype(BF16)
    blk_sum = jnp.where(same, 1.0, 0.0).astype(BF16)
    sel_rows = 16
    g_r = lax.broadcasted_iota(jnp.int32, (sel_rows, rt), 0)
    g_c = lax.broadcasted_iota(jnp.int32, (sel_rows, rt), 1)
    blk_sel = jnp.where((g_c // blk) == g_r, 1.0, 0.0).astype(BF16)
    h_r = lax.broadcasted_iota(jnp.int32, (w, w), 0) // HG_KEY
    h_c = lax.broadcasted_iota(jnp.int32, (w, w), 1) // HG_KEY
    head_ones = jnp.where(h_r == h_c, 1.0, 0.0).astype(BF16)
    head_mask = jnp.where(h_r == h_c, 1.0, 0.0).astype(F32)
    pos = lax.broadcasted_iota(jnp.int32, (rt, 1), 0) % blk

    lbf = lbf_ref[...]
    lbb = lbb_ref[...]

    zpad = jnp.zeros((pad, w), F32)
    for s in (kf_s, bf_s, kb_s, cb_s, v_s):
        s[0:pad, :] = zpad
        s[pad + rt:pad + rt + pad, :] = zpad

    def gates(z, lb):
        f = lb + (1.0 - lb) * _sigmoid(z)
        return (1.0 - lb) * _sigmoid(-z), jnp.log(f)

    ch = HG_CHUNK
    ch_per_tile = rt // ch
    same_c = (r_i // ch) == (c_i // ch)
    lo_c = same_c & (c_i <= r_i)
    hi_c = same_c & (c_i >= r_i)
    cum_lo = jnp.where(lo_c, 1.0, 0.0).astype(BF16)
    cum_hi = jnp.where(hi_c, 1.0, 0.0).astype(BF16)
    ch_sum = jnp.where(same_c, 1.0, 0.0).astype(BF16)
    ch_sel = jnp.where((g_c // ch) == g_r, 1.0, 0.0).astype(BF16)
    head_of_lane = lax.broadcasted_iota(jnp.int32, (1, w), 1) // HG_KEY
    head_cols = [jnp.where(head_of_lane == h, 1.0, 0.0).astype(BF16) for h in range(HG_HEADS)]

    def chunk_tile(ti, worst):
        rows = pl.ds(pl.multiple_of(ti * rt, rt), rt)
        q = hg_ref[0, rows, 0:w] * (HG_KEY ** -0.5)
        v = hg_ref[0, rows, w:2 * w]
        vb = v.astype(BF16)
        od = jnp.zeros((rt, w), F32)
        for col, lb, cum, keep, qt_s, kt_s, dc_s in (
                (2, lbf, cum_lo, lo_c, qtf_s, ktf_s, dcf_s), (3, lbb, cum_hi, hi_c, qtb_s, ktb_s, dcb_s)):
            k, logf = gates(hg_ref[0, rows, col * w:(col + 1) * w], lb)
            l2 = logf * LOG2_E
            b = _dot_exact_lhs(cum, l2)
            tot = _dot_exact_lhs(ch_sum, l2)
            half = 0.5 * tot
            qs = (q * jnp.exp2(b - half)).astype(BF16)
            ks = (k * jnp.exp2(half - b)).astype(BF16)
            qt_s[rows, :] = (q * jnp.exp2(b)).astype(BF16)
            kt_s[rows, :] = (k * jnp.exp2(tot - b)).astype(BF16)
            dc_s[ti] = jnp.exp2(_dot_exact_lhs(ch_sel, l2)[0:8])
            worst = jnp.maximum(worst, jnp.max(-tot, axis=0, keepdims=True))
            for h in range(HG_HEADS):
                p = jnp.where(keep, _dot_nt(qs * head_cols[h], ks), 0.0)
                od = od + _dot(p.astype(BF16), vb * head_cols[h])
        vt_s[:, rows] = v.T.astype(BF16)
        oacc_s[rows, :] = od
        return worst

    worst = lax.fori_loop(0, n_tiles, chunk_tile, jnp.zeros((1, w), F32), unroll=2)
    exact_needed = jnp.max(worst) > HG_CHUNK_RANGE

    half_w = w // 2
    p_r = lax.broadcasted_iota(jnp.int32, (half_w, half_w), 0) // HG_KEY
    p_c = lax.broadcasted_iota(jnp.int32, (half_w, half_w), 1) // HG_KEY
    pair_mask = jnp.where(p_r == p_c, 1.0, 0.0).astype(F32)
    for s in (sfa_s, sfb_s, sba_s, sbb_s):
        s[...] = jnp.zeros((half_w, half_w), F32)
    lane_c = lax.broadcasted_iota(jnp.int32, (1, LANES), 1)

    def advance_chunk(ci, qt_s, kt_s, dc_s, sa_s, sb_s):
        r = pl.multiple_of(ci * ch, ch)
        rows = pl.ds(r, ch)
        sa = sa_s[...]
        sb = sb_s[...]
        oi = jnp.concatenate([_dot_nt(qt_s[rows, 0:half_w], sa.astype(BF16)),
                              _dot_nt(qt_s[rows, half_w:w], sb.astype(BF16))], axis=1)
        oacc_s[rows, :] = oacc_s[rows, :] + oi
        slab = pl.multiple_of((r // LANES) * LANES, LANES)
        off = r - slab
        vt = vt_s[:, pl.ds(slab, LANES)]
        vt = vt * jnp.where((lane_c >= off) & (lane_c < off + ch), 1.0, 0.0).astype(BF16)
        kt = kt_s[pl.ds(slab, LANES), :]
        dec = dc_s[ci // ch_per_tile, pl.ds(ci % ch_per_tile, 1), :]
        sa_s[...] = sa * dec[:, 0:half_w] + _dot(vt[0:half_w, :], kt[:, 0:half_w]) * pair_mask
        sb_s[...] = sb * dec[:, half_w:w] + _dot(vt[half_w:w, :], kt[:, half_w:w]) * pair_mask

    n_ch = t_len // ch
    n_cch = n_ctx // ch

    def chunk_step(i, carry):
        advance_chunk(i, qtf_s, ktf_s, dcf_s, sfa_s, sfb_s)
        jb = jnp.where(i < n_cch, n_cch - 1 - i, n_ch - 1 - (i - n_cch))
        advance_chunk(jb, qtb_s, ktb_s, dcb_s, sba_s, sbb_s)
        return carry

    lax.fori_loop(0, jnp.where(exact_needed, 0, n_ch), chunk_step, 0)

    n_tiles_x = jnp.where(exact_needed, n_tiles, 0)

    def tile_body(ti, carry):
        rows = pl.ds(pl.multiple_of(ti * rt, rt), rt)
        q = hg_ref[0, rows, 0:w] * (HG_KEY ** -0.5)
        v = hg_ref[0, rows, w:2 * w]
        kf, lf = gates(hg_ref[0, rows, 2 * w:3 * w], lbf)
        kb, lg = gates(hg_ref[0, rows, 3 * w:4 * w], lbb)
        b_f = _dot_exact_lhs(incl_lo, lf)
        tot_f = _dot_exact_lhs(blk_sum, lf)
        c_b = _dot_exact_lhs(incl_hi, lg)
        tot_b = _dot_exact_lhs(blk_sum, lg)
        qtf_s[rows, :] = (q * jnp.exp(b_f)).astype(BF16)
        ktf_s[rows, :] = (kf * jnp.exp(tot_f - b_f)).astype(BF16)
        qtb_s[rows, :] = (q * jnp.exp(c_b)).astype(BF16)
        ktb_s[rows, :] = (kb * jnp.exp(tot_b - c_b)).astype(BF16)
        drows = pl.ds(pl.multiple_of(ti * blk_per_tile, blk_per_tile), blk_per_tile)
        df_s[drows, :] = jnp.exp(_dot_exact_lhs(blk_sel, lf)[0:blk_per_tile])
        db_s[drows, :] = jnp.exp(_dot_exact_lhs(blk_sel, lg)[0:blk_per_tile])
        vt_s[:, rows] = v.T.astype(BF16)

        kf_s[pad:pad + rt, :] = kf
        bf_s[pad:pad + rt, :] = b_f
        kb_s[pad:pad + rt, :] = kb
        cb_s[pad:pad + rt, :] = c_b
        v_s[pad:pad + rt, :] = v
        od = jnp.zeros((rt, w), F32)
        for dl in range(blk):
            lo = pad - dl
            e = q * kf_s[lo:lo + rt, :] * jnp.exp(jnp.minimum(b_f - bf_s[lo:lo + rt, :], 0.0))
            e = jnp.where(pos >= dl, e, 0.0)
            od = od + _dot(e.astype(BF16), head_ones) * v_s[lo:lo + rt, :]
            hi = pad + dl
            e = q * kb_s[hi:hi + rt, :] * jnp.exp(jnp.minimum(c_b - cb_s[hi:hi + rt, :], 0.0))
            e = jnp.where(pos + dl < blk, e, 0.0)
            od = od + _dot(e.astype(BF16), head_ones) * v_s[hi:hi + rt, :]
        oacc_s[rows, :] = od
        return carry

    lax.fori_loop(0, n_tiles_x, tile_body, 0)

    stf_s[...] = jnp.zeros((w, w), F32)
    stb_s[...] = jnp.zeros((w, w), F32)
    n_blk = t_len // blk
    n_cblk = n_ctx // blk
    lane = lax.broadcasted_iota(jnp.int32, (1, LANES), 1)

    def advance(bi, qt_s, kt_s, d_s, st_s):
        r = pl.multiple_of(bi * blk, blk)
        st = st_s[...]
        rows = pl.ds(r, blk)
        oacc_s[rows, :] = oacc_s[rows, :] + _dot_nt(qt_s[rows, :], st.astype(BF16))
        slab = pl.multiple_of((r // LANES) * LANES, LANES)
        off = r - slab
        vt = vt_s[:, pl.ds(slab, LANES)]
        vt = vt * jnp.where((lane >= off) & (lane < off + blk), 1.0, 0.0).astype(BF16)
        upd = _dot(vt, kt_s[pl.ds(slab, LANES), :])
        st_s[...] = st * d_s[pl.ds(bi, 1), :] + upd * head_mask

    def step(i, carry):
        advance(i, qtf_s, ktf_s, df_s, stf_s)
        jb = jnp.where(i < n_cblk, n_cblk - 1 - i, n_blk - 1 - (i - n_cblk))
        advance(jb, qtb_s, ktb_s, db_s, stb_s)
        return carry

    lax.fori_loop(0, jnp.where(exact_needed, n_blk, 0), step, 0)

    gain = gain_ref[...]

    def out_body(ti, carry):
        rows = pl.ds(pl.multiple_of(ti * rt, rt), rt)
        o = oacc_s[rows, :]
        ms = _dot_exact_rhs(o * o, head_ones) * (1.0 / HG_KEY)
        y = o * lax.rsqrt(ms + 1e-6) * gain
        g = hg_ref[0, rows, 4 * w:5 * w]
        o_ref[0, rows, :] = (y * (g * _sigmoid(g))).astype(BF16)
        return carry

    lax.fori_loop(0, n_tiles, out_body, 0)


def _hgrn_call(hg, lbf, lbb, gain_t, n_ctx):
    b, t, _ = hg.shape
    w = HG_WIDTH
    tile_pad = pltpu.VMEM((HG_TILE + 2 * HG_BLOCK, w), F32)
    seq_bf16 = pltpu.VMEM((t, w), BF16)
    dec = pltpu.VMEM((t // HG_BLOCK, w), F32)
    return pl.pallas_call(
        functools.partial(_hgrn_kernel, t_len=t, n_ctx=n_ctx),
        grid=(b,),
        in_specs=[
            pl.BlockSpec((1, t, 5 * w), lambda bi: (bi, 0, 0)),
            pl.BlockSpec((1, w), lambda bi: (0, 0)),
            pl.BlockSpec((1, w), lambda bi: (0, 0)),
            pl.BlockSpec((1, w), lambda bi: (0, 0)),
        ],
        out_specs=pl.BlockSpec((1, t, w), lambda bi: (bi, 0, 0)),
        out_shape=jax.ShapeDtypeStruct((b, t, w), BF16),
        scratch_shapes=[tile_pad] * 5 + [seq_bf16] * 4 + [dec, dec,
                        pltpu.VMEM((w, t), BF16), pltpu.VMEM((t, w), F32),
                        pltpu.VMEM((w, w), F32), pltpu.VMEM((w, w), F32)]
        + [pltpu.VMEM((t // HG_TILE, 8, w), F32)] * 2 + [pltpu.VMEM((w // 2, w // 2), F32)] * 4,
        compiler_params=_cparams(("parallel",)),
        name="hgrn2",
    )(hg, lbf, lbb, gain_t)


def _mod_rows(ml_ref, mc_ref, tm, n_ctx, idx):
    i = pl.program_id(1)
    row = i * tm + lax.broadcasted_iota(jnp.int32, (tm, 1), 0)
    isctx = row < n_ctx
    ml = ml_ref[0]
    mc = mc_ref[0]
    return [jnp.where(isctx, mc[k:k + 1], ml[k:k + 1]) for k in idx]


def _out_kernel(oa_ref, ob_ref, og_ref, wo_ref, x_ref, ml_ref, mc_ref, gain_ref, *rest,
                tm, n_ctx, routed):
    if routed:
        router_ref, xo_ref, h_ref, gates_ref = rest
    else:
        xo_ref, h_ref = rest
    gate, shift, scale = _mod_rows(ml_ref, mc_ref, tm, n_ctx, (2, 3, 4))
    mix = (_dot(oa_ref[0], wo_ref[0:512, :]) + _dot(ob_ref[0], wo_ref[512:768, :])
           + _dot(og_ref[0], wo_ref[768:1024, :]))
    x = x_ref[0] + gate * mix
    xo_ref[0] = x
    y = x * lax.rsqrt(jnp.mean(x * x, axis=-1, keepdims=True) + 1e-6) * gain_ref[...]
    h = y * (1.0 + scale) + shift
    h_ref[0] = h.astype(h_ref.dtype)
    if routed:
        lane = lax.broadcasted_iota(jnp.int32, (1, LANES), 1)
        logits = _dot_f32(h, router_ref[...])
        logits = jnp.where(lane < N_EXPERTS, logits, -jnp.inf)
        m1 = jnp.max(logits, axis=-1, keepdims=True)
        i1 = jnp.min(jnp.where(logits == m1, lane, LANES), axis=-1, keepdims=True)
        rest_l = jnp.where(lane == i1, -jnp.inf, logits)
        m2 = jnp.max(rest_l, axis=-1, keepdims=True)
        i2 = jnp.min(jnp.where(rest_l == m2, lane, LANES), axis=-1, keepdims=True)
        e2 = jnp.exp(m2 - m1)
        w1 = 1.0 / (1.0 + e2)
        route = jnp.where(lane == 0, i1.astype(F32), 0.0) + jnp.where(lane == 1, i2.astype(F32), 0.0)
        gates_ref[0] = route + jnp.where(lane == 2, w1, 0.0) + jnp.where(lane == 3, e2 * w1, 0.0)


def _out_call(oa, ob, og, wo, xa, mods, gain, router_p, n_ctx):
    b, t, d = xa.shape
    tm = _row_tile(t)
    routed = router_p is not None
    row_blk = lambda width: pl.BlockSpec((1, tm, width), lambda bi, i: (bi, i, 0))
    in_specs = [
        row_blk(oa.shape[-1]), row_blk(ob.shape[-1]), row_blk(og.shape[-1]),
        _resident(wo.shape),
        row_blk(d),
        pl.BlockSpec((1, 6, d), lambda bi, i: (bi, 0, 0)),
        pl.BlockSpec((1, 6, d), lambda bi, i: (b, 0, 0)),
        _resident((1, d)),
    ]
    args = [oa, ob, og, wo, xa, mods, mods, gain]
    out_specs = [row_blk(d), row_blk(d)]
    out_shape = [jax.ShapeDtypeStruct((b, t, d), F32),
                 jax.ShapeDtypeStruct((b, t, d), F32 if routed else BF16)]
    if routed:
        in_specs.append(_resident(router_p.shape))
        args.append(router_p)
        out_specs.append(row_blk(LANES))
        out_shape.append(jax.ShapeDtypeStruct((b, t, LANES), F32))
    return pl.pallas_call(
        functools.partial(_out_kernel, tm=tm, n_ctx=n_ctx, routed=routed),
        grid=(b, t // tm),
        in_specs=in_specs,
        out_specs=out_specs,
        out_shape=out_shape,
        compiler_params=_cparams(("parallel", "parallel")),
        name="out_proj_routed" if routed else "out_proj",
    )(*args)


def _swiglu_into(acc_ref, h, wg_ref, wu_ref, wd_ref, row_gate, d_ff):
    for c in range(d_ff // FF_CHUNK):
        cs = slice(c * FF_CHUNK, (c + 1) * FF_CHUNK)
        g = _dot(h, wg_ref[0, :, cs])
        u = _dot(h, wu_ref[0, :, cs])
        a = g * _sigmoid(g) * u
        if row_gate is not None:
            a = a * row_gate
        acc_ref[...] += _dot(a.astype(BF16), wd_ref[0, cs, :])


def _ffn_kernel(h_ref, x_ref, ml_ref, mc_ref, wg_ref, wu_ref, wd_ref, xo_ref, acc_ref,
                *, tm, n_ctx, d_ff):
    (gate,) = _mod_rows(ml_ref, mc_ref, tm, n_ctx, (5,))
    acc_ref[...] = jnp.zeros_like(acc_ref)
    _swiglu_into(acc_ref, h_ref[0], wg_ref, wu_ref, wd_ref, None, d_ff)
    xo_ref[0] = x_ref[0] + gate * acc_ref[...]


def _ffn_call(h, xa, mods, wg, wu, wd, n_ctx):
    b, t, d = xa.shape
    d_ff = wg.shape[-1]
    tm = _row_tile(t)
    row_blk = pl.BlockSpec((1, tm, d), lambda bi, i: (bi, i, 0))
    return pl.pallas_call(
        functools.partial(_ffn_kernel, tm=tm, n_ctx=n_ctx, d_ff=d_ff),
        grid=(b, t // tm),
        in_specs=[
            row_blk, row_blk,
            pl.BlockSpec((1, 6, d), lambda bi, i: (bi, 0, 0)),
            pl.BlockSpec((1, 6, d), lambda bi, i: (b, 0, 0)),
            _resident((1, d, d_ff)), _resident((1, d, d_ff)), _resident((1, d_ff, d)),
        ],
        out_specs=row_blk,
        out_shape=jax.ShapeDtypeStruct((b, t, d), F32),
        scratch_shapes=[pltpu.VMEM((tm, d), F32)],
        compiler_params=_cparams(("parallel", "parallel")),
        name="ffn_dense",
    )(h, xa, mods, mods, wg[None], wu[None], wd[None])


def _row_copy(src, src_row, dst, dst_row, sem):
    return pltpu.make_async_copy(src.at[pl.ds(src_row, 1), :], dst.at[pl.ds(dst_row, 1), :], sem)


def _dispatch_kernel(pos_ref, pad_ref, h_ref, xs_ref, zero_s, sem, zsem, *, tm, n_exp, n_tiles):
    step = pl.program_id(0)
    base = step * tm

    def pad_copies(act):
        for e in range(n_exp):
            start = pad_ref[e]

            def tail_row(r, carry):
                act(_row_copy(zero_s, 0, xs_ref, start + r, zsem))
                return carry

            lax.fori_loop(0, pad_ref[n_exp + e], tail_row, 0)
        for j in range(n_exp):
            tile = pad_ref[2 * n_exp] + j

            @pl.when(tile < n_tiles)
            def _():
                row0 = pl.multiple_of(tile * tm, tm)
                act(pltpu.make_async_copy(zero_s, xs_ref.at[pl.ds(row0, tm), :], zsem))

    @pl.when(step == 0)
    def _():
        zero_s[...] = jnp.zeros_like(zero_s)
        pad_copies(lambda c: c.start())

    def issue(r, carry):
        for s in range(TOP_K):
            _row_copy(h_ref, r, xs_ref, pos_ref[TOP_K * (base + r) + s], sem).start()
        return carry

    lax.fori_loop(0, tm, issue, 0, unroll=8)

    def drain(r, carry):
        for s in range(TOP_K):
            _row_copy(h_ref, r, xs_ref, pos_ref[TOP_K * (base + r) + s], sem).wait()
        return carry

    lax.fori_loop(0, tm, drain, 0, unroll=8)

    @pl.when(step == 0)
    def _():
        pad_copies(lambda c: c.wait())


def _dispatch_call(pos, pad_info, h2d, n_tiles, n_exp):
    n, d = h2d.shape
    tm = MOE_TILE
    return pl.pallas_call(
        functools.partial(_dispatch_kernel, tm=tm, n_exp=n_exp, n_tiles=n_tiles),
        grid_spec=pltpu.PrefetchScalarGridSpec(
            num_scalar_prefetch=2,
            grid=(n // tm,),
            in_specs=[pl.BlockSpec((tm, d), lambda i, pos, pad: (i, 0))],
            out_specs=pl.BlockSpec(memory_space=pl.ANY),
            scratch_shapes=[pltpu.VMEM((tm, d), F32), pltpu.SemaphoreType.DMA(()),
                            pltpu.SemaphoreType.DMA(())],
        ),
        out_shape=jax.ShapeDtypeStruct((n_tiles * tm, d), F32),
        compiler_params=_cparams(("arbitrary",)),
        name="moe_dispatch",
    )(pos, pad_info, h2d)


def _expert_kernel(te_ref, nv_ref, xs_ref, wg_ref, wu_ref, wd_ref, ys_ref, acc_ref, *, tm, d_ff):
    n_valid = nv_ref[pl.program_id(0)]

    @pl.when(n_valid > 0)
    def _():
        acc_ref[...] = jnp.zeros_like(acc_ref)
        _swiglu_into(acc_ref, xs_ref[...].astype(BF16), wg_ref, wu_ref, wd_ref, None, d_ff)
        ys_ref[...] = acc_ref[...]

    @pl.when(n_valid == 0)
    def _():
        ys_ref[...] = jnp.zeros_like(ys_ref)


def _expert_call(tile_expert, tile_valid, xs, wg, wu, wd):
    n_slots, d = xs.shape
    d_ff = wg.shape[-1]
    tm = MOE_TILE
    w_blk = lambda shape: pl.BlockSpec(shape, lambda t, te, nv: (te[t], 0, 0))
    return pl.pallas_call(
        functools.partial(_expert_kernel, tm=tm, d_ff=d_ff),
        grid_spec=pltpu.PrefetchScalarGridSpec(
            num_scalar_prefetch=2,
            grid=(n_slots // tm,),
            in_specs=[
                pl.BlockSpec((tm, d), lambda t, te, nv: (t, 0)),
                w_blk((1, d, d_ff)), w_blk((1, d, d_ff)), w_blk((1, d_ff, d)),
            ],
            out_specs=pl.BlockSpec((tm, d), lambda t, te, nv: (t, 0)),
            scratch_shapes=[pltpu.VMEM((tm, d), F32)],
        ),
        out_shape=jax.ShapeDtypeStruct((n_slots, d), F32),
        compiler_params=_cparams(("arbitrary",)),
        name="moe_experts",
    )(tile_expert, tile_valid, xs, wg, wu, wd)


def _combine_kernel(pos_ref, ys_ref, x_ref, route_ref, ml_ref, mc_ref, xo_ref, y1_s, y2_s, sem,
                    *, tm, n_ctx):
    base = (pl.program_id(0) * pl.num_programs(1) + pl.program_id(1)) * tm
    bufs = (y1_s, y2_s)

    def issue(r, carry):
        for s in range(TOP_K):
            _row_copy(ys_ref, pos_ref[TOP_K * (base + r) + s], bufs[s], r, sem).start()
        return carry

    lax.fori_loop(0, tm, issue, 0, unroll=8)

    def drain(r, carry):
        for s in range(TOP_K):
            _row_copy(ys_ref, pos_ref[TOP_K * (base + r) + s], bufs[s], r, sem).wait()
        return carry

    lax.fori_loop(0, tm, drain, 0, unroll=8)

    lane = lax.broadcasted_iota(jnp.int32, (1, LANES), 1)
    route = route_ref[0]
    w1 = jnp.sum(jnp.where(lane == 2, route, 0.0), axis=-1, keepdims=True)
    w2 = jnp.sum(jnp.where(lane == 3, route, 0.0), axis=-1, keepdims=True)
    (gate,) = _mod_rows(ml_ref, mc_ref, tm, n_ctx, (5,))
    xo_ref[0] = x_ref[0] + gate * (w1 * y1_s[...] + w2 * y2_s[...])


def _combine_call(pos, ys, xa, route, mods, n_ctx):
    b, t, d = xa.shape
    tm = _row_tile(t)
    row_blk = lambda width: pl.BlockSpec((1, tm, width), lambda bi, i, pos: (bi, i, 0))
    return pl.pallas_call(
        functools.partial(_combine_kernel, tm=tm, n_ctx=n_ctx),
        grid_spec=pltpu.PrefetchScalarGridSpec(
            num_scalar_prefetch=1,
            grid=(b, t // tm),
            in_specs=[
                pl.BlockSpec(memory_space=pl.ANY),
                row_blk(d), row_blk(LANES),
                pl.BlockSpec((1, 6, d), lambda bi, i, pos: (bi, 0, 0)),
                pl.BlockSpec((1, 6, d), lambda bi, i, pos: (b, 0, 0)),
            ],
            out_specs=row_blk(d),
            scratch_shapes=[pltpu.VMEM((tm, d), F32), pltpu.VMEM((tm, d), F32),
                            pltpu.SemaphoreType.DMA(())],
        ),
        out_shape=jax.ShapeDtypeStruct((b, t, d), F32),
        compiler_params=_cparams(("arbitrary", "arbitrary")),
        name="moe_combine",
    )(pos, ys, xa, route, mods, mods)


def _moe_call(h, xa, route, mods, wg, wu, wd, n_ctx):
    b, t, d = xa.shape
    n = b * t
    tm = MOE_TILE
    n_exp = wg.shape[0]
    n_tiles = (TOP_K * n) // tm + n_exp
    choice = route.reshape(n, LANES)[:, :TOP_K].astype(jnp.int32).reshape(-1)
    onehot = (choice[:, None] == jnp.arange(n_exp, dtype=jnp.int32)[None, :]).astype(jnp.int32)
    csum = jnp.cumsum(onehot, axis=0)
    counts = csum[-1]
    rank = jnp.sum((csum - onehot) * onehot, axis=1)
    group_tiles = (counts + tm - 1) // tm
    tile_end = jnp.cumsum(group_tiles)
    group_row0 = (tile_end - group_tiles) * tm
    pos = jnp.sum(group_row0[None, :] * onehot, axis=1) + rank
    tile_id = jnp.arange(n_tiles, dtype=jnp.int32)
    tile_group = jnp.sum((tile_id[:, None] >= tile_end[None, :]).astype(jnp.int32), axis=1)
    last_group = jnp.max(jnp.where(counts > 0, jnp.arange(n_exp, dtype=jnp.int32), 0))
    tile_expert = jnp.minimum(tile_group, last_group)
    in_group = tile_group < n_exp
    g = jnp.minimum(tile_group, n_exp - 1)
    tile_valid = jnp.where(in_group, jnp.clip(counts[g] - (tile_id * tm - group_row0[g]), 0, tm), 0)

    pad_info = jnp.concatenate([group_row0 + counts, group_tiles * tm - counts, tile_end[-1:]])
    xs = _dispatch_call(pos, pad_info.astype(jnp.int32), h.reshape(n, d), n_tiles, n_exp)
    ys = _expert_call(tile_expert.astype(jnp.int32), tile_valid.astype(jnp.int32), xs, wg, wu, wd)
    return _combine_call(pos, ys, xa, route, mods, n_ctx)


def _final_kernel(x_ref, g_ref, o_ref):
    x = x_ref[0]
    o_ref[0] = x * lax.rsqrt(jnp.mean(x * x, axis=-1, keepdims=True) + 1e-6) * g_ref[...]


def _final_call(xa, gain, n_ctx):
    b, t, d = xa.shape
    tm = 256
    skip = n_ctx // tm
    return pl.pallas_call(
        _final_kernel,
        grid=(b, (t - n_ctx) // tm),
        in_specs=[pl.BlockSpec((1, tm, d), lambda bi, i: (bi, i + skip, 0)), _resident((1, d))],
        out_specs=pl.BlockSpec((1, tm, d), lambda bi, i: (bi, i, 0)),
        out_shape=jax.ShapeDtypeStruct((b, t - n_ctx, d), F32),
        compiler_params=_cparams(("parallel", "parallel")),
        name="final_norm",
    )(xa, gain)


def _rope_tables(seq, n_ctx):
    rows = seq // GRID_W
    row, col = jnp.meshgrid(jnp.arange(rows, dtype=F32), jnp.arange(GRID_W, dtype=F32), indexing="ij")
    row, col = row.reshape(-1), col.reshape(-1)

    def angles(rot_dim):
        n_freq = rot_dim // 4
        inv = ROPE_BASE ** (-jnp.arange(n_freq, dtype=F32) / n_freq)
        ang = jnp.concatenate([row[:, None] * inv, col[:, None] * inv], axis=-1)
        return jnp.cos(ang), jnp.sin(ang)

    def with_ctx(a, fill):
        return jnp.concatenate([jnp.full((n_ctx, a.shape[1]), fill, F32), a], axis=0)

    cos, sin = angles(DA_HEAD_DIM)
    cos_da = with_ctx(jnp.tile(cos, (1, 4)), 1.0)
    sin_da = with_ctx(jnp.tile(jnp.concatenate([-sin, sin], axis=1), (1, 2)), 0.0)
    cos, sin = angles(MLA_ROPE)
    ones = jnp.ones((seq, MLA_NOPE), F32)
    tail = jnp.ones((seq, LANES - MLA_NOPE - MLA_ROPE), F32)
    cos_m = with_ctx(jnp.concatenate([ones, cos, cos, tail], axis=1), 1.0)
    sin_m = with_ctx(jnp.concatenate([0 * ones, -sin, sin, 0 * tail], axis=1), 0.0)
    return cos_da, sin_da, cos_m, sin_m


def _pad_in_proj(w_in_l):
    d = w_in_l.shape[0]
    cut = 1536 + MLA_Q_RANK + MLA_KV_RANK + MLA_ROPE
    padw = 2048 - cut
    return jnp.concatenate([w_in_l[:, :cut], jnp.zeros((d, padw), w_in_l.dtype), w_in_l[:, cut:]],
                           axis=1).astype(BF16)


def _pad_uq(w_uq_l):
    qk = MLA_NOPE + MLA_ROPE
    w = w_uq_l.reshape(MLA_Q_RANK, MLA_HEADS, qk)
    w = jnp.pad(w, ((0, 0), (0, 0), (0, LANES - qk)))
    return w.reshape(MLA_Q_RANK, MLA_HEADS * LANES).astype(BF16)


def _pad_ukv(w_ukv_l):
    w = w_ukv_l.reshape(MLA_KV_RANK, MLA_HEADS, MLA_NOPE + MLA_V)
    k_part = jnp.pad(w[:, :, :MLA_NOPE], ((0, 0), (0, 0), (0, LANES - MLA_NOPE)))
    k_part = k_part.reshape(MLA_KV_RANK, MLA_HEADS * LANES)
    v_part = w[:, :, MLA_NOPE:].reshape(MLA_KV_RANK, MLA_HEADS * MLA_V)
    place = jnp.zeros((MLA_ROPE, MLA_HEADS, LANES), F32)
    eye = jnp.eye(MLA_ROPE, dtype=F32)
    place = place.at[:, :, MLA_NOPE:MLA_NOPE + MLA_ROPE].set(eye[:, None, :])
    place = place.reshape(MLA_ROPE, MLA_HEADS * LANES)
    top = jnp.concatenate([k_part, v_part], axis=1)
    mid = jnp.concatenate([place, jnp.zeros((MLA_ROPE, MLA_HEADS * MLA_V), F32)], axis=1)
    bot = jnp.zeros((MXU_COLS - MLA_KV_RANK - MLA_ROPE, top.shape[1]), F32)
    return jnp.concatenate([top, mid, bot], axis=0).astype(BF16)


def kernel(x, c, ctx, c_ctx, w_mod, b_mod, norm_mix, norm_ffn, norm_final, w_in, w_out, diff_lambda, diff_norm, mla_q_norm, mla_w_uq, mla_kv_norm, mla_w_ukv, hgrn_lb_logits, hgrn_norm, ffn_w_gate, ffn_w_up, ffn_w_down, moe_router, moe_w_gate, moe_w_up, moe_w_down):
    b, seq, d = x.shape
    n_ctx = ctx.shape[1]
    depth = w_mod.shape[0]
    assert d == 1024 and n_ctx % ATT_SUB == 0 and seq % ATT_SUB == 0

    xa = jnp.concatenate([ctx, x], axis=1)
    bp = -(-(b + 1) // 8) * 8
    c_all = jnp.concatenate([c, c_ctx[None, :], jnp.zeros((bp - b - 1, d), F32)], axis=0)
    mods = _mod_call(c_all, w_mod, b_mod).reshape(depth, bp, 6, d)

    lb = jnp.cumsum(jax.nn.softmax(hgrn_lb_logits.astype(F32), axis=1), axis=1)
    lb = lb - lb[:, :1]
    tabs = _rope_tables(seq, n_ctx)

    for l in range(depth):
        lam_init = 0.8 - 0.6 * math.exp(-0.3 * l)
        q_da, k_da, v_da, q_m, k_m, v_m, hg = _in_call(
            xa, norm_mix[l][None], mods[l], _pad_in_proj(w_in[l]), tabs,
            mla_q_norm[l][None], mla_kv_norm[l][None], _pad_uq(mla_w_uq[l]), _pad_ukv(mla_w_ukv[l]),
            n_ctx)
        oa = _da_call(q_da, k_da, v_da, diff_lambda[l], jnp.full((1, 1), lam_init, F32),
                      (diff_norm[l] * (1.0 - lam_init))[None], n_ctx)
        ob = _mla_call(q_m, k_m, v_m, n_ctx)
        og = _hgrn_call(hg, lb[0, l][None], lb[1, l][None], jnp.tile(hgrn_norm[l], HG_HEADS)[None], n_ctx)

        wo = w_out[l].astype(BF16)
        if l % 2 == 0:
            xa, h = _out_call(oa, ob, og, wo, xa, mods[l], norm_ffn[l][None], None, n_ctx)
            xa = _ffn_call(h, xa, mods[l], ffn_w_gate[l // 2].astype(BF16),
                           ffn_w_up[l // 2].astype(BF16), ffn_w_down[l // 2].astype(BF16), n_ctx)
        else:
            router_p = jnp.pad(moe_router[l // 2], ((0, 0), (0, LANES - N_EXPERTS)))
            xa, h, gates = _out_call(oa, ob, og, wo, xa, mods[l], norm_ffn[l][None], router_p, n_ctx)
            xa = _moe_call(h, xa, gates, mods[l], moe_w_gate[l // 2].astype(BF16),
                           moe_w_up[l // 2].astype(BF16), moe_w_down[l // 2].astype(BF16), n_ctx)
    return _final_call(xa, norm_final[None], n_ctx)
```

```python
import functools
import math

import jax
import jax.numpy as jnp
from jax import lax
from jax.experimental import pallas as pl
from jax.experimental.pallas import tpu as pltpu

F32 = jnp.float32
BF16 = jnp.bfloat16

GRID_W = 64
ROPE_BASE = 10000.0
DA_HEADS = 4
DA_HEAD_DIM = 64
MLA_HEADS = 4
MLA_Q_RANK = 256
MLA_KV_RANK = 128
MLA_NOPE = 64
MLA_ROPE = 32
MLA_V = 64
HG_HEADS = 4
HG_KEY = 64
HG_WIDTH = HG_HEADS * HG_KEY
N_EXPERTS = 8
TOP_K = 2
LOG2_E = 1.4426950408889634

LANES = 128
MXU_COLS = 256
VMEM_LIMIT = 56 * 1024 * 1024

HG_BLOCK = 16
HG_CHUNK = 64
HG_CHUNK_RANGE = 180.0
HG_TILE = 128
ATT_SUB = 256
IN_SPLIT = 2
FF_CHUNK = 256
MOE_TILE = 512


def _dot(a, b):
    return jnp.dot(a, b, preferred_element_type=F32)


def _dot_nt(a, b):
    return lax.dot_general(a, b, (((1,), (1,)), ((), ())), preferred_element_type=F32)


def _split3(x):
    x0 = x.astype(BF16)
    r = x - x0.astype(F32)
    x1 = r.astype(BF16)
    x2 = (r - x1.astype(F32)).astype(BF16)
    return x0, x1, x2


def _dot_exact_lhs(a_bf16, x):
    x0, x1, x2 = _split3(x)
    return _dot(a_bf16, x0) + _dot(a_bf16, x1) + _dot(a_bf16, x2)


def _dot_exact_rhs(x, b_bf16):
    x0, x1, x2 = _split3(x)
    return _dot(x0, b_bf16) + _dot(x1, b_bf16) + _dot(x2, b_bf16)


def _dot_f32(a, b):
    a0 = a.astype(BF16)
    a1 = (a - a0.astype(F32)).astype(BF16)
    b0 = b.astype(BF16)
    b1 = (b - b0.astype(F32)).astype(BF16)
    return _dot(a0, b0) + _dot(a0, b1) + _dot(a1, b0)


def _sigmoid(z):
    return 1.0 / (1.0 + jnp.exp(-z))


def _row_tile(t):
    return 768 if t % 768 == 0 else 256


def _cparams(sem):
    return pltpu.CompilerParams(dimension_semantics=sem, vmem_limit_bytes=VMEM_LIMIT)


def _resident(shape):
    nd = len(shape)
    return pl.BlockSpec(shape, lambda *_: (0,) * nd, pipeline_mode=pl.Buffered(1))


def _mod_kernel(c_ref, w_ref, b_ref, o_ref):
    c = c_ref[...]
    o_ref[0] = _dot_f32(c * _sigmoid(c), w_ref[0]) + b_ref[0]


def _mod_call(c_all, w_mod, b_mod):
    depth, d, n6 = w_mod.shape
    bp = c_all.shape[0]
    tn = 1536
    return pl.pallas_call(
        _mod_kernel,
        grid=(depth, n6 // tn),
        in_specs=[
            pl.BlockSpec((bp, d), lambda l, j: (0, 0)),
            pl.BlockSpec((1, d, tn), lambda l, j: (l, 0, j)),
            pl.BlockSpec((1, 1, tn), lambda l, j: (l, 0, j)),
        ],
        out_specs=pl.BlockSpec((1, bp, tn), lambda l, j: (l, 0, j)),
        out_shape=jax.ShapeDtypeStruct((depth, bp, n6), F32),
        compiler_params=_cparams(("parallel", "parallel")),
        name="modulation",
    )(c_all, w_mod, b_mod.reshape(depth, 1, n6))


def _rope128(seg, cos, sin, mla):
    lane = lax.broadcasted_iota(jnp.int32, (1, LANES), 1)
    if mla:
        half = MLA_ROPE // 2
        take_next = lane < MLA_NOPE + half
    else:
        half = DA_HEAD_DIM // 2
        take_next = (lane % DA_HEAD_DIM) < half
    sw = jnp.where(take_next, pltpu.roll(seg, LANES - half, 1), pltpu.roll(seg, half, 1))
    return seg * cos + sw * sin


def _in_kernel(x_ref, gain_ref, ml_ref, mc_ref, w_ref, cd_ref, sd_ref, cm_ref, sm_ref,
               qn_ref, kvn_ref, wuq_ref, wkv_ref,
               oq_ref, ok_ref, ov_ref, mq_ref, mk_ref, mv_ref, hg_ref, *, tm, n_ctx):
    i = pl.program_id(1)
    ml = ml_ref[0]
    mc = mc_ref[0]
    da_scale = DA_HEAD_DIM ** -0.5 * LOG2_E
    mla_scale = (MLA_NOPE + MLA_ROPE) ** -0.5 * LOG2_E
    th = tm // IN_SPLIT

    for part in range(IN_SPLIT):
        rows = slice(part * th, (part + 1) * th)
        x = x_ref[0, rows, :]
        y = x * lax.rsqrt(jnp.mean(x * x, axis=-1, keepdims=True) + 1e-6) * gain_ref[...]
        row = i * tm + part * th + lax.broadcasted_iota(jnp.int32, (th, 1), 0)
        isctx = row < n_ctx
        shift = jnp.where(isctx, mc[0:1], ml[0:1])
        scale = jnp.where(isctx, mc[1:2], ml[1:2])
        h = (y * (1.0 + scale) + shift).astype(BF16)

        cd, sd = cd_ref[rows, :], sd_ref[rows, :]
        cm, sm = cm_ref[rows, :], sm_ref[rows, :]

        aq = _dot(h, w_ref[:, 0:512])
        for g in range(4):
            sl = slice(g * LANES, (g + 1) * LANES)
            oq_ref[0, rows, sl] = (_rope128(aq[:, sl], cd, sd, False) * da_scale).astype(BF16)
        ak = _dot(h, w_ref[:, 512:1024])
        for g in range(4):
            sl = slice(g * LANES, (g + 1) * LANES)
            ok_ref[0, rows, sl] = _rope128(ak[:, sl], cd, sd, False).astype(BF16)
        ov_ref[0, rows, :] = _dot(h, w_ref[:, 1024:1536]).astype(BF16)

        cc = _dot(h, w_ref[:, 1536:2048])
        cq = cc[:, 0:MLA_Q_RANK]
        qn = (cq * lax.rsqrt(jnp.mean(cq * cq, axis=-1, keepdims=True) + 1e-6) * qn_ref[...]).astype(BF16)
        qu = _dot(qn, wuq_ref[...])
        for g in range(4):
            sl = slice(g * LANES, (g + 1) * LANES)
            mq_ref[0, rows, sl] = (_rope128(qu[:, sl], cm, sm, True) * mla_scale).astype(BF16)
        ckv = cc[:, MLA_Q_RANK:MLA_Q_RANK + MLA_KV_RANK]
        rest = cc[:, MLA_Q_RANK + MLA_KV_RANK:]
        kvn = ckv * lax.rsqrt(jnp.mean(ckv * ckv, axis=-1, keepdims=True) + 1e-6) * kvn_ref[...]
        t2 = jnp.concatenate([kvn, rest], axis=1).astype(BF16)
        kv = _dot(t2, wkv_ref[...])
        for g in range(4):
            sl = slice(g * LANES, (g + 1) * LANES)
            mk_ref[0, rows, sl] = _rope128(kv[:, sl], cm, sm, True).astype(BF16)
        mv_ref[0, rows, :] = kv[:, 512:768].astype(BF16)

        hg_ref[0, rows, :] = _dot(h, w_ref[:, 2048:3328])


def _in_call(xa, gain, mods, w_in_p, tabs, qn, kvn, wuq_p, wkv_p, n_ctx):
    b, t, d = xa.shape
    tm = _row_tile(t)
    cd, sd, cm, sm = tabs
    row_blk = lambda width: pl.BlockSpec((1, tm, width), lambda bi, i: (bi, i, 0))
    tab_blk = pl.BlockSpec((tm, LANES), lambda bi, i: (i, 0))
    out_w = (512, 512, 512, 512, 512, 256)
    return pl.pallas_call(
        functools.partial(_in_kernel, tm=tm, n_ctx=n_ctx),
        grid=(b, t // tm),
        in_specs=[
            row_blk(d),
            _resident((1, d)),
            pl.BlockSpec((1, 6, d), lambda bi, i: (bi, 0, 0)),
            pl.BlockSpec((1, 6, d), lambda bi, i: (b, 0, 0)),
            _resident(w_in_p.shape),
            tab_blk, tab_blk, tab_blk, tab_blk,
            _resident((1, MLA_Q_RANK)),
            _resident((1, MLA_KV_RANK)),
            _resident(wuq_p.shape),
            _resident(wkv_p.shape),
        ],
        out_specs=[row_blk(w) for w in out_w] + [row_blk(5 * HG_WIDTH)],
        out_shape=[jax.ShapeDtypeStruct((b, t, w), BF16) for w in out_w]
        + [jax.ShapeDtypeStruct((b, t, 5 * HG_WIDTH), F32)],
        compiler_params=_cparams(("parallel", "parallel")),
        name="in_proj",
    )(xa, gain, mods, mods, w_in_p, cd, sd, cm, sm, qn, kvn, wuq_p, wkv_p)


def _softmax_parts(s):
    m = jnp.max(s, axis=-1, keepdims=True)
    p = jnp.exp2(s - m)
    return p.astype(BF16), 1.0 / jnp.sum(p, axis=-1, keepdims=True)


def _attend_tile(attend, qi, q_ref, k_ref, v_ref, o_ref, tq, n_ctx):
    n_sub = tq // ATT_SUB
    ctx_sub = n_ctx // ATT_SUB
    for j in range(n_sub):
        rows = slice(j * ATT_SUB, (j + 1) * ATT_SUB)
        if j < ctx_sub:
            @pl.when(qi == 0)
            def _():
                o_ref[0, rows, :] = attend(q_ref[0, rows, :], k_ref[0, 0:n_ctx, :], v_ref[0, 0:n_ctx, :])

            @pl.when(qi != 0)
            def _():
                o_ref[0, rows, :] = attend(q_ref[0, rows, :], k_ref[0], v_ref[0])
        else:
            o_ref[0, rows, :] = attend(q_ref[0, rows, :], k_ref[0], v_ref[0])


def _da_kernel(q_ref, k_ref, v_ref, lamv_ref, lami_ref, gain_ref, o_ref, *, tq, n_ctx):
    qi = pl.program_id(2)
    lane = lax.broadcasted_iota(jnp.int32, (1, LANES), 1)
    map1 = jnp.where(lane < DA_HEAD_DIM, 1.0, 0.0).astype(BF16)
    map2 = jnp.where(lane < DA_HEAD_DIM, 0.0, 1.0).astype(BF16)
    lv = lamv_ref[...]
    lam = (jnp.exp(jnp.sum(lv[0:1] * lv[1:2], axis=-1, keepdims=True))
           - jnp.exp(jnp.sum(lv[2:3] * lv[3:4], axis=-1, keepdims=True)) + lami_ref[...])
    gain = gain_ref[...]

    def attend(q, k, v):
        p1, r1 = _softmax_parts(_dot_nt(q * map1, k))
        p2, r2 = _softmax_parts(_dot_nt(q * map2, k))
        o = _dot(p1, v) * r1 - _dot(p2, v) * (lam * r2)
        return (o * lax.rsqrt(jnp.mean(o * o, axis=-1, keepdims=True) + 1e-5) * gain).astype(BF16)

    _attend_tile(attend, qi, q_ref, k_ref, v_ref, o_ref, tq, n_ctx)


def _da_call(q, k, v, lam_vec, lam_init, gain_eff, n_ctx):
    b, t, _ = q.shape
    tq = _row_tile(t)
    return pl.pallas_call(
        functools.partial(_da_kernel, tq=tq, n_ctx=n_ctx),
        grid=(b, DA_HEADS, t // tq),
        in_specs=[
            pl.BlockSpec((1, tq, LANES), lambda bi, h, i: (bi, i, h)),
            pl.BlockSpec((1, t, LANES), lambda bi, h, i: (bi, 0, h)),
            pl.BlockSpec((1, t, LANES), lambda bi, h, i: (bi, 0, h)),
            pl.BlockSpec((4, DA_HEAD_DIM), lambda bi, h, i: (0, 0)),
            pl.BlockSpec((1, 1), lambda bi, h, i: (0, 0)),
            pl.BlockSpec((1, LANES), lambda bi, h, i: (0, 0)),
        ],
        out_specs=pl.BlockSpec((1, tq, LANES), lambda bi, h, i: (bi, i, h)),
        out_shape=jax.ShapeDtypeStruct((b, t, DA_HEADS * LANES), BF16),
        compiler_params=_cparams(("parallel", "parallel", "parallel")),
        name="diff_attention",
    )(q, k, v, lam_vec, lam_init, gain_eff)


def _mla_kernel(q_ref, k_ref, v_ref, o_ref, *, tq, n_ctx):
    qi = pl.program_id(1)
    lane = lax.broadcasted_iota(jnp.int32, (1, LANES), 1)
    low_half = lane < MLA_V

    def attend(q, k, v):
        pairs = []
        for pr in range(MLA_HEADS // 2):
            vp = v[:, pr * LANES:(pr + 1) * LANES]
            outs = []
            for h in (2 * pr, 2 * pr + 1):
                sl = slice(h * LANES, (h + 1) * LANES)
                p, r = _softmax_parts(_dot_nt(q[:, sl], k[:, sl]))
                outs.append(_dot(p, vp) * r)
            pairs.append(jnp.where(low_half, outs[0], outs[1]))
        return jnp.concatenate(pairs, axis=1).astype(BF16)

    _attend_tile(attend, qi, q_ref, k_ref, v_ref, o_ref, tq, n_ctx)


def _mla_call(q, k, v, n_ctx):
    b, t, _ = q.shape
    tq = _row_tile(t)
    wv = MLA_HEADS * MLA_V
    return pl.pallas_call(
        functools.partial(_mla_kernel, tq=tq, n_ctx=n_ctx),
        grid=(b, t // tq),
        in_specs=[
            pl.BlockSpec((1, tq, MLA_HEADS * LANES), lambda bi, i: (bi, i, 0)),
            pl.BlockSpec((1, t, MLA_HEADS * LANES), lambda bi, i: (bi, 0, 0)),
            pl.BlockSpec((1, t, wv), lambda bi, i: (bi, 0, 0)),
        ],
        out_specs=pl.BlockSpec((1, tq, wv), lambda bi, i: (bi, i, 0)),
        out_shape=jax.ShapeDtypeStruct((b, t, wv), BF16),
        compiler_params=_cparams(("parallel", "parallel")),
        name="latent_attention",
    )(q, k, v)


def _hgrn_kernel(hg_ref, lbf_ref, lbb_ref, gain_ref, o_ref,
                 kf_s, bf_s, kb_s, cb_s, v_s, qtf_s, ktf_s, qtb_s, ktb_s,
                 df_s, db_s, vt_s, oacc_s, stf_s, stb_s,
                 dcf_s, dcb_s, sfa_s, sfb_s, sba_s, sbb_s, *, t_len, n_ctx):
    w = HG_WIDTH
    rt = HG_TILE
    blk = HG_BLOCK
    pad = blk
    n_tiles = t_len // rt
    blk_per_tile = rt // blk

    r_i = lax.broadcasted_iota(jnp.int32, (rt, rt), 0)
    c_i = lax.broadcasted_iota(jnp.int32, (rt, rt), 1)
    same = (r_i // blk) == (c_i // blk)
    incl_lo = jnp.where(same & (c_i <= r_i), 1.0, 0.0).astype(BF16)
    incl_hi = jnp.where(same & (c_i >= r_i), 1.0, 0.0).astype(BF16)
    blk_sum = jnp.where(same, 1.0, 0.0).astype(BF16)
    sel_rows = 16
    g_r = lax.broadcasted_iota(jnp.int32, (sel_rows, rt), 0)
    g_c = lax.broadcasted_iota(jnp.int32, (sel_rows, rt), 1)
    blk_sel = jnp.where((g_c // blk) == g_r, 1.0, 0.0).astype(BF16)
    h_r = lax.broadcasted_iota(jnp.int32, (w, w), 0) // HG_KEY
    h_c = lax.broadcasted_iota(jnp.int32, (w, w), 1) // HG_KEY
    head_ones = jnp.where(h_r == h_c, 1.0, 0.0).astype(BF16)
    head_mask = jnp.where(h_r == h_c, 1.0, 0.0).astype(F32)
    pos = lax.broadcasted_iota(jnp.int32, (rt, 1), 0) % blk

    lbf = lbf_ref[...]
    lbb = lbb_ref[...]

    zpad = jnp.zeros((pad, w), F32)
    for s in (kf_s, bf_s, kb_s, cb_s, v_s):
        s[0:pad, :] = zpad
        s[pad + rt:pad + rt + pad, :] = zpad

    def gates(z, lb):
        f = lb + (1.0 - lb) * _sigmoid(z)
        return (1.0 - lb) * _sigmoid(-z), jnp.log(f)

    ch = HG_CHUNK
    ch_per_tile = rt // ch
    same_c = (r_i // ch) == (c_i // ch)
    lo_c = same_c & (c_i <= r_i)
    hi_c = same_c & (c_i >= r_i)
    cum_lo = jnp.where(lo_c, 1.0, 0.0).astype(BF16)
    cum_hi = jnp.where(hi_c, 1.0, 0.0).astype(BF16)
    ch_sum = jnp.where(same_c, 1.0, 0.0).astype(BF16)
    ch_sel = jnp.where((g_c // ch) == g_r, 1.0, 0.0).astype(BF16)
    head_of_lane = lax.broadcasted_iota(jnp.int32, (1, w), 1) // HG_KEY
    head_cols = [jnp.where(head_of_lane == h, 1.0, 0.0).astype(BF16) for h in range(HG_HEADS)]

    def chunk_tile(ti, worst):
        rows = pl.ds(pl.multiple_of(ti * rt, rt), rt)
        q = hg_ref[0, rows, 0:w] * (HG_KEY ** -0.5)
        v = hg_ref[0, rows, w:2 * w]
        vb = v.astype(BF16)
        od = jnp.zeros((rt, w), F32)
        for col, lb, cum, keep, qt_s, kt_s, dc_s in (
                (2, lbf, cum_lo, lo_c, qtf_s, ktf_s, dcf_s), (3, lbb, cum_hi, hi_c, qtb_s, ktb_s, dcb_s)):
            k, logf = gates(hg_ref[0, rows, col * w:(col + 1) * w], lb)
            l2 = logf * LOG2_E
            b = _dot_exact_lhs(cum, l2)
            tot = _dot_exact_lhs(ch_sum, l2)
            half = 0.5 * tot
            qs = (q * jnp.exp2(b - half)).astype(BF16)
            ks = (k * jnp.exp2(half - b)).astype(BF16)
            qt_s[rows, :] = (q * jnp.exp2(b)).astype(BF16)
            kt_s[rows, :] = (k * jnp.exp2(tot - b)).astype(BF16)
            dc_s[ti] = jnp.exp2(_dot_exact_lhs(ch_sel, l2)[0:8])
            worst = jnp.maximum(worst, jnp.max(-tot, axis=0, keepdims=True))
            for h in range(HG_HEADS):
                p = jnp.where(keep, _dot_nt(qs * head_cols[h], ks), 0.0)
                od = od + _dot(p.astype(BF16), vb * head_cols[h])
        vt_s[:, rows] = v.T.astype(BF16)
        oacc_s[rows, :] = od
        return worst

    worst = lax.fori_loop(0, n_tiles, chunk_tile, jnp.zeros((1, w), F32), unroll=2)
    exact_needed = jnp.max(worst) > HG_CHUNK_RANGE

    half_w = w // 2
    p_r = lax.broadcasted_iota(jnp.int32, (half_w, half_w), 0) // HG_KEY
    p_c = lax.broadcasted_iota(jnp.int32, (half_w, half_w), 1) // HG_KEY
    pair_mask = jnp.where(p_r == p_c, 1.0, 0.0).astype(F32)
    for s in (sfa_s, sfb_s, sba_s, sbb_s):
        s[...] = jnp.zeros((half_w, half_w), F32)
    lane_c = lax.broadcasted_iota(jnp.int32, (1, LANES), 1)

    def advance_chunk(ci, qt_s, kt_s, dc_s, sa_s, sb_s):
        r = pl.multiple_of(ci * ch, ch)
        rows = pl.ds(r, ch)
        sa = sa_s[...]
        sb = sb_s[...]
        oi = jnp.concatenate([_dot_nt(qt_s[rows, 0:half_w], sa.astype(BF16)),
                              _dot_nt(qt_s[rows, half_w:w], sb.astype(BF16))], axis=1)
        oacc_s[rows, :] = oacc_s[rows, :] + oi
        slab = pl.multiple_of((r // LANES) * LANES, LANES)
        off = r - slab
        vt = vt_s[:, pl.ds(slab, LANES)]
        vt = vt * jnp.where((lane_c >= off) & (lane_c < off + ch), 1.0, 0.0).astype(BF16)
        kt = kt_s[pl.ds(slab, LANES), :]
        dec = dc_s[ci // ch_per_tile, pl.ds(ci % ch_per_tile, 1), :]
        sa_s[...] = sa * dec[:, 0:half_w] + _dot(vt[0:half_w, :], kt[:, 0:half_w]) * pair_mask
        sb_s[...] = sb * dec[:, half_w:w] + _dot(vt[half_w:w, :], kt[:, half_w:w]) * pair_mask

    n_ch = t_len // ch
    n_cch = n_ctx // ch

    def chunk_step(i, carry):
        advance_chunk(i, qtf_s, ktf_s, dcf_s, sfa_s, sfb_s)
        jb = jnp.where(i < n_cch, n_cch - 1 - i, n_ch - 1 - (i - n_cch))
        advance_chunk(jb, qtb_s, ktb_s, dcb_s, sba_s, sbb_s)
        return carry

    lax.fori_loop(0, jnp.where(exact_needed, 0, n_ch), chunk_step, 0)

    n_tiles_x = jnp.where(exact_needed, n_tiles, 0)

    def tile_body(ti, carry):
        rows = pl.ds(pl.multiple_of(ti * rt, rt), rt)
        q = hg_ref[0, rows, 0:w] * (HG_KEY ** -0.5)
        v = hg_ref[0, rows, w:2 * w]
        kf, lf = gates(hg_ref[0, rows, 2 * w:3 * w], lbf)
        kb, lg = gates(hg_ref[0, rows, 3 * w:4 * w], lbb)
        b_f = _dot_exact_lhs(incl_lo, lf)
        tot_f = _dot_exact_lhs(blk_sum, lf)
        c_b = _dot_exact_lhs(incl_hi, lg)
        tot_b = _dot_exact_lhs(blk_sum, lg)
        qtf_s[rows, :] = (q * jnp.exp(b_f)).astype(BF16)
        ktf_s[rows, :] = (kf * jnp.exp(tot_f - b_f)).astype(BF16)
        qtb_s[rows, :] = (q * jnp.exp(c_b)).astype(BF16)
        ktb_s[rows, :] = (kb * jnp.exp(tot_b - c_b)).astype(BF16)
        drows = pl.ds(pl.multiple_of(ti * blk_per_tile, blk_per_tile), blk_per_tile)
        df_s[drows, :] = jnp.exp(_dot_exact_lhs(blk_sel, lf)[0:blk_per_tile])
        db_s[drows, :] = jnp.exp(_dot_exact_lhs(blk_sel, lg)[0:blk_per_tile])
        vt_s[:, rows] = v.T.astype(BF16)

        kf_s[pad:pad + rt, :] = kf
        bf_s[pad:pad + rt, :] = b_f
        kb_s[pad:pad + rt, :] = kb
        cb_s[pad:pad + rt, :] = c_b
        v_s[pad:pad + rt, :] = v
        od = jnp.zeros((rt, w), F32)
        for dl in range(blk):
            lo = pad - dl
            e = q * kf_s[lo:lo + rt, :] * jnp.exp(jnp.minimum(b_f - bf_s[lo:lo + rt, :], 0.0))
            e = jnp.where(pos >= dl, e, 0.0)
            od = od + _dot(e.astype(BF16), head_ones) * v_s[lo:lo + rt, :]
            hi = pad + dl
            e = q * kb_s[hi:hi + rt, :] * jnp.exp(jnp.minimum(c_b - cb_s[hi:hi + rt, :], 0.0))
            e = jnp.where(pos + dl < blk, e, 0.0)
            od = od + _dot(e.astype(BF16), head_ones) * v_s[hi:hi + rt, :]
        oacc_s[rows, :] = od
        return carry

    lax.fori_loop(0, n_tiles_x, tile_body, 0)

    stf_s[...] = jnp.zeros((w, w), F32)
    stb_s[...] = jnp.zeros((w, w), F32)
    n_blk = t_len // blk
    n_cblk = n_ctx // blk
    lane = lax.broadcasted_iota(jnp.int32, (1, LANES), 1)

    def advance(bi, qt_s, kt_s, d_s, st_s):
        r = pl.multiple_of(bi * blk, blk)
        st = st_s[...]
        rows = pl.ds(r, blk)
        oacc_s[rows, :] = oacc_s[rows, :] + _dot_nt(qt_s[rows, :], st.astype(BF16))
        slab = pl.multiple_of((r // LANES) * LANES, LANES)
        off = r - slab
        vt = vt_s[:, pl.ds(slab, LANES)]
        vt = vt * jnp.where((lane >= off) & (lane < off + blk), 1.0, 0.0).astype(BF16)
        upd = _dot(vt, kt_s[pl.ds(slab, LANES), :])
        st_s[...] = st * d_s[pl.ds(bi, 1), :] + upd * head_mask

    def step(i, carry):
        advance(i, qtf_s, ktf_s, df_s, stf_s)
        jb = jnp.where(i < n_cblk, n_cblk - 1 - i, n_blk - 1 - (i - n_cblk))
        advance(jb, qtb_s, ktb_s, db_s, stb_s)
        return carry

    lax.fori_loop(0, jnp.where(exact_needed, n_blk, 0), step, 0)

    gain = gain_ref[...]

    def out_body(ti, carry):
        rows = pl.ds(pl.multiple_of(ti * rt, rt), rt)
        o = oacc_s[rows, :]
        ms = _dot_exact_rhs(o * o, head_ones) * (1.0 / HG_KEY)
        y = o * lax.rsqrt(ms + 1e-6) * gain
        g = hg_ref[0, rows, 4 * w:5 * w]
        o_ref[0, rows, :] = (y * (g * _sigmoid(g))).astype(BF16)
        return carry

    lax.fori_loop(0, n_tiles, out_body, 0)


def _hgrn_call(hg, lbf, lbb, gain_t, n_ctx):
    b, t, _ = hg.shape
    w = HG_WIDTH
    tile_pad = pltpu.VMEM((HG_TILE + 2 * HG_BLOCK, w), F32)
    seq_bf16 = pltpu.VMEM((t, w), BF16)
    dec = pltpu.VMEM((t // HG_BLOCK, w), F32)
    return pl.pallas_call(
        functools.partial(_hgrn_kernel, t_len=t, n_ctx=n_ctx),
        grid=(b,),
        in_specs=[
            pl.BlockSpec((1, t, 5 * w), lambda bi: (bi, 0, 0)),
            pl.BlockSpec((1, w), lambda bi: (0, 0)),
            pl.BlockSpec((1, w), lambda bi: (0, 0)),
            pl.BlockSpec((1, w), lambda bi: (0, 0)),
        ],
        out_specs=pl.BlockSpec((1, t, w), lambda bi: (bi, 0, 0)),
        out_shape=jax.ShapeDtypeStruct((b, t, w), BF16),
        scratch_shapes=[tile_pad] * 5 + [seq_bf16] * 4 + [dec, dec,
                        pltpu.VMEM((w, t), BF16), pltpu.VMEM((t, w), F32),
                        pltpu.VMEM((w, w), F32), pltpu.VMEM((w, w), F32)]
        + [pltpu.VMEM((t // HG_TILE, 8, w), F32)] * 2 + [pltpu.VMEM((w // 2, w // 2), F32)] * 4,
        compiler_params=_cparams(("parallel",)),
        name="hgrn2",
    )(hg, lbf, lbb, gain_t)


def _mod_rows(ml_ref, mc_ref, tm, n_ctx, idx):
    i = pl.program_id(1)
    row = i * tm + lax.broadcasted_iota(jnp.int32, (tm, 1), 0)
    isctx = row < n_ctx
    ml = ml_ref[0]
    mc = mc_ref[0]
    return [jnp.where(isctx, mc[k:k + 1], ml[k:k + 1]) for k in idx]


def _out_kernel(oa_ref, ob_ref, og_ref, wo_ref, x_ref, ml_ref, mc_ref, gain_ref, *rest,
                tm, n_ctx, routed, d_ff):
    if routed:
        router_ref, xo_ref, h_ref, gates_ref = rest
    else:
        wg_ref, wu_ref, wd_ref, xo_ref, acc_ref = rest
    gate, shift, scale, ffn_gate = _mod_rows(ml_ref, mc_ref, tm, n_ctx, (2, 3, 4, 5))
    mix = (_dot(oa_ref[0], wo_ref[0:512, :]) + _dot(ob_ref[0], wo_ref[512:768, :])
           + _dot(og_ref[0], wo_ref[768:1024, :]))
    x = x_ref[0] + gate * mix
    y = x * lax.rsqrt(jnp.mean(x * x, axis=-1, keepdims=True) + 1e-6) * gain_ref[...]
    h = y * (1.0 + scale) + shift
    if not routed:
        acc_ref[...] = jnp.zeros_like(acc_ref)
        _swiglu_into(acc_ref, h.astype(BF16), wg_ref, wu_ref, wd_ref, None, d_ff)
        xo_ref[0] = x + ffn_gate * acc_ref[...]
    else:
        xo_ref[0] = x
        h_ref[0] = h
    if routed:
        lane = lax.broadcasted_iota(jnp.int32, (1, LANES), 1)
        logits = _dot_f32(h, router_ref[...])
        logits = jnp.where(lane < N_EXPERTS, logits, -jnp.inf)
        m1 = jnp.max(logits, axis=-1, keepdims=True)
        i1 = jnp.min(jnp.where(logits == m1, lane, LANES), axis=-1, keepdims=True)
        rest_l = jnp.where(lane == i1, -jnp.inf, logits)
        m2 = jnp.max(rest_l, axis=-1, keepdims=True)
        i2 = jnp.min(jnp.where(rest_l == m2, lane, LANES), axis=-1, keepdims=True)
        e2 = jnp.exp(m2 - m1)
        w1 = 1.0 / (1.0 + e2)
        route = jnp.where(lane == 0, i1.astype(F32), 0.0) + jnp.where(lane == 1, i2.astype(F32), 0.0)
        gates_ref[0] = route + jnp.where(lane == 2, w1, 0.0) + jnp.where(lane == 3, e2 * w1, 0.0)


def _out_call(oa, ob, og, wo, xa, mods, gain, n_ctx, router_p=None, ffn=None, skip_rows=0):
    b, t, d = xa.shape
    routed = router_p is not None
    tm = _row_tile(t) if skip_rows == 0 else ATT_SUB
    skip = skip_rows // tm
    t_out = t - skip_rows
    ctx_rows = n_ctx - skip_rows
    in_blk = lambda width: pl.BlockSpec((1, tm, width), lambda bi, i: (bi, i + skip, 0))
    out_blk = lambda width: pl.BlockSpec((1, tm, width), lambda bi, i: (bi, i, 0))
    in_specs = [
        in_blk(oa.shape[-1]), in_blk(ob.shape[-1]), in_blk(og.shape[-1]),
        _resident(wo.shape),
        in_blk(d),
        pl.BlockSpec((1, 6, d), lambda bi, i: (bi, 0, 0)),
        pl.BlockSpec((1, 6, d), lambda bi, i: (b, 0, 0)),
        _resident((1, d)),
    ]
    args = [oa, ob, og, wo, xa, mods, mods, gain]
    scratch = []
    d_ff = 0
    if routed:
        in_specs.append(_resident(router_p.shape))
        args.append(router_p)
        out_specs = [out_blk(d), out_blk(d), out_blk(LANES)]
        out_shape = [jax.ShapeDtypeStruct((b, t_out, d), F32), jax.ShapeDtypeStruct((b, t_out, d), F32),
                     jax.ShapeDtypeStruct((b, t_out, LANES), F32)]
    else:
        wg, wu, wd = ffn
        d_ff = wg.shape[-1]
        in_specs += [_resident((1, d, d_ff)), _resident((1, d, d_ff)), _resident((1, d_ff, d))]
        args += [wg[None], wu[None], wd[None]]
        out_specs = out_blk(d)
        out_shape = jax.ShapeDtypeStruct((b, t_out, d), F32)
        scratch = [pltpu.VMEM((tm, d), F32)]
    return pl.pallas_call(
        functools.partial(_out_kernel, tm=tm, n_ctx=ctx_rows, routed=routed, d_ff=d_ff),
        grid=(b, t_out // tm),
        in_specs=in_specs,
        out_specs=out_specs,
        out_shape=out_shape,
        scratch_shapes=scratch,
        compiler_params=_cparams(("parallel", "parallel")),
        name="out_proj_routed" if routed else "out_proj_ffn",
    )(*args)


def _swiglu_into(acc_ref, h, wg_ref, wu_ref, wd_ref, row_gate, d_ff):
    for c in range(d_ff // FF_CHUNK):
        cs = slice(c * FF_CHUNK, (c + 1) * FF_CHUNK)
        g = _dot(h, wg_ref[0, :, cs])
        u = _dot(h, wu_ref[0, :, cs])
        a = g * _sigmoid(g) * u
        if row_gate is not None:
            a = a * row_gate
        acc_ref[...] += _dot(a.astype(BF16), wd_ref[0, cs, :])


def _row_copy(src, src_row, dst, dst_row, sem):
    return pltpu.make_async_copy(src.at[pl.ds(src_row, 1), :], dst.at[pl.ds(dst_row, 1), :], sem)


def _dispatch_kernel(pos_ref, pad_ref, h_ref, xs_ref, zero_s, sem, zsem, *, tm, n_exp, n_tiles):
    step = pl.program_id(0)
    base = step * tm

    def pad_copies(act):
        for e in range(n_exp):
            start = pad_ref[e]

            def tail_row(r, carry):
                act(_row_copy(zero_s, 0, xs_ref, start + r, zsem))
                return carry

            lax.fori_loop(0, pad_ref[n_exp + e], tail_row, 0)
        for j in range(n_exp):
            tile = pad_ref[2 * n_exp] + j

            @pl.when(tile < n_tiles)
            def _():
                row0 = pl.multiple_of(tile * tm, tm)
                act(pltpu.make_async_copy(zero_s, xs_ref.at[pl.ds(row0, tm), :], zsem))

    @pl.when(step == 0)
    def _():
        zero_s[...] = jnp.zeros_like(zero_s)
        pad_copies(lambda c: c.start())

    def issue(r, carry):
        for s in range(TOP_K):
            _row_copy(h_ref, r, xs_ref, pos_ref[TOP_K * (base + r) + s], sem).start()
        return carry

    lax.fori_loop(0, tm, issue, 0, unroll=8)

    def drain(r, carry):
        for s in range(TOP_K):
            _row_copy(h_ref, r, xs_ref, pos_ref[TOP_K * (base + r) + s], sem).wait()
        return carry

    lax.fori_loop(0, tm, drain, 0, unroll=8)

    @pl.when(step == 0)
    def _():
        pad_copies(lambda c: c.wait())


def _dispatch_call(pos, pad_info, h2d, n_tiles, n_exp):
    n, d = h2d.shape
    tm = MOE_TILE
    return pl.pallas_call(
        functools.partial(_dispatch_kernel, tm=tm, n_exp=n_exp, n_tiles=n_tiles),
        grid_spec=pltpu.PrefetchScalarGridSpec(
            num_scalar_prefetch=2,
            grid=(n // tm,),
            in_specs=[pl.BlockSpec((tm, d), lambda i, pos, pad: (i, 0))],
            out_specs=pl.BlockSpec(memory_space=pl.ANY),
            scratch_shapes=[pltpu.VMEM((tm, d), F32), pltpu.SemaphoreType.DMA(()),
                            pltpu.SemaphoreType.DMA(())],
        ),
        out_shape=jax.ShapeDtypeStruct((n_tiles * tm, d), F32),
        compiler_params=_cparams(("arbitrary",)),
        name="moe_dispatch",
    )(pos, pad_info, h2d)


def _expert_kernel(te_ref, nv_ref, xs_ref, wg_ref, wu_ref, wd_ref, ys_ref, acc_ref, *, tm, d_ff):
    n_valid = nv_ref[pl.program_id(0)]

    @pl.when(n_valid > 0)
    def _():
        acc_ref[...] = jnp.zeros_like(acc_ref)
        _swiglu_into(acc_ref, xs_ref[...].astype(BF16), wg_ref, wu_ref, wd_ref, None, d_ff)
        ys_ref[...] = acc_ref[...]

    @pl.when(n_valid == 0)
    def _():
        ys_ref[...] = jnp.zeros_like(ys_ref)


def _expert_call(tile_expert, tile_valid, xs, wg, wu, wd):
    n_slots, d = xs.shape
    d_ff = wg.shape[-1]
    tm = MOE_TILE
    w_blk = lambda shape: pl.BlockSpec(shape, lambda t, te, nv: (te[t], 0, 0))
    return pl.pallas_call(
        functools.partial(_expert_kernel, tm=tm, d_ff=d_ff),
        grid_spec=pltpu.PrefetchScalarGridSpec(
            num_scalar_prefetch=2,
            grid=(n_slots // tm,),
            in_specs=[
                pl.BlockSpec((tm, d), lambda t, te, nv: (t, 0)),
                w_blk((1, d, d_ff)), w_blk((1, d, d_ff)), w_blk((1, d_ff, d)),
            ],
            out_specs=pl.BlockSpec((tm, d), lambda t, te, nv: (t, 0)),
            scratch_shapes=[pltpu.VMEM((tm, d), F32)],
        ),
        out_shape=jax.ShapeDtypeStruct((n_slots, d), F32),
        compiler_params=_cparams(("arbitrary",)),
        name="moe_experts",
    )(tile_expert, tile_valid, xs, wg, wu, wd)


def _combine_kernel(pos_ref, ys_ref, x_ref, route_ref, ml_ref, mc_ref, xo_ref, y1_s, y2_s, sem,
                    *, tm, n_ctx):
    base = (pl.program_id(0) * pl.num_programs(1) + pl.program_id(1)) * tm
    bufs = (y1_s, y2_s)

    def issue(r, carry):
        for s in range(TOP_K):
            _row_copy(ys_ref, pos_ref[TOP_K * (base + r) + s], bufs[s], r, sem).start()
        return carry

    lax.fori_loop(0, tm, issue, 0, unroll=8)

    def drain(r, carry):
        for s in range(TOP_K):
            _row_copy(ys_ref, pos_ref[TOP_K * (base + r) + s], bufs[s], r, sem).wait()
        return carry

    lax.fori_loop(0, tm, drain, 0, unroll=8)

    lane = lax.broadcasted_iota(jnp.int32, (1, LANES), 1)
    route = route_ref[0]
    w1 = jnp.sum(jnp.where(lane == 2, route, 0.0), axis=-1, keepdims=True)
    w2 = jnp.sum(jnp.where(lane == 3, route, 0.0), axis=-1, keepdims=True)
    (gate,) = _mod_rows(ml_ref, mc_ref, tm, n_ctx, (5,))
    xo_ref[0] = x_ref[0] + gate * (w1 * y1_s[...] + w2 * y2_s[...])


def _combine_call(pos, ys, xa, route, mods, n_ctx):
    b, t, d = xa.shape
    tm = _row_tile(t)
    row_blk = lambda width: pl.BlockSpec((1, tm, width), lambda bi, i, pos: (bi, i, 0))
    return pl.pallas_call(
        functools.partial(_combine_kernel, tm=tm, n_ctx=n_ctx),
        grid_spec=pltpu.PrefetchScalarGridSpec(
            num_scalar_prefetch=1,
            grid=(b, t // tm),
            in_specs=[
                pl.BlockSpec(memory_space=pl.ANY),
                row_blk(d), row_blk(LANES),
                pl.BlockSpec((1, 6, d), lambda bi, i, pos: (bi, 0, 0)),
                pl.BlockSpec((1, 6, d), lambda bi, i, pos: (b, 0, 0)),
            ],
            out_specs=row_blk(d),
            scratch_shapes=[pltpu.VMEM((tm, d), F32), pltpu.VMEM((tm, d), F32),
                            pltpu.SemaphoreType.DMA(())],
        ),
        out_shape=jax.ShapeDtypeStruct((b, t, d), F32),
        compiler_params=_cparams(("arbitrary", "arbitrary")),
        name="moe_combine",
    )(pos, ys, xa, route, mods, mods)


def _moe_call(h, xa, route, mods, wg, wu, wd, n_ctx):
    b, t, d = xa.shape
    n = b * t
    tm = MOE_TILE
    n_exp = wg.shape[0]
    n_tiles = (TOP_K * n) // tm + n_exp
    choice = route.reshape(n, LANES)[:, :TOP_K].astype(jnp.int32).reshape(-1)
    onehot = (choice[:, None] == jnp.arange(n_exp, dtype=jnp.int32)[None, :]).astype(jnp.int32)
    csum = jnp.cumsum(onehot, axis=0)
    counts = csum[-1]
    rank = jnp.sum((csum - onehot) * onehot, axis=1)
    group_tiles = (counts + tm - 1) // tm
    tile_end = jnp.cumsum(group_tiles)
    group_row0 = (tile_end - group_tiles) * tm
    pos = jnp.sum(group_row0[None, :] * onehot, axis=1) + rank
    tile_id = jnp.arange(n_tiles, dtype=jnp.int32)
    tile_group = jnp.sum((tile_id[:, None] >= tile_end[None, :]).astype(jnp.int32), axis=1)
    last_group = jnp.max(jnp.where(counts > 0, jnp.arange(n_exp, dtype=jnp.int32), 0))
    tile_expert = jnp.minimum(tile_group, last_group)
    in_group = tile_group < n_exp
    g = jnp.minimum(tile_group, n_exp - 1)
    tile_valid = jnp.where(in_group, jnp.clip(counts[g] - (tile_id * tm - group_row0[g]), 0, tm), 0)

    pad_info = jnp.concatenate([group_row0 + counts, group_tiles * tm - counts, tile_end[-1:]])
    xs = _dispatch_call(pos, pad_info.astype(jnp.int32), h.reshape(n, d), n_tiles, n_exp)
    ys = _expert_call(tile_expert.astype(jnp.int32), tile_valid.astype(jnp.int32), xs, wg, wu, wd)
    return _combine_call(pos, ys, xa, route, mods, n_ctx)


def _final_kernel(x_ref, g_ref, o_ref):
    x = x_ref[0]
    o_ref[0] = x * lax.rsqrt(jnp.mean(x * x, axis=-1, keepdims=True) + 1e-6) * g_ref[...]


def _final_call(xa, gain, n_ctx):
    b, t, d = xa.shape
    tm = 256
    skip = n_ctx // tm
    return pl.pallas_call(
        _final_kernel,
        grid=(b, (t - n_ctx) // tm),
        in_specs=[pl.BlockSpec((1, tm, d), lambda bi, i: (bi, i + skip, 0)), _resident((1, d))],
        out_specs=pl.BlockSpec((1, tm, d), lambda bi, i: (bi, i, 0)),
        out_shape=jax.ShapeDtypeStruct((b, t - n_ctx, d), F32),
        compiler_params=_cparams(("parallel", "parallel")),
        name="final_norm",
    )(xa, gain)


def _rope_tables(seq, n_ctx):
    rows = seq // GRID_W
    row, col = jnp.meshgrid(jnp.arange(rows, dtype=F32), jnp.arange(GRID_W, dtype=F32), indexing="ij")
    row, col = row.reshape(-1), col.reshape(-1)

    def angles(rot_dim):
        n_freq = rot_dim // 4
        inv = ROPE_BASE ** (-jnp.arange(n_freq, dtype=F32) / n_freq)
        ang = jnp.concatenate([row[:, None] * inv, col[:, None] * inv], axis=-1)
        return jnp.cos(ang), jnp.sin(ang)

    def with_ctx(a, fill):
        return jnp.concatenate([jnp.full((n_ctx, a.shape[1]), fill, F32), a], axis=0)

    cos, sin = angles(DA_HEAD_DIM)
    cos_da = with_ctx(jnp.tile(cos, (1, 4)), 1.0)
    sin_da = with_ctx(jnp.tile(jnp.concatenate([-sin, sin], axis=1), (1, 2)), 0.0)
    cos, sin = angles(MLA_ROPE)
    ones = jnp.ones((seq, MLA_NOPE), F32)
    tail = jnp.ones((seq, LANES - MLA_NOPE - MLA_ROPE), F32)
    cos_m = with_ctx(jnp.concatenate([ones, cos, cos, tail], axis=1), 1.0)
    sin_m = with_ctx(jnp.concatenate([0 * ones, -sin, sin, 0 * tail], axis=1), 0.0)
    return cos_da, sin_da, cos_m, sin_m


def _pad_in_proj(w_in_l):
    d = w_in_l.shape[0]
    cut = 1536 + MLA_Q_RANK + MLA_KV_RANK + MLA_ROPE
    padw = 2048 - cut
    return jnp.concatenate([w_in_l[:, :cut], jnp.zeros((d, padw), w_in_l.dtype), w_in_l[:, cut:]],
                           axis=1).astype(BF16)


def _pad_uq(w_uq_l):
    qk = MLA_NOPE + MLA_ROPE
    w = w_uq_l.reshape(MLA_Q_RANK, MLA_HEADS, qk)
    w = jnp.pad(w, ((0, 0), (0, 0), (0, LANES - qk)))
    return w.reshape(MLA_Q_RANK, MLA_HEADS * LANES).astype(BF16)


def _pad_ukv(w_ukv_l):
    w = w_ukv_l.reshape(MLA_KV_RANK, MLA_HEADS, MLA_NOPE + MLA_V)
    k_part = jnp.pad(w[:, :, :MLA_NOPE], ((0, 0), (0, 0), (0, LANES - MLA_NOPE)))
    k_part = k_part.reshape(MLA_KV_RANK, MLA_HEADS * LANES)
    v_part = w[:, :, MLA_NOPE:].reshape(MLA_KV_RANK, MLA_HEADS * MLA_V)
    place = jnp.zeros((MLA_ROPE, MLA_HEADS, LANES), F32)
    eye = jnp.eye(MLA_ROPE, dtype=F32)
    place = place.at[:, :, MLA_NOPE:MLA_NOPE + MLA_ROPE].set(eye[:, None, :])
    place = place.reshape(MLA_ROPE, MLA_HEADS * LANES)
    top = jnp.concatenate([k_part, v_part], axis=1)
    mid = jnp.concatenate([place, jnp.zeros((MLA_ROPE, MLA_HEADS * MLA_V), F32)], axis=1)
    bot = jnp.zeros((MXU_COLS - MLA_KV_RANK - MLA_ROPE, top.shape[1]), F32)
    return jnp.concatenate([top, mid, bot], axis=0).astype(BF16)


def kernel(x, c, ctx, c_ctx, w_mod, b_mod, norm_mix, norm_ffn, norm_final, w_in, w_out, diff_lambda, diff_norm, mla_q_norm, mla_w_uq, mla_kv_norm, mla_w_ukv, hgrn_lb_logits, hgrn_norm, ffn_w_gate, ffn_w_up, ffn_w_down, moe_router, moe_w_gate, moe_w_up, moe_w_down):
    b, seq, d = x.shape
    n_ctx = ctx.shape[1]
    depth = w_mod.shape[0]
    assert d == 1024 and n_ctx % ATT_SUB == 0 and seq % ATT_SUB == 0

    xa = jnp.concatenate([ctx, x], axis=1)
    bp = -(-(b + 1) // 8) * 8
    c_all = jnp.concatenate([c, c_ctx[None, :], jnp.zeros((bp - b - 1, d), F32)], axis=0)
    mods = _mod_call(c_all, w_mod, b_mod).reshape(depth, bp, 6, d)

    lb = jnp.cumsum(jax.nn.softmax(hgrn_lb_logits.astype(F32), axis=1), axis=1)
    lb = lb - lb[:, :1]
    tabs = _rope_tables(seq, n_ctx)

    for l in range(depth):
        lam_init = 0.8 - 0.6 * math.exp(-0.3 * l)
        q_da, k_da, v_da, q_m, k_m, v_m, hg = _in_call(
            xa, norm_mix[l][None], mods[l], _pad_in_proj(w_in[l]), tabs,
            mla_q_norm[l][None], mla_kv_norm[l][None], _pad_uq(mla_w_uq[l]), _pad_ukv(mla_w_ukv[l]),
            n_ctx)
        oa = _da_call(q_da, k_da, v_da, diff_lambda[l], jnp.full((1, 1), lam_init, F32),
                      (diff_norm[l] * (1.0 - lam_init))[None], n_ctx)
        ob = _mla_call(q_m, k_m, v_m, n_ctx)
        og = _hgrn_call(hg, lb[0, l][None], lb[1, l][None], jnp.tile(hgrn_norm[l], HG_HEADS)[None], n_ctx)

        wo = w_out[l].astype(BF16)
        skip = n_ctx if l == depth - 1 else 0
        if l % 2 == 0:
            ffn = (ffn_w_gate[l // 2].astype(BF16), ffn_w_up[l // 2].astype(BF16),
                   ffn_w_down[l // 2].astype(BF16))
            xa = _out_call(oa, ob, og, wo, xa, mods[l], norm_ffn[l][None], n_ctx, ffn=ffn, skip_rows=skip)
        else:
            router_p = jnp.pad(moe_router[l // 2], ((0, 0), (0, LANES - N_EXPERTS)))
            xa, h, route = _out_call(oa, ob, og, wo, xa, mods[l], norm_ffn[l][None], n_ctx,
                                     router_p=router_p, skip_rows=skip)
            xa = _moe_call(h, xa, route, mods[l], moe_w_gate[l // 2].astype(BF16),
                           moe_w_up[l // 2].astype(BF16), moe_w_down[l // 2].astype(BF16), n_ctx - skip)
        n_ctx_out = n_ctx - skip
    return _final_call(xa, norm_final[None], n_ctx_out)
```

```python
import functools
import math

import jax
import jax.numpy as jnp
from jax import lax
from jax.experimental import pallas as pl
from jax.experimental.pallas import tpu as pltpu

F32 = jnp.float32
BF16 = jnp.bfloat16

GRID_W = 64
ROPE_BASE = 10000.0
DA_HEADS = 4
DA_HEAD_DIM = 64
MLA_HEADS = 4
MLA_Q_RANK = 256
MLA_KV_RANK = 128
MLA_NOPE = 64
MLA_ROPE = 32
MLA_V = 64
HG_HEADS = 4
HG_KEY = 64
HG_WIDTH = HG_HEADS * HG_KEY
N_EXPERTS = 8
TOP_K = 2
LOG2_E = 1.4426950408889634

LANES = 128
MXU_COLS = 256
VMEM_LIMIT = 56 * 1024 * 1024

HG_BLOCK = 16
HG_CHUNK = 64
HG_CHUNK_RANGE = 180.0
HG_TILE = 128
ATT_SUB = 256
IN_SPLIT = 1
FF_CHUNK = 256
MOE_TILE = 512


def _dot(a, b):
    return jnp.dot(a, b, preferred_element_type=F32)


def _dot_nt(a, b):
    return lax.dot_general(a, b, (((1,), (1,)), ((), ())), preferred_element_type=F32)


def _split3(x):
    x0 = x.astype(BF16)
    r = x - x0.astype(F32)
    x1 = r.astype(BF16)
    x2 = (r - x1.astype(F32)).astype(BF16)
    return x0, x1, x2


def _dot_exact_lhs(a_bf16, x):
    x0, x1, x2 = _split3(x)
    return _dot(a_bf16, x0) + _dot(a_bf16, x1) + _dot(a_bf16, x2)


def _dot_exact_rhs(x, b_bf16):
    x0, x1, x2 = _split3(x)
    return _dot(x0, b_bf16) + _dot(x1, b_bf16) + _dot(x2, b_bf16)


def _dot_f32(a, b):
    a0 = a.astype(BF16)
    a1 = (a - a0.astype(F32)).astype(BF16)
    b0 = b.astype(BF16)
    b1 = (b - b0.astype(F32)).astype(BF16)
    return _dot(a0, b0) + _dot(a0, b1) + _dot(a1, b0)


def _sigmoid(z):
    return 1.0 / (1.0 + jnp.exp(-z))


def _row_tile(t):
    return next(tm for tm in (768, 512, 256) if t % tm == 0)


def _cparams(sem):
    return pltpu.CompilerParams(dimension_semantics=sem, vmem_limit_bytes=VMEM_LIMIT)


def _resident(shape):
    nd = len(shape)
    return pl.BlockSpec(shape, lambda *_: (0,) * nd, pipeline_mode=pl.Buffered(1))


def _mod_kernel(c_ref, w_ref, b_ref, o_ref):
    c = c_ref[...]
    o_ref[0] = _dot_f32(c * _sigmoid(c), w_ref[0]) + b_ref[0]


def _mod_call(c_all, w_mod, b_mod):
    depth, d, n6 = w_mod.shape
    bp = c_all.shape[0]
    tn = 1536
    return pl.pallas_call(
        _mod_kernel,
        grid=(depth, n6 // tn),
        in_specs=[
            pl.BlockSpec((bp, d), lambda l, j: (0, 0)),
            pl.BlockSpec((1, d, tn), lambda l, j: (l, 0, j)),
            pl.BlockSpec((1, 1, tn), lambda l, j: (l, 0, j)),
        ],
        out_specs=pl.BlockSpec((1, bp, tn), lambda l, j: (l, 0, j)),
        out_shape=jax.ShapeDtypeStruct((depth, bp, n6), F32),
        compiler_params=_cparams(("parallel", "parallel")),
        name="modulation",
    )(c_all, w_mod, b_mod.reshape(depth, 1, n6))


def _rope128(seg, cos, sin, mla):
    lane = lax.broadcasted_iota(jnp.int32, (1, LANES), 1)
    if mla:
        half = MLA_ROPE // 2
        take_next = lane < MLA_NOPE + half
    else:
        half = DA_HEAD_DIM // 2
        take_next = (lane % DA_HEAD_DIM) < half
    sw = jnp.where(take_next, pltpu.roll(seg, LANES - half, 1), pltpu.roll(seg, half, 1))
    return seg * cos + sw * sin


def _in_kernel(x_ref, gain_ref, ml_ref, mc_ref, w_ref, cd_ref, sd_ref, cm_ref, sm_ref,
               qn_ref, kvn_ref, wuq_ref, wkv_ref,
               oq_ref, ok_ref, ov_ref, mq_ref, mk_ref, mv_ref, hg_ref, *, tm, n_ctx):
    i = pl.program_id(1)
    ml = ml_ref[0]
    mc = mc_ref[0]
    da_scale = DA_HEAD_DIM ** -0.5 * LOG2_E
    mla_scale = (MLA_NOPE + MLA_ROPE) ** -0.5 * LOG2_E
    th = tm // IN_SPLIT

    for part in range(IN_SPLIT):
        rows = slice(part * th, (part + 1) * th)
        x = x_ref[0, rows, :]
        y = x * lax.rsqrt(jnp.mean(x * x, axis=-1, keepdims=True) + 1e-6) * gain_ref[...]
        row = i * tm + part * th + lax.broadcasted_iota(jnp.int32, (th, 1), 0)
        isctx = row < n_ctx
        shift = jnp.where(isctx, mc[0:1], ml[0:1])
        scale = jnp.where(isctx, mc[1:2], ml[1:2])
        h = (y * (1.0 + scale) + shift).astype(BF16)

        cd, sd = cd_ref[rows, :], sd_ref[rows, :]
        cm, sm = cm_ref[rows, :], sm_ref[rows, :]

        aq = _dot(h, w_ref[:, 0:512])
        for g in range(4):
            sl = slice(g * LANES, (g + 1) * LANES)
            oq_ref[0, rows, sl] = (_rope128(aq[:, sl], cd, sd, False) * da_scale).astype(BF16)
        ak = _dot(h, w_ref[:, 512:1024])
        for g in range(4):
            sl = slice(g * LANES, (g + 1) * LANES)
            ok_ref[0, rows, sl] = _rope128(ak[:, sl], cd, sd, False).astype(BF16)
        ov_ref[0, rows, :] = _dot(h, w_ref[:, 1024:1536]).astype(BF16)

        cc = _dot(h, w_ref[:, 1536:2048])
        cq = cc[:, 0:MLA_Q_RANK]
        qn = (cq * lax.rsqrt(jnp.mean(cq * cq, axis=-1, keepdims=True) + 1e-6) * qn_ref[...]).astype(BF16)
        qu = _dot(qn, wuq_ref[...])
        for g in range(4):
            sl = slice(g * LANES, (g + 1) * LANES)
            mq_ref[0, rows, sl] = (_rope128(qu[:, sl], cm, sm, True) * mla_scale).astype(BF16)
        ckv = cc[:, MLA_Q_RANK:MLA_Q_RANK + MLA_KV_RANK]
        rest = cc[:, MLA_Q_RANK + MLA_KV_RANK:]
        kvn = ckv * lax.rsqrt(jnp.mean(ckv * ckv, axis=-1, keepdims=True) + 1e-6) * kvn_ref[...]
        t2 = jnp.concatenate([kvn, rest], axis=1).astype(BF16)
        kv = _dot(t2, wkv_ref[...])
        for g in range(4):
            sl = slice(g * LANES, (g + 1) * LANES)
            mk_ref[0, rows, sl] = _rope128(kv[:, sl], cm, sm, True).astype(BF16)
        mv_ref[0, rows, :] = kv[:, 512:768].astype(BF16)

        hg_ref[0, rows, :] = _dot(h, w_ref[:, 2048:3328])


def _in_call(xa, gain, mods, w_in_p, tabs, qn, kvn, wuq_p, wkv_p, n_ctx):
    b, t, d = xa.shape
    tm = _row_tile(t)
    cd, sd, cm, sm = tabs
    row_blk = lambda width: pl.BlockSpec((1, tm, width), lambda bi, i: (bi, i, 0))
    tab_blk = pl.BlockSpec((tm, LANES), lambda bi, i: (i, 0))
    out_w = (512, 512, 512, 512, 512, 256)
    return pl.pallas_call(
        functools.partial(_in_kernel, tm=tm, n_ctx=n_ctx),
        grid=(b, t // tm),
        in_specs=[
            row_blk(d),
            _resident((1, d)),
            pl.BlockSpec((1, 6, d), lambda bi, i: (bi, 0, 0)),
            pl.BlockSpec((1, 6, d), lambda bi, i: (b, 0, 0)),
            _resident(w_in_p.shape),
            tab_blk, tab_blk, tab_blk, tab_blk,
            _resident((1, MLA_Q_RANK)),
            _resident((1, MLA_KV_RANK)),
            _resident(wuq_p.shape),
            _resident(wkv_p.shape),
        ],
        out_specs=[row_blk(w) for w in out_w] + [row_blk(5 * HG_WIDTH)],
        out_shape=[jax.ShapeDtypeStruct((b, t, w), BF16) for w in out_w]
        + [jax.ShapeDtypeStruct((b, t, 5 * HG_WIDTH), F32)],
        compiler_params=_cparams(("parallel", "parallel")),
        name="in_proj",
    )(xa, gain, mods, mods, w_in_p, cd, sd, cm, sm, qn, kvn, wuq_p, wkv_p)


def _softmax_parts(s):
    m = jnp.max(s, axis=-1, keepdims=True)
    p = jnp.exp2(s - m)
    return p.astype(BF16), 1.0 / jnp.sum(p, axis=-1, keepdims=True)


def _attend_tile(attend, qi, q_ref, k_ref, v_ref, o_ref, tq, n_ctx):
    n_sub = tq // ATT_SUB
    ctx_sub = n_ctx // ATT_SUB

    def tile(first_latent):
        for j in range(n_sub):
            rows = slice(j * ATT_SUB, (j + 1) * ATT_SUB)
            if j < first_latent:
                o_ref[0, rows, :] = attend(q_ref[0, rows, :], k_ref[0, 0:n_ctx, :], v_ref[0, 0:n_ctx, :])
            else:
                o_ref[0, rows, :] = attend(q_ref[0, rows, :], k_ref[0], v_ref[0])

    @pl.when(qi == 0)
    def _():
        tile(ctx_sub)

    @pl.when(qi != 0)
    def _():
        tile(0)


def _da_kernel(q_ref, k_ref, v_ref, lamv_ref, lami_ref, gain_ref, o_ref, *, tq, n_ctx):
    qi = pl.program_id(2)
    lane = lax.broadcasted_iota(jnp.int32, (1, LANES), 1)
    map1 = jnp.where(lane < DA_HEAD_DIM, 1.0, 0.0).astype(BF16)
    map2 = jnp.where(lane < DA_HEAD_DIM, 0.0, 1.0).astype(BF16)
    lv = lamv_ref[...]
    lam = (jnp.exp(jnp.sum(lv[0:1] * lv[1:2], axis=-1, keepdims=True))
           - jnp.exp(jnp.sum(lv[2:3] * lv[3:4], axis=-1, keepdims=True)) + lami_ref[...])
    gain = gain_ref[...]

    def attend(q, k, v):
        p1, r1 = _softmax_parts(_dot_nt(q * map1, k))
        p2, r2 = _softmax_parts(_dot_nt(q * map2, k))
        o = _dot(p1, v) * r1 - _dot(p2, v) * (lam * r2)
        return (o * lax.rsqrt(jnp.mean(o * o, axis=-1, keepdims=True) + 1e-5) * gain).astype(BF16)

    _attend_tile(attend, qi, q_ref, k_ref, v_ref, o_ref, tq, n_ctx)


def _da_call(q, k, v, lam_vec, lam_init, gain_eff, n_ctx):
    b, t, _ = q.shape
    tq = _row_tile(t)
    return pl.pallas_call(
        functools.partial(_da_kernel, tq=tq, n_ctx=n_ctx),
        grid=(b, DA_HEADS, t // tq),
        in_specs=[
            pl.BlockSpec((1, tq, LANES), lambda bi, h, i: (bi, i, h)),
            pl.BlockSpec((1, t, LANES), lambda bi, h, i: (bi, 0, h)),
            pl.BlockSpec((1, t, LANES), lambda bi, h, i: (bi, 0, h)),
            pl.BlockSpec((4, DA_HEAD_DIM), lambda bi, h, i: (0, 0)),
            pl.BlockSpec((1, 1), lambda bi, h, i: (0, 0)),
            pl.BlockSpec((1, LANES), lambda bi, h, i: (0, 0)),
        ],
        out_specs=pl.BlockSpec((1, tq, LANES), lambda bi, h, i: (bi, i, h)),
        out_shape=jax.ShapeDtypeStruct((b, t, DA_HEADS * LANES), BF16),
        compiler_params=_cparams(("parallel", "parallel", "parallel")),
        name="diff_attention",
    )(q, k, v, lam_vec, lam_init, gain_eff)


def _mla_kernel(q_ref, k_ref, v_ref, o_ref, *, tq, n_ctx):
    qi = pl.program_id(1)
    lane = lax.broadcasted_iota(jnp.int32, (1, LANES), 1)
    low_half = lane < MLA_V

    def attend(q, k, v):
        pairs = []
        for pr in range(MLA_HEADS // 2):
            vp = v[:, pr * LANES:(pr + 1) * LANES]
            outs = []
            for h in (2 * pr, 2 * pr + 1):
                sl = slice(h * LANES, (h + 1) * LANES)
                p, r = _softmax_parts(_dot_nt(q[:, sl], k[:, sl]))
                outs.append(_dot(p, vp) * r)
            pairs.append(jnp.where(low_half, outs[0], outs[1]))
        return jnp.concatenate(pairs, axis=1).astype(BF16)

    _attend_tile(attend, qi, q_ref, k_ref, v_ref, o_ref, tq, n_ctx)


def _mla_call(q, k, v, n_ctx):
    b, t, _ = q.shape
    tq = _row_tile(t)
    wv = MLA_HEADS * MLA_V
    return pl.pallas_call(
        functools.partial(_mla_kernel, tq=tq, n_ctx=n_ctx),
        grid=(b, t // tq),
        in_specs=[
            pl.BlockSpec((1, tq, MLA_HEADS * LANES), lambda bi, i: (bi, i, 0)),
            pl.BlockSpec((1, t, MLA_HEADS * LANES), lambda bi, i: (bi, 0, 0)),
            pl.BlockSpec((1, t, wv), lambda bi, i: (bi, 0, 0)),
        ],
        out_specs=pl.BlockSpec((1, tq, wv), lambda bi, i: (bi, i, 0)),
        out_shape=jax.ShapeDtypeStruct((b, t, wv), BF16),
        compiler_params=_cparams(("parallel", "parallel")),
        name="latent_attention",
    )(q, k, v)


def _hgrn_kernel(hg_ref, lbf_ref, lbb_ref, gain_ref, o_ref,
                 kf_s, bf_s, kb_s, cb_s, v_s, qtf_s, ktf_s, qtb_s, ktb_s,
                 df_s, db_s, vt_s, oacc_s, stf_s, stb_s,
                 dcf_s, dcb_s, sfa_s, sfb_s, sba_s, sbb_s, *, t_len, n_ctx):
    w = HG_WIDTH
    rt = HG_TILE
    blk = HG_BLOCK
    pad = blk
    n_tiles = t_len // rt
    blk_per_tile = rt // blk

    r_i = lax.broadcasted_iota(jnp.int32, (rt, rt), 0)
    c_i = lax.broadcasted_iota(jnp.int32, (rt, rt), 1)
    same = (r_i // blk) == (c_i // blk)
    incl_lo = jnp.where(same & (c_i <= r_i), 1.0, 0.0).astype(BF16)
    incl_hi = jnp.where(same & (c_i >= r_i), 1.0, 0.0).astype(BF16)
    blk_sum = jnp.where(same, 1.0, 0.0).astype(BF16)
    sel_rows = 16
    g_r = lax.broadcasted_iota(jnp.int32, (sel_rows, rt), 0)
    g_c = lax.broadcasted_iota(jnp.int32, (sel_rows, rt), 1)
    blk_sel = jnp.where((g_c // blk) == g_r, 1.0, 0.0).astype(BF16)
    h_r = lax.broadcasted_iota(jnp.int32, (w, w), 0) // HG_KEY
    h_c = lax.broadcasted_iota(jnp.int32, (w, w), 1) // HG_KEY
    head_ones = jnp.where(h_r == h_c, 1.0, 0.0).astype(BF16)
    head_mask = jnp.where(h_r == h_c, 1.0, 0.0).astype(F32)
    pos = lax.broadcasted_iota(jnp.int32, (rt, 1), 0) % blk

    lbf = lbf_ref[...]
    lbb = lbb_ref[...]

    zpad = jnp.zeros((pad, w), F32)
    for s in (kf_s, bf_s, kb_s, cb_s, v_s):
        s[0:pad, :] = zpad
        s[pad + rt:pad + rt + pad, :] = zpad

    def gates(z, lb):
        f = lb + (1.0 - lb) * _sigmoid(z)
        return (1.0 - lb) * _sigmoid(-z), jnp.log(f)

    ch = HG_CHUNK
    ch_per_tile = rt // ch
    same_c = (r_i // ch) == (c_i // ch)
    lo_c = same_c & (c_i <= r_i)
    hi_c = same_c & (c_i >= r_i)
    cum_lo = jnp.where(lo_c, 1.0, 0.0).astype(BF16)
    cum_hi = jnp.where(hi_c, 1.0, 0.0).astype(BF16)
    ch_sum = jnp.where(same_c, 1.0, 0.0).astype(BF16)
    ch_sel = jnp.where((g_c // ch) == g_r, 1.0, 0.0).astype(BF16)
    head_of_lane = lax.broadcasted_iota(jnp.int32, (1, w), 1) // HG_KEY
    head_cols = [jnp.where(head_of_lane == h, 1.0, 0.0).astype(BF16) for h in range(HG_HEADS)]

    def chunk_tile(ti, worst):
        rows = pl.ds(pl.multiple_of(ti * rt, rt), rt)
        q = hg_ref[0, rows, 0:w] * (HG_KEY ** -0.5)
        v = hg_ref[0, rows, w:2 * w]
        vb = v.astype(BF16)
        od = jnp.zeros((rt, w), F32)
        for col, lb, cum, keep, qt_s, kt_s, dc_s in (
                (2, lbf, cum_lo, lo_c, qtf_s, ktf_s, dcf_s), (3, lbb, cum_hi, hi_c, qtb_s, ktb_s, dcb_s)):
            k, logf = gates(hg_ref[0, rows, col * w:(col + 1) * w], lb)
            l2 = logf * LOG2_E
            b = _dot_exact_lhs(cum, l2)
            tot = _dot_exact_lhs(ch_sum, l2)
            half = 0.5 * tot
            qs = (q * jnp.exp2(b - half)).astype(BF16)
            ks = (k * jnp.exp2(half - b)).astype(BF16)
            qt_s[rows, :] = (q * jnp.exp2(b)).astype(BF16)
            kt_s[rows, :] = (k * jnp.exp2(tot - b)).astype(BF16)
            dc_s[ti] = jnp.exp2(_dot_exact_lhs(ch_sel, l2)[0:8])
            worst = jnp.maximum(worst, jnp.max(-tot, axis=0, keepdims=True))
            for h in range(HG_HEADS):
                p = jnp.where(keep, _dot_nt(qs * head_cols[h], ks), 0.0)
                od = od + _dot(p.astype(BF16), vb * head_cols[h])
        vt_s[:, rows] = v.T.astype(BF16)
        oacc_s[rows, :] = od
        return worst

    worst = lax.fori_loop(0, n_tiles, chunk_tile, jnp.zeros((1, w), F32), unroll=2)
    exact_needed = jnp.max(worst) > HG_CHUNK_RANGE

    half_w = w // 2
    p_r = lax.broadcasted_iota(jnp.int32, (half_w, half_w), 0) // HG_KEY
    p_c = lax.broadcasted_iota(jnp.int32, (half_w, half_w), 1) // HG_KEY
    pair_mask = jnp.where(p_r == p_c, 1.0, 0.0).astype(F32)
    for s in (sfa_s, sfb_s, sba_s, sbb_s):
        s[...] = jnp.zeros((half_w, half_w), F32)
    lane_c = lax.broadcasted_iota(jnp.int32, (1, LANES), 1)

    def advance_chunk(ci, qt_s, kt_s, dc_s, sa_s, sb_s):
        r = pl.multiple_of(ci * ch, ch)
        rows = pl.ds(r, ch)
        sa = sa_s[...]
        sb = sb_s[...]
        oi = jnp.concatenate([_dot_nt(qt_s[rows, 0:half_w], sa.astype(BF16)),
                              _dot_nt(qt_s[rows, half_w:w], sb.astype(BF16))], axis=1)
        oacc_s[rows, :] = oacc_s[rows, :] + oi
        slab = pl.multiple_of((r // LANES) * LANES, LANES)
        off = r - slab
        vt = vt_s[:, pl.ds(slab, LANES)]
        vt = vt * jnp.where((lane_c >= off) & (lane_c < off + ch), 1.0, 0.0).astype(BF16)
        kt = kt_s[pl.ds(slab, LANES), :]
        dec = dc_s[ci // ch_per_tile, pl.ds(ci % ch_per_tile, 1), :]
        sa_s[...] = sa * dec[:, 0:half_w] + _dot(vt[0:half_w, :], kt[:, 0:half_w]) * pair_mask
        sb_s[...] = sb * dec[:, half_w:w] + _dot(vt[half_w:w, :], kt[:, half_w:w]) * pair_mask

    n_ch = t_len // ch
    n_cch = n_ctx // ch

    def chunk_step(i, carry):
        advance_chunk(i, qtf_s, ktf_s, dcf_s, sfa_s, sfb_s)
        jb = jnp.where(i < n_cch, n_cch - 1 - i, n_ch - 1 - (i - n_cch))
        advance_chunk(jb, qtb_s, ktb_s, dcb_s, sba_s, sbb_s)
        return carry

    lax.fori_loop(0, jnp.where(exact_needed, 0, n_ch), chunk_step, 0)

    n_tiles_x = jnp.where(exact_needed, n_tiles, 0)

    def tile_body(ti, carry):
        rows = pl.ds(pl.multiple_of(ti * rt, rt), rt)
        q = hg_ref[0, rows, 0:w] * (HG_KEY ** -0.5)
        v = hg_ref[0, rows, w:2 * w]
        kf, lf = gates(hg_ref[0, rows, 2 * w:3 * w], lbf)
        kb, lg = gates(hg_ref[0, rows, 3 * w:4 * w], lbb)
        b_f = _dot_exact_lhs(incl_lo, lf)
        tot_f = _dot_exact_lhs(blk_sum, lf)
        c_b = _dot_exact_lhs(incl_hi, lg)
        tot_b = _dot_exact_lhs(blk_sum, lg)
        qtf_s[rows, :] = (q * jnp.exp(b_f)).astype(BF16)
        ktf_s[rows, :] = (kf * jnp.exp(tot_f - b_f)).astype(BF16)
        qtb_s[rows, :] = (q * jnp.exp(c_b)).astype(BF16)
        ktb_s[rows, :] = (kb * jnp.exp(tot_b - c_b)).astype(BF16)
        drows = pl.ds(pl.multiple_of(ti * blk_per_tile, blk_per_tile), blk_per_tile)
        df_s[drows, :] = jnp.exp(_dot_exact_lhs(blk_sel, lf)[0:blk_per_tile])
        db_s[drows, :] = jnp.exp(_dot_exact_lhs(blk_sel, lg)[0:blk_per_tile])
        vt_s[:, rows] = v.T.astype(BF16)

        kf_s[pad:pad + rt, :] = kf
        bf_s[pad:pad + rt, :] = b_f
        kb_s[pad:pad + rt, :] = kb
        cb_s[pad:pad + rt, :] = c_b
        v_s[pad:pad + rt, :] = v
        od = jnp.zeros((rt, w), F32)
        for dl in range(blk):
            lo = pad - dl
            e = q * kf_s[lo:lo + rt, :] * jnp.exp(jnp.minimum(b_f - bf_s[lo:lo + rt, :], 0.0))
            e = jnp.where(pos >= dl, e, 0.0)
            od = od + _dot(e.astype(BF16), head_ones) * v_s[lo:lo + rt, :]
            hi = pad + dl
            e = q * kb_s[hi:hi + rt, :] * jnp.exp(jnp.minimum(c_b - cb_s[hi:hi + rt, :], 0.0))
            e = jnp.where(pos + dl < blk, e, 0.0)
            od = od + _dot(e.astype(BF16), head_ones) * v_s[hi:hi + rt, :]
        oacc_s[rows, :] = od
        return carry

    lax.fori_loop(0, n_tiles_x, tile_body, 0)

    stf_s[...] = jnp.zeros((w, w), F32)
    stb_s[...] = jnp.zeros((w, w), F32)
    n_blk = t_len // blk
    n_cblk = n_ctx // blk
    lane = lax.broadcasted_iota(jnp.int32, (1, LANES), 1)

    def advance(bi, qt_s, kt_s, d_s, st_s):
        r = pl.multiple_of(bi * blk, blk)
        st = st_s[...]
        rows = pl.ds(r, blk)
        oacc_s[rows, :] = oacc_s[rows, :] + _dot_nt(qt_s[rows, :], st.astype(BF16))
        slab = pl.multiple_of((r // LANES) * LANES, LANES)
        off = r - slab
        vt = vt_s[:, pl.ds(slab, LANES)]
        vt = vt * jnp.where((lane >= off) & (lane < off + blk), 1.0, 0.0).astype(BF16)
        upd = _dot(vt, kt_s[pl.ds(slab, LANES), :])
        st_s[...] = st * d_s[pl.ds(bi, 1), :] + upd * head_mask

    def step(i, carry):
        advance(i, qtf_s, ktf_s, df_s, stf_s)
        jb = jnp.where(i < n_cblk, n_cblk - 1 - i, n_blk - 1 - (i - n_cblk))
        advance(jb, qtb_s, ktb_s, db_s, stb_s)
        return carry

    lax.fori_loop(0, jnp.where(exact_needed, n_blk, 0), step, 0)

    gain = gain_ref[...]

    def out_body(ti, carry):
        rows = pl.ds(pl.multiple_of(ti * rt, rt), rt)
        o = oacc_s[rows, :]
        ms = _dot_exact_rhs(o * o, head_ones) * (1.0 / HG_KEY)
        y = o * lax.rsqrt(ms + 1e-6) * gain
        g = hg_ref[0, rows, 4 * w:5 * w]
        o_ref[0, rows, :] = (y * (g * _sigmoid(g))).astype(BF16)
        return carry

    lax.fori_loop(0, n_tiles, out_body, 0)


def _hgrn_call(hg, lbf, lbb, gain_t, n_ctx):
    b, t, _ = hg.shape
    w = HG_WIDTH
    tile_pad = pltpu.VMEM((HG_TILE + 2 * HG_BLOCK, w), F32)
    seq_bf16 = pltpu.VMEM((t, w), BF16)
    dec = pltpu.VMEM((t // HG_BLOCK, w), F32)
    return pl.pallas_call(
        functools.partial(_hgrn_kernel, t_len=t, n_ctx=n_ctx),
        grid=(b,),
        in_specs=[
            pl.BlockSpec((1, t, 5 * w), lambda bi: (bi, 0, 0)),
            pl.BlockSpec((1, w), lambda bi: (0, 0)),
            pl.BlockSpec((1, w), lambda bi: (0, 0)),
            pl.BlockSpec((1, w), lambda bi: (0, 0)),
        ],
        out_specs=pl.BlockSpec((1, t, w), lambda bi: (bi, 0, 0)),
        out_shape=jax.ShapeDtypeStruct((b, t, w), BF16),
        scratch_shapes=[tile_pad] * 5 + [seq_bf16] * 4 + [dec, dec,
                        pltpu.VMEM((w, t), BF16), pltpu.VMEM((t, w), F32),
                        pltpu.VMEM((w, w), F32), pltpu.VMEM((w, w), F32)]
        + [pltpu.VMEM((t // HG_TILE, 8, w), F32)] * 2 + [pltpu.VMEM((w // 2, w // 2), F32)] * 4,
        compiler_params=_cparams(("parallel",)),
        name="hgrn2",
    )(hg, lbf, lbb, gain_t)


def _mod_rows(ml_ref, mc_ref, tm, n_ctx, idx):
    i = pl.program_id(1)
    row = i * tm + lax.broadcasted_iota(jnp.int32, (tm, 1), 0)
    isctx = row < n_ctx
    ml = ml_ref[0]
    mc = mc_ref[0]
    return [jnp.where(isctx, mc[k:k + 1], ml[k:k + 1]) for k in idx]


def _out_kernel(oa_ref, ob_ref, og_ref, wo_ref, x_ref, ml_ref, mc_ref, gain_ref, *rest,
                tm, n_ctx, routed, d_ff):
    if routed:
        router_ref, xo_ref, h_ref, gates_ref = rest
    else:
        wg_ref, wu_ref, wd_ref, xo_ref, acc_ref = rest
    gate, shift, scale, ffn_gate = _mod_rows(ml_ref, mc_ref, tm, n_ctx, (2, 3, 4, 5))
    mix = (_dot(oa_ref[0], wo_ref[0:512, :]) + _dot(ob_ref[0], wo_ref[512:768, :])
           + _dot(og_ref[0], wo_ref[768:1024, :]))
    x = x_ref[0] + gate * mix
    y = x * lax.rsqrt(jnp.mean(x * x, axis=-1, keepdims=True) + 1e-6) * gain_ref[...]
    h = y * (1.0 + scale) + shift
    if not routed:
        acc_ref[...] = jnp.zeros_like(acc_ref)
        _swiglu_into(acc_ref, h.astype(BF16), wg_ref, wu_ref, wd_ref, None, d_ff)
        xo_ref[0] = x + ffn_gate * acc_ref[...]
    else:
        xo_ref[0] = x
        h_ref[0] = h
    if routed:
        lane = lax.broadcasted_iota(jnp.int32, (1, LANES), 1)
        logits = _dot_f32(h, router_ref[...])
        logits = jnp.where(lane < N_EXPERTS, logits, -jnp.inf)
        m1 = jnp.max(logits, axis=-1, keepdims=True)
        i1 = jnp.min(jnp.where(logits == m1, lane, LANES), axis=-1, keepdims=True)
        rest_l = jnp.where(lane == i1, -jnp.inf, logits)
        m2 = jnp.max(rest_l, axis=-1, keepdims=True)
        i2 = jnp.min(jnp.where(rest_l == m2, lane, LANES), axis=-1, keepdims=True)
        e2 = jnp.exp(m2 - m1)
        w1 = 1.0 / (1.0 + e2)
        route = jnp.where(lane == 0, i1.astype(F32), 0.0) + jnp.where(lane == 1, i2.astype(F32), 0.0)
        gates_ref[0] = route + jnp.where(lane == 2, w1, 0.0) + jnp.where(lane == 3, e2 * w1, 0.0)


def _out_call(oa, ob, og, wo, xa, mods, gain, n_ctx, router_p=None, ffn=None, skip_rows=0):
    b, t, d = xa.shape
    routed = router_p is not None
    tm = _row_tile(t) if skip_rows == 0 else ATT_SUB
    skip = skip_rows // tm
    t_out = t - skip_rows
    ctx_rows = n_ctx - skip_rows
    in_blk = lambda width: pl.BlockSpec((1, tm, width), lambda bi, i: (bi, i + skip, 0))
    out_blk = lambda width: pl.BlockSpec((1, tm, width), lambda bi, i: (bi, i, 0))
    in_specs = [
        in_blk(oa.shape[-1]), in_blk(ob.shape[-1]), in_blk(og.shape[-1]),
        _resident(wo.shape),
        in_blk(d),
        pl.BlockSpec((1, 6, d), lambda bi, i: (bi, 0, 0)),
        pl.BlockSpec((1, 6, d), lambda bi, i: (b, 0, 0)),
        _resident((1, d)),
    ]
    args = [oa, ob, og, wo, xa, mods, mods, gain]
    scratch = []
    d_ff = 0
    if routed:
        in_specs.append(_resident(router_p.shape))
        args.append(router_p)
        out_specs = [out_blk(d), out_blk(d), out_blk(LANES)]
        out_shape = [jax.ShapeDtypeStruct((b, t_out, d), F32), jax.ShapeDtypeStruct((b, t_out, d), F32),
                     jax.ShapeDtypeStruct((b, t_out, LANES), F32)]
    else:
        wg, wu, wd = ffn
        d_ff = wg.shape[-1]
        in_specs += [_resident((1, d, d_ff)), _resident((1, d, d_ff)), _resident((1, d_ff, d))]
        args += [wg[None], wu[None], wd[None]]
        out_specs = out_blk(d)
        out_shape = jax.ShapeDtypeStruct((b, t_out, d), F32)
        scratch = [pltpu.VMEM((tm, d), F32)]
    return pl.pallas_call(
        functools.partial(_out_kernel, tm=tm, n_ctx=ctx_rows, routed=routed, d_ff=d_ff),
        grid=(b, t_out // tm),
        in_specs=in_specs,
        out_specs=out_specs,
        out_shape=out_shape,
        scratch_shapes=scratch,
        compiler_params=_cparams(("parallel", "parallel")),
        name="out_proj_routed" if routed else "out_proj_ffn",
    )(*args)


def _swiglu_into(acc_ref, h, wg_ref, wu_ref, wd_ref, row_gate, d_ff):
    for c in range(d_ff // FF_CHUNK):
        cs = slice(c * FF_CHUNK, (c + 1) * FF_CHUNK)
        g = _dot(h, wg_ref[0, :, cs])
        u = _dot(h, wu_ref[0, :, cs])
        a = g * _sigmoid(g) * u
        if row_gate is not None:
            a = a * row_gate
        acc_ref[...] += _dot(a.astype(BF16), wd_ref[0, cs, :])


def _row_copy(src, src_row, dst, dst_row, sem):
    return pltpu.make_async_copy(src.at[pl.ds(src_row, 1), :], dst.at[pl.ds(dst_row, 1), :], sem)


def _dispatch_kernel(pos_ref, pad_ref, h_ref, xs_ref, zero_s, sem, zsem, *, tm, n_exp, n_tiles):
    step = pl.program_id(0)
    base = step * tm

    def pad_copies(act):
        for e in range(n_exp):
            start = pad_ref[e]

            def tail_row(r, carry):
                act(_row_copy(zero_s, 0, xs_ref, start + r, zsem))
                return carry

            lax.fori_loop(0, pad_ref[n_exp + e], tail_row, 0)
        for j in range(n_exp):
            tile = pad_ref[2 * n_exp] + j

            @pl.when(tile < n_tiles)
            def _():
                row0 = pl.multiple_of(tile * tm, tm)
                act(pltpu.make_async_copy(zero_s, xs_ref.at[pl.ds(row0, tm), :], zsem))

    @pl.when(step == 0)
    def _():
        zero_s[...] = jnp.zeros_like(zero_s)
        pad_copies(lambda c: c.start())

    def issue(r, carry):
        for s in range(TOP_K):
            _row_copy(h_ref, r, xs_ref, pos_ref[TOP_K * (base + r) + s], sem).start()
        return carry

    lax.fori_loop(0, tm, issue, 0, unroll=8)

    def drain(r, carry):
        for s in range(TOP_K):
            _row_copy(h_ref, r, xs_ref, pos_ref[TOP_K * (base + r) + s], sem).wait()
        return carry

    lax.fori_loop(0, tm, drain, 0, unroll=8)

    @pl.when(step == 0)
    def _():
        pad_copies(lambda c: c.wait())


def _dispatch_call(pos, pad_info, h2d, n_tiles, n_exp):
    n, d = h2d.shape
    tm = MOE_TILE
    return pl.pallas_call(
        functools.partial(_dispatch_kernel, tm=tm, n_exp=n_exp, n_tiles=n_tiles),
        grid_spec=pltpu.PrefetchScalarGridSpec(
            num_scalar_prefetch=2,
            grid=(n // tm,),
            in_specs=[pl.BlockSpec((tm, d), lambda i, pos, pad: (i, 0))],
            out_specs=pl.BlockSpec(memory_space=pl.ANY),
            scratch_shapes=[pltpu.VMEM((tm, d), F32), pltpu.SemaphoreType.DMA(()),
                            pltpu.SemaphoreType.DMA(())],
        ),
        out_shape=jax.ShapeDtypeStruct((n_tiles * tm, d), F32),
        compiler_params=_cparams(("arbitrary",)),
        name="moe_dispatch",
    )(pos, pad_info, h2d)


def _expert_kernel(te_ref, nv_ref, xs_ref, wg_ref, wu_ref, wd_ref, ys_ref, acc_ref, *, tm, d_ff):
    n_valid = nv_ref[pl.program_id(0)]

    @pl.when(n_valid > 0)
    def _():
        acc_ref[...] = jnp.zeros_like(acc_ref)
        _swiglu_into(acc_ref, xs_ref[...].astype(BF16), wg_ref, wu_ref, wd_ref, None, d_ff)
        ys_ref[...] = acc_ref[...]

    @pl.when(n_valid == 0)
    def _():
        ys_ref[...] = jnp.zeros_like(ys_ref)


def _expert_call(tile_expert, tile_valid, xs, wg, wu, wd):
    n_slots, d = xs.shape
    d_ff = wg.shape[-1]
    tm = MOE_TILE
    w_blk = lambda shape: pl.BlockSpec(shape, lambda t, te, nv: (te[t], 0, 0))
    return pl.pallas_call(
        functools.partial(_expert_kernel, tm=tm, d_ff=d_ff),
        grid_spec=pltpu.PrefetchScalarGridSpec(
            num_scalar_prefetch=2,
            grid=(n_slots // tm,),
            in_specs=[
                pl.BlockSpec((tm, d), lambda t, te, nv: (t, 0)),
                w_blk((1, d, d_ff)), w_blk((1, d, d_ff)), w_blk((1, d_ff, d)),
            ],
            out_specs=pl.BlockSpec((tm, d), lambda t, te, nv: (t, 0)),
            scratch_shapes=[pltpu.VMEM((tm, d), F32)],
        ),
        out_shape=jax.ShapeDtypeStruct((n_slots, d), F32),
        compiler_params=_cparams(("arbitrary",)),
        name="moe_experts",
    )(tile_expert, tile_valid, xs, wg, wu, wd)


def _combine_kernel(pos_ref, ys_ref, x_ref, route_ref, ml_ref, mc_ref, fgain_ref, xo_ref, y1_s, y2_s, sem,
                    *, tm, n_ctx, final_norm):
    base = (pl.program_id(0) * pl.num_programs(1) + pl.program_id(1)) * tm
    bufs = (y1_s, y2_s)

    def issue(r, carry):
        for s in range(TOP_K):
            _row_copy(ys_ref, pos_ref[TOP_K * (base + r) + s], bufs[s], r, sem).start()
        return carry

    lax.fori_loop(0, tm, issue, 0, unroll=8)

    def drain(r, carry):
        for s in range(TOP_K):
            _row_copy(ys_ref, pos_ref[TOP_K * (base + r) + s], bufs[s], r, sem).wait()
        return carry

    lax.fori_loop(0, tm, drain, 0, unroll=8)

    lane = lax.broadcasted_iota(jnp.int32, (1, LANES), 1)
    route = route_ref[0]
    w1 = jnp.sum(jnp.where(lane == 2, route, 0.0), axis=-1, keepdims=True)
    w2 = jnp.sum(jnp.where(lane == 3, route, 0.0), axis=-1, keepdims=True)
    (gate,) = _mod_rows(ml_ref, mc_ref, tm, n_ctx, (5,))
    x = x_ref[0] + gate * (w1 * y1_s[...] + w2 * y2_s[...])
    if final_norm:
        x = x * lax.rsqrt(jnp.mean(x * x, axis=-1, keepdims=True) + 1e-6) * fgain_ref[...]
    xo_ref[0] = x


def _combine_call(pos, ys, xa, route, mods, n_ctx, final_gain, final_norm):
    b, t, d = xa.shape
    tm = _row_tile(t)
    row_blk = lambda width: pl.BlockSpec((1, tm, width), lambda bi, i, pos: (bi, i, 0))
    return pl.pallas_call(
        functools.partial(_combine_kernel, tm=tm, n_ctx=n_ctx, final_norm=final_norm),
        grid_spec=pltpu.PrefetchScalarGridSpec(
            num_scalar_prefetch=1,
            grid=(b, t // tm),
            in_specs=[
                pl.BlockSpec(memory_space=pl.ANY),
                row_blk(d), row_blk(LANES),
                pl.BlockSpec((1, 6, d), lambda bi, i, pos: (bi, 0, 0)),
                pl.BlockSpec((1, 6, d), lambda bi, i, pos: (b, 0, 0)),
                pl.BlockSpec((1, d), lambda bi, i, pos: (0, 0)),
            ],
            out_specs=row_blk(d),
            scratch_shapes=[pltpu.VMEM((tm, d), F32), pltpu.VMEM((tm, d), F32),
                            pltpu.SemaphoreType.DMA(())],
        ),
        out_shape=jax.ShapeDtypeStruct((b, t, d), F32),
        compiler_params=_cparams(("arbitrary", "arbitrary")),
        name="moe_combine",
    )(pos, ys, xa, route, mods, mods, final_gain)


def _moe_call(h, xa, route, mods, wg, wu, wd, n_ctx, final_gain, final_norm):
    b, t, d = xa.shape
    n = b * t
    tm = MOE_TILE
    n_exp = wg.shape[0]
    n_tiles = (TOP_K * n) // tm + n_exp
    choice = route.reshape(n, LANES)[:, :TOP_K].astype(jnp.int32).reshape(-1)
    onehot = (choice[:, None] == jnp.arange(n_exp, dtype=jnp.int32)[None, :]).astype(jnp.int32)
    csum = jnp.cumsum(onehot, axis=0)
    counts = csum[-1]
    rank = jnp.sum((csum - onehot) * onehot, axis=1)
    group_tiles = (counts + tm - 1) // tm
    tile_end = jnp.cumsum(group_tiles)
    group_row0 = (tile_end - group_tiles) * tm
    pos = jnp.sum(group_row0[None, :] * onehot, axis=1) + rank
    tile_id = jnp.arange(n_tiles, dtype=jnp.int32)
    tile_group = jnp.sum((tile_id[:, None] >= tile_end[None, :]).astype(jnp.int32), axis=1)
    last_group = jnp.max(jnp.where(counts > 0, jnp.arange(n_exp, dtype=jnp.int32), 0))
    tile_expert = jnp.minimum(tile_group, last_group)
    in_group = tile_group < n_exp
    g = jnp.minimum(tile_group, n_exp - 1)
    tile_valid = jnp.where(in_group, jnp.clip(counts[g] - (tile_id * tm - group_row0[g]), 0, tm), 0)

    pad_info = jnp.concatenate([group_row0 + counts, group_tiles * tm - counts, tile_end[-1:]])
    xs = _dispatch_call(pos, pad_info.astype(jnp.int32), h.reshape(n, d), n_tiles, n_exp)
    ys = _expert_call(tile_expert.astype(jnp.int32), tile_valid.astype(jnp.int32), xs, wg, wu, wd)
    return _combine_call(pos, ys, xa, route, mods, n_ctx, final_gain, final_norm)


def _final_kernel(x_ref, g_ref, o_ref):
    x = x_ref[0]
    o_ref[0] = x * lax.rsqrt(jnp.mean(x * x, axis=-1, keepdims=True) + 1e-6) * g_ref[...]


def _final_call(xa, gain, n_ctx):
    b, t, d = xa.shape
    tm = 256
    skip = n_ctx // tm
    return pl.pallas_call(
        _final_kernel,
        grid=(b, (t - n_ctx) // tm),
        in_specs=[pl.BlockSpec((1, tm, d), lambda bi, i: (bi, i + skip, 0)), _resident((1, d))],
        out_specs=pl.BlockSpec((1, tm, d), lambda bi, i: (bi, i, 0)),
        out_shape=jax.ShapeDtypeStruct((b, t - n_ctx, d), F32),
        compiler_params=_cparams(("parallel", "parallel")),
        name="final_norm",
    )(xa, gain)


def _rope_tables(seq, n_ctx):
    rows = seq // GRID_W
    row, col = jnp.meshgrid(jnp.arange(rows, dtype=F32), jnp.arange(GRID_W, dtype=F32), indexing="ij")
    row, col = row.reshape(-1), col.reshape(-1)

    def angles(rot_dim):
        n_freq = rot_dim // 4
        inv = ROPE_BASE ** (-jnp.arange(n_freq, dtype=F32) / n_freq)
        ang = jnp.concatenate([row[:, None] * inv, col[:, None] * inv], axis=-1)
        return jnp.cos(ang), jnp.sin(ang)

    def with_ctx(a, fill):
        return jnp.concatenate([jnp.full((n_ctx, a.shape[1]), fill, F32), a], axis=0)

    cos, sin = angles(DA_HEAD_DIM)
    cos_da = with_ctx(jnp.tile(cos, (1, 4)), 1.0)
    sin_da = with_ctx(jnp.tile(jnp.concatenate([-sin, sin], axis=1), (1, 2)), 0.0)
    cos, sin = angles(MLA_ROPE)
    ones = jnp.ones((seq, MLA_NOPE), F32)
    tail = jnp.ones((seq, LANES - MLA_NOPE - MLA_ROPE), F32)
    cos_m = with_ctx(jnp.concatenate([ones, cos, cos, tail], axis=1), 1.0)
    sin_m = with_ctx(jnp.concatenate([0 * ones, -sin, sin, 0 * tail], axis=1), 0.0)
    return cos_da, sin_da, cos_m, sin_m


def _pad_in_proj(w_in_l):
    d = w_in_l.shape[0]
    cut = 1536 + MLA_Q_RANK + MLA_KV_RANK + MLA_ROPE
    padw = 2048 - cut
    return jnp.concatenate([w_in_l[:, :cut], jnp.zeros((d, padw), w_in_l.dtype), w_in_l[:, cut:]],
                           axis=1).astype(BF16)


def _pad_uq(w_uq_l):
    qk = MLA_NOPE + MLA_ROPE
    w = w_uq_l.reshape(MLA_Q_RANK, MLA_HEADS, qk)
    w = jnp.pad(w, ((0, 0), (0, 0), (0, LANES - qk)))
    return w.reshape(MLA_Q_RANK, MLA_HEADS * LANES).astype(BF16)


def _pad_ukv(w_ukv_l):
    w = w_ukv_l.reshape(MLA_KV_RANK, MLA_HEADS, MLA_NOPE + MLA_V)
    k_part = jnp.pad(w[:, :, :MLA_NOPE], ((0, 0), (0, 0), (0, LANES - MLA_NOPE)))
    k_part = k_part.reshape(MLA_KV_RANK, MLA_HEADS * LANES)
    v_part = w[:, :, MLA_NOPE:].reshape(MLA_KV_RANK, MLA_HEADS * MLA_V)
    place = jnp.zeros((MLA_ROPE, MLA_HEADS, LANES), F32)
    eye = jnp.eye(MLA_ROPE, dtype=F32)
    place = place.at[:, :, MLA_NOPE:MLA_NOPE + MLA_ROPE].set(eye[:, None, :])
    place = place.reshape(MLA_ROPE, MLA_HEADS * LANES)
    top = jnp.concatenate([k_part, v_part], axis=1)
    mid = jnp.concatenate([place, jnp.zeros((MLA_ROPE, MLA_HEADS * MLA_V), F32)], axis=1)
    bot = jnp.zeros((MXU_COLS - MLA_KV_RANK - MLA_ROPE, top.shape[1]), F32)
    return jnp.concatenate([top, mid, bot], axis=0).astype(BF16)


def kernel(x, c, ctx, c_ctx, w_mod, b_mod, norm_mix, norm_ffn, norm_final, w_in, w_out, diff_lambda, diff_norm, mla_q_norm, mla_w_uq, mla_kv_norm, mla_w_ukv, hgrn_lb_logits, hgrn_norm, ffn_w_gate, ffn_w_up, ffn_w_down, moe_router, moe_w_gate, moe_w_up, moe_w_down):
    b, seq, d = x.shape
    n_ctx = ctx.shape[1]
    depth = w_mod.shape[0]
    assert d == 1024 and n_ctx % ATT_SUB == 0 and seq % ATT_SUB == 0

    xa = jnp.concatenate([ctx, x], axis=1)
    bp = -(-(b + 1) // 8) * 8
    c_all = jnp.concatenate([c, c_ctx[None, :], jnp.zeros((bp - b - 1, d), F32)], axis=0)
    mods = _mod_call(c_all, w_mod, b_mod).reshape(depth, bp, 6, d)

    lb = jnp.cumsum(jax.nn.softmax(hgrn_lb_logits.astype(F32), axis=1), axis=1)
    lb = lb - lb[:, :1]
    tabs = _rope_tables(seq, n_ctx)

    for l in range(depth):
        lam_init = 0.8 - 0.6 * math.exp(-0.3 * l)
        q_da, k_da, v_da, q_m, k_m, v_m, hg = _in_call(
            xa, norm_mix[l][None], mods[l], _pad_in_proj(w_in[l]), tabs,
            mla_q_norm[l][None], mla_kv_norm[l][None], _pad_uq(mla_w_uq[l]), _pad_ukv(mla_w_ukv[l]),
            n_ctx)
        oa = _da_call(q_da, k_da, v_da, diff_lambda[l], jnp.full((1, 1), lam_init, F32),
                      (diff_norm[l] * (1.0 - lam_init))[None], n_ctx)
        ob = _mla_call(q_m, k_m, v_m, n_ctx)
        og = _hgrn_call(hg, lb[0, l][None], lb[1, l][None], jnp.tile(hgrn_norm[l], HG_HEADS)[None], n_ctx)

        wo = w_out[l].astype(BF16)
        skip = n_ctx if l == depth - 1 else 0
        if l % 2 == 0:
            ffn = (ffn_w_gate[l // 2].astype(BF16), ffn_w_up[l // 2].astype(BF16),
                   ffn_w_down[l // 2].astype(BF16))
            xa = _out_call(oa, ob, og, wo, xa, mods[l], norm_ffn[l][None], n_ctx, ffn=ffn, skip_rows=skip)
        else:
            router_p = jnp.pad(moe_router[l // 2], ((0, 0), (0, LANES - N_EXPERTS)))
            xa, h, route = _out_call(oa, ob, og, wo, xa, mods[l], norm_ffn[l][None], n_ctx,
                                     router_p=router_p, skip_rows=skip)
            xa = _moe_call(h, xa, route, mods[l], moe_w_gate[l // 2].astype(BF16),
                           moe_w_up[l // 2].astype(BF16), moe_w_down[l // 2].astype(BF16), n_ctx - skip,
                           norm_final[None], l == depth - 1)
    if depth % 2 == 0:
        return xa
    return _final_call(xa, norm_final[None], 0)
```

```python
import functools
import math

import jax
import jax.numpy as jnp
from jax import lax
from jax.experimental import pallas as pl
from jax.experimental.pallas import tpu as pltpu

F32 = jnp.float32
BF16 = jnp.bfloat16

GRID_W = 64
ROPE_BASE = 10000.0
DA_HEADS = 4
DA_HEAD_DIM = 64
MLA_HEADS = 4
MLA_Q_RANK = 256
MLA_KV_RANK = 128
MLA_NOPE = 64
MLA_ROPE = 32
MLA_V = 64
HG_HEADS = 4
HG_KEY = 64
HG_WIDTH = HG_HEADS * HG_KEY
N_EXPERTS = 8
TOP_K = 2
LOG2_E = 1.4426950408889634

LANES = 128
MXU_COLS = 256
VMEM_LIMIT = 56 * 1024 * 1024

HG_BLOCK = 16
HG_CHUNK = 64
HG_CHUNK_RANGE = 180.0
HG_TILE = 128
ATT_SUB = 256
IN_SPLIT = 1
DA_HEADS_PER_STEP = 2
FF_CHUNK = 256
MOE_TILE = 512


def _dot(a, b):
    return jnp.dot(a, b, preferred_element_type=F32)


def _dot_nt(a, b):
    return lax.dot_general(a, b, (((1,), (1,)), ((), ())), preferred_element_type=F32)


def _split3(x):
    x0 = x.astype(BF16)
    r = x - x0.astype(F32)
    x1 = r.astype(BF16)
    x2 = (r - x1.astype(F32)).astype(BF16)
    return x0, x1, x2


def _dot_exact_lhs(a_bf16, x):
    x0, x1, x2 = _split3(x)
    return _dot(a_bf16, x0) + _dot(a_bf16, x1) + _dot(a_bf16, x2)


def _dot_exact_rhs(x, b_bf16):
    x0, x1, x2 = _split3(x)
    return _dot(x0, b_bf16) + _dot(x1, b_bf16) + _dot(x2, b_bf16)


def _dot_f32(a, b):
    a0 = a.astype(BF16)
    a1 = (a - a0.astype(F32)).astype(BF16)
    b0 = b.astype(BF16)
    b1 = (b - b0.astype(F32)).astype(BF16)
    return _dot(a0, b0) + _dot(a0, b1) + _dot(a1, b0)


def _sigmoid(z):
    return 1.0 / (1.0 + jnp.exp(-z))


def _row_tile(t):
    return next(tm for tm in (768, 512, 256) if t % tm == 0)


def _cparams(sem):
    return pltpu.CompilerParams(dimension_semantics=sem, vmem_limit_bytes=VMEM_LIMIT)


def _resident(shape):
    nd = len(shape)
    return pl.BlockSpec(shape, lambda *_: (0,) * nd, pipeline_mode=pl.Buffered(1))


def _mod_kernel(c_ref, w_ref, b_ref, o_ref):
    c = c_ref[...]
    o_ref[0] = _dot_f32(c * _sigmoid(c), w_ref[0]) + b_ref[0]


def _mod_call(c_all, w_mod, b_mod):
    depth, d, n6 = w_mod.shape
    bp = c_all.shape[0]
    tn = 1536
    return pl.pallas_call(
        _mod_kernel,
        grid=(depth, n6 // tn),
        in_specs=[
            pl.BlockSpec((bp, d), lambda l, j: (0, 0)),
            pl.BlockSpec((1, d, tn), lambda l, j: (l, 0, j)),
            pl.BlockSpec((1, 1, tn), lambda l, j: (l, 0, j)),
        ],
        out_specs=pl.BlockSpec((1, bp, tn), lambda l, j: (l, 0, j)),
        out_shape=jax.ShapeDtypeStruct((depth, bp, n6), F32),
        compiler_params=_cparams(("parallel", "parallel")),
        name="modulation",
    )(c_all, w_mod, b_mod.reshape(depth, 1, n6))


def _rope128(seg, cos, sin, mla):
    lane = lax.broadcasted_iota(jnp.int32, (1, LANES), 1)
    if mla:
        half = MLA_ROPE // 2
        take_next = lane < MLA_NOPE + half
    else:
        half = DA_HEAD_DIM // 2
        take_next = (lane % DA_HEAD_DIM) < half
    sw = jnp.where(take_next, pltpu.roll(seg, LANES - half, 1), pltpu.roll(seg, half, 1))
    return seg * cos + sw * sin


def _in_kernel(x_ref, gain_ref, ml_ref, mc_ref, w_ref, cd_ref, sd_ref, cm_ref, sm_ref,
               qn_ref, kvn_ref, wuq_ref, wkv_ref,
               oq_ref, ok_ref, ov_ref, mq_ref, mk_ref, mv_ref, hg_ref, *, tm, n_ctx):
    i = pl.program_id(1)
    ml = ml_ref[0]
    mc = mc_ref[0]
    da_scale = DA_HEAD_DIM ** -0.5 * LOG2_E
    mla_scale = (MLA_NOPE + MLA_ROPE) ** -0.5 * LOG2_E
    th = tm // IN_SPLIT

    for part in range(IN_SPLIT):
        rows = slice(part * th, (part + 1) * th)
        x = x_ref[0, rows, :]
        y = x * lax.rsqrt(jnp.mean(x * x, axis=-1, keepdims=True) + 1e-6) * gain_ref[...]
        row = i * tm + part * th + lax.broadcasted_iota(jnp.int32, (th, 1), 0)
        isctx = row < n_ctx
        shift = jnp.where(isctx, mc[0:1], ml[0:1])
        scale = jnp.where(isctx, mc[1:2], ml[1:2])
        h = (y * (1.0 + scale) + shift).astype(BF16)

        cd, sd = cd_ref[rows, :], sd_ref[rows, :]
        cm, sm = cm_ref[rows, :], sm_ref[rows, :]

        aq = _dot(h, w_ref[:, 0:512])
        for g in range(4):
            sl = slice(g * LANES, (g + 1) * LANES)
            oq_ref[0, rows, sl] = (_rope128(aq[:, sl], cd, sd, False) * da_scale).astype(BF16)
        ak = _dot(h, w_ref[:, 512:1024])
        for g in range(4):
            sl = slice(g * LANES, (g + 1) * LANES)
            ok_ref[0, rows, sl] = _rope128(ak[:, sl], cd, sd, False).astype(BF16)
        ov_ref[0, rows, :] = _dot(h, w_ref[:, 1024:1536]).astype(BF16)

        cc = _dot(h, w_ref[:, 1536:2048])
        cq = cc[:, 0:MLA_Q_RANK]
        qn = (cq * lax.rsqrt(jnp.mean(cq * cq, axis=-1, keepdims=True) + 1e-6) * qn_ref[...]).astype(BF16)
        qu = _dot(qn, wuq_ref[...])
        for g in range(4):
            sl = slice(g * LANES, (g + 1) * LANES)
            mq_ref[0, rows, sl] = (_rope128(qu[:, sl], cm, sm, True) * mla_scale).astype(BF16)
        ckv = cc[:, MLA_Q_RANK:MLA_Q_RANK + MLA_KV_RANK]
        rest = cc[:, MLA_Q_RANK + MLA_KV_RANK:]
        kvn = ckv * lax.rsqrt(jnp.mean(ckv * ckv, axis=-1, keepdims=True) + 1e-6) * kvn_ref[...]
        t2 = jnp.concatenate([kvn, rest], axis=1).astype(BF16)
        kv = _dot(t2, wkv_ref[...])
        for g in range(4):
            sl = slice(g * LANES, (g + 1) * LANES)
            mk_ref[0, rows, sl] = _rope128(kv[:, sl], cm, sm, True).astype(BF16)
        mv_ref[0, rows, :] = kv[:, 512:768].astype(BF16)

        hg_ref[0, rows, :] = _dot(h, w_ref[:, 2048:3328])


def _in_call(xa, gain, mods, w_in_p, tabs, qn, kvn, wuq_p, wkv_p, n_ctx):
    b, t, d = xa.shape
    tm = _row_tile(t)
    cd, sd, cm, sm = tabs
    row_blk = lambda width: pl.BlockSpec((1, tm, width), lambda bi, i: (bi, i, 0))
    tab_blk = pl.BlockSpec((tm, LANES), lambda bi, i: (i, 0))
    out_w = (512, 512, 512, 512, 512, 256)
    return pl.pallas_call(
        functools.partial(_in_kernel, tm=tm, n_ctx=n_ctx),
        grid=(b, t // tm),
        in_specs=[
            row_blk(d),
            _resident((1, d)),
            pl.BlockSpec((1, 6, d), lambda bi, i: (bi, 0, 0)),
            pl.BlockSpec((1, 6, d), lambda bi, i: (b, 0, 0)),
            _resident(w_in_p.shape),
            tab_blk, tab_blk, tab_blk, tab_blk,
            _resident((1, MLA_Q_RANK)),
            _resident((1, MLA_KV_RANK)),
            _resident(wuq_p.shape),
            _resident(wkv_p.shape),
        ],
        out_specs=[row_blk(w) for w in out_w] + [row_blk(5 * HG_WIDTH)],
        out_shape=[jax.ShapeDtypeStruct((b, t, w), BF16) for w in out_w]
        + [jax.ShapeDtypeStruct((b, t, 5 * HG_WIDTH), F32)],
        compiler_params=_cparams(("parallel", "parallel")),
        name="in_proj",
    )(xa, gain, mods, mods, w_in_p, cd, sd, cm, sm, qn, kvn, wuq_p, wkv_p)


def _softmax_parts(s):
    m = jnp.max(s, axis=-1, keepdims=True)
    p = jnp.exp2(s - m)
    return p.astype(BF16), 1.0 / jnp.sum(p, axis=-1, keepdims=True)


def _attend_tile(attend, qi, q_ref, k_ref, v_ref, o_ref, tq, n_ctx):
    n_sub = tq // ATT_SUB
    ctx_sub = n_ctx // ATT_SUB

    def tile(first_latent):
        for j in range(n_sub):
            rows = slice(j * ATT_SUB, (j + 1) * ATT_SUB)
            if j < first_latent:
                o_ref[0, rows, :] = attend(q_ref[0, rows, :], k_ref[0, 0:n_ctx, :], v_ref[0, 0:n_ctx, :])
            else:
                o_ref[0, rows, :] = attend(q_ref[0, rows, :], k_ref[0], v_ref[0])

    @pl.when(qi == 0)
    def _():
        tile(ctx_sub)

    @pl.when(qi != 0)
    def _():
        tile(0)


def _da_kernel(q_ref, k_ref, v_ref, lamv_ref, lami_ref, gain_ref, o_ref, *, tq, n_ctx):
    qi = pl.program_id(2)
    lane = lax.broadcasted_iota(jnp.int32, (1, LANES), 1)
    map1 = jnp.where(lane < DA_HEAD_DIM, 1.0, 0.0).astype(BF16)
    map2 = jnp.where(lane < DA_HEAD_DIM, 0.0, 1.0).astype(BF16)
    lv = lamv_ref[...]
    lam = (jnp.exp(jnp.sum(lv[0:1] * lv[1:2], axis=-1, keepdims=True))
           - jnp.exp(jnp.sum(lv[2:3] * lv[3:4], axis=-1, keepdims=True)) + lami_ref[...])
    gain = gain_ref[...]

    def attend_head(q, k, v):
        p1, r1 = _softmax_parts(_dot_nt(q * map1, k))
        p2, r2 = _softmax_parts(_dot_nt(q * map2, k))
        o = _dot(p1, v) * r1 - _dot(p2, v) * (lam * r2)
        return (o * lax.rsqrt(jnp.mean(o * o, axis=-1, keepdims=True) + 1e-5) * gain).astype(BF16)

    def attend(q, k, v):
        heads = [slice(h * LANES, (h + 1) * LANES) for h in range(DA_HEADS_PER_STEP)]
        return jnp.concatenate([attend_head(q[:, sl], k[:, sl], v[:, sl]) for sl in heads], axis=1)

    _attend_tile(attend, qi, q_ref, k_ref, v_ref, o_ref, tq, n_ctx)


def _da_call(q, k, v, lam_vec, lam_init, gain_eff, n_ctx):
    b, t, _ = q.shape
    tq = _row_tile(t)
    wb = DA_HEADS_PER_STEP * LANES
    return pl.pallas_call(
        functools.partial(_da_kernel, tq=tq, n_ctx=n_ctx),
        grid=(b, DA_HEADS // DA_HEADS_PER_STEP, t // tq),
        in_specs=[
            pl.BlockSpec((1, tq, wb), lambda bi, h, i: (bi, i, h)),
            pl.BlockSpec((1, t, wb), lambda bi, h, i: (bi, 0, h)),
            pl.BlockSpec((1, t, wb), lambda bi, h, i: (bi, 0, h)),
            pl.BlockSpec((4, DA_HEAD_DIM), lambda bi, h, i: (0, 0)),
            pl.BlockSpec((1, 1), lambda bi, h, i: (0, 0)),
            pl.BlockSpec((1, LANES), lambda bi, h, i: (0, 0)),
        ],
        out_specs=pl.BlockSpec((1, tq, wb), lambda bi, h, i: (bi, i, h)),
        out_shape=jax.ShapeDtypeStruct((b, t, DA_HEADS * LANES), BF16),
        compiler_params=_cparams(("parallel", "parallel", "parallel")),
        name="diff_attention",
    )(q, k, v, lam_vec, lam_init, gain_eff)


def _mla_kernel(q_ref, k_ref, v_ref, o_ref, *, tq, n_ctx):
    qi = pl.program_id(1)
    lane = lax.broadcasted_iota(jnp.int32, (1, LANES), 1)
    low_half = lane < MLA_V

    def attend(q, k, v):
        pairs = []
        for pr in range(MLA_HEADS // 2):
            vp = v[:, pr * LANES:(pr + 1) * LANES]
            outs = []
            for h in (2 * pr, 2 * pr + 1):
                sl = slice(h * LANES, (h + 1) * LANES)
                p, r = _softmax_parts(_dot_nt(q[:, sl], k[:, sl]))
                outs.append(_dot(p, vp) * r)
            pairs.append(jnp.where(low_half, outs[0], outs[1]))
        return jnp.concatenate(pairs, axis=1).astype(BF16)

    _attend_tile(attend, qi, q_ref, k_ref, v_ref, o_ref, tq, n_ctx)


def _mla_call(q, k, v, n_ctx):
    b, t, _ = q.shape
    tq = _row_tile(t)
    wv = MLA_HEADS * MLA_V
    return pl.pallas_call(
        functools.partial(_mla_kernel, tq=tq, n_ctx=n_ctx),
        grid=(b, t // tq),
        in_specs=[
            pl.BlockSpec((1, tq, MLA_HEADS * LANES), lambda bi, i: (bi, i, 0)),
            pl.BlockSpec((1, t, MLA_HEADS * LANES), lambda bi, i: (bi, 0, 0)),
            pl.BlockSpec((1, t, wv), lambda bi, i: (bi, 0, 0)),
        ],
        out_specs=pl.BlockSpec((1, tq, wv), lambda bi, i: (bi, i, 0)),
        out_shape=jax.ShapeDtypeStruct((b, t, wv), BF16),
        compiler_params=_cparams(("parallel", "parallel")),
        name="latent_attention",
    )(q, k, v)


def _hgrn_kernel(hg_ref, lbf_ref, lbb_ref, gain_ref, o_ref,
                 kf_s, bf_s, kb_s, cb_s, v_s, qtf_s, ktf_s, qtb_s, ktb_s,
                 df_s, db_s, vt_s, oacc_s, stf_s, stb_s,
                 dcf_s, dcb_s, sfa_s, sfb_s, sba_s, sbb_s, *, t_len, n_ctx):
    w = HG_WIDTH
    rt = HG_TILE
    blk = HG_BLOCK
    pad = blk
    n_tiles = t_len // rt
    blk_per_tile = rt // blk

    r_i = lax.broadcasted_iota(jnp.int32, (rt, rt), 0)
    c_i = lax.broadcasted_iota(jnp.int32, (rt, rt), 1)
    same = (r_i // blk) == (c_i // blk)
    incl_lo = jnp.where(same & (c_i <= r_i), 1.0, 0.0).astype(BF16)
    incl_hi = jnp.where(same & (c_i >= r_i), 1.0, 0.0).astype(BF16)
    blk_sum = jnp.where(same, 1.0, 0.0).astype(BF16)
    sel_rows = 16
    g_r = lax.broadcasted_iota(jnp.int32, (sel_rows, rt), 0)
    g_c = lax.broadcasted_iota(jnp.int32, (sel_rows, rt), 1)
    blk_sel = jnp.where((g_c // blk) == g_r, 1.0, 0.0).astype(BF16)
    h_r = lax.broadcasted_iota(jnp.int32, (w, w), 0) // HG_KEY
    h_c = lax.broadcasted_iota(jnp.int32, (w, w), 1) // HG_KEY
    head_ones = jnp.where(h_r == h_c, 1.0, 0.0).astype(BF16)
    head_mask = jnp.where(h_r == h_c, 1.0, 0.0).astype(F32)
    pos = lax.broadcasted_iota(jnp.int32, (rt, 1), 0) % blk

    lbf = lbf_ref[...]
    lbb = lbb_ref[...]

    zpad = jnp.zeros((pad, w), F32)
    for s in (kf_s, bf_s, kb_s, cb_s, v_s):
        s[0:pad, :] = zpad
        s[pad + rt:pad + rt + pad, :] = zpad

    def gates(z, lb):
        f = lb + (1.0 - lb) * _sigmoid(z)
        return (1.0 - lb) * _sigmoid(-z), jnp.log(f)

    ch = HG_CHUNK
    ch_per_tile = rt // ch
    same_c = (r_i // ch) == (c_i // ch)
    lo_c = same_c & (c_i <= r_i)
    hi_c = same_c & (c_i >= r_i)
    cum_lo = jnp.where(lo_c, 1.0, 0.0).astype(BF16)
    cum_hi = jnp.where(hi_c, 1.0, 0.0).astype(BF16)
    ch_sum = jnp.where(same_c, 1.0, 0.0).astype(BF16)
    ch_sel = jnp.where((g_c // ch) == g_r, 1.0, 0.0).astype(BF16)
    head_of_lane = lax.broadcasted_iota(jnp.int32, (1, w), 1) // HG_KEY
    head_cols = [jnp.where(head_of_lane == h, 1.0, 0.0).astype(BF16) for h in range(HG_HEADS)]

    def chunk_tile(ti, worst):
        rows = pl.ds(pl.multiple_of(ti * rt, rt), rt)
        q = hg_ref[0, rows, 0:w] * (HG_KEY ** -0.5)
        v = hg_ref[0, rows, w:2 * w]
        vb = v.astype(BF16)
        od = jnp.zeros((rt, w), F32)
        for col, lb, cum, keep, qt_s, kt_s, dc_s in (
                (2, lbf, cum_lo, lo_c, qtf_s, ktf_s, dcf_s), (3, lbb, cum_hi, hi_c, qtb_s, ktb_s, dcb_s)):
            k, logf = gates(hg_ref[0, rows, col * w:(col + 1) * w], lb)
            l2 = logf * LOG2_E
            b = _dot_exact_lhs(cum, l2)
            tot = _dot_exact_lhs(ch_sum, l2)
            half = 0.5 * tot
            qs = (q * jnp.exp2(b - half)).astype(BF16)
            ks = (k * jnp.exp2(half - b)).astype(BF16)
            qt_s[rows, :] = (q * jnp.exp2(b)).astype(BF16)
            kt_s[rows, :] = (k * jnp.exp2(tot - b)).astype(BF16)
            dc_s[ti] = jnp.exp2(_dot_exact_lhs(ch_sel, l2)[0:8])
            worst = jnp.maximum(worst, jnp.max(-tot, axis=0, keepdims=True))
            for h in range(HG_HEADS):
                p = jnp.where(keep, _dot_nt(qs * head_cols[h], ks), 0.0)
                od = od + _dot(p.astype(BF16), vb * head_cols[h])
        vt_s[:, rows] = v.T.astype(BF16)
        oacc_s[rows, :] = od
        return worst

    worst = lax.fori_loop(0, n_tiles, chunk_tile, jnp.zeros((1, w), F32), unroll=3)
    exact_needed = jnp.max(worst) > HG_CHUNK_RANGE

    half_w = w // 2
    p_r = lax.broadcasted_iota(jnp.int32, (half_w, half_w), 0) // HG_KEY
    p_c = lax.broadcasted_iota(jnp.int32, (half_w, half_w), 1) // HG_KEY
    pair_mask = jnp.where(p_r == p_c, 1.0, 0.0).astype(F32)
    for s in (sfa_s, sfb_s, sba_s, sbb_s):
        s[...] = jnp.zeros((half_w, half_w), F32)
    lane_c = lax.broadcasted_iota(jnp.int32, (1, LANES), 1)

    def advance_chunk(ci, qt_s, kt_s, dc_s, sa_s, sb_s):
        r = pl.multiple_of(ci * ch, ch)
        rows = pl.ds(r, ch)
        sa = sa_s[...]
        sb = sb_s[...]
        oi = jnp.concatenate([_dot_nt(qt_s[rows, 0:half_w], sa.astype(BF16)),
                              _dot_nt(qt_s[rows, half_w:w], sb.astype(BF16))], axis=1)
        oacc_s[rows, :] = oacc_s[rows, :] + oi
        slab = pl.multiple_of((r // LANES) * LANES, LANES)
        off = r - slab
        vt = vt_s[:, pl.ds(slab, LANES)]
        vt = vt * jnp.where((lane_c >= off) & (lane_c < off + ch), 1.0, 0.0).astype(BF16)
        kt = kt_s[pl.ds(slab, LANES), :]
        dec = dc_s[ci // ch_per_tile, pl.ds(ci % ch_per_tile, 1), :]
        sa_s[...] = sa * dec[:, 0:half_w] + _dot(vt[0:half_w, :], kt[:, 0:half_w]) * pair_mask
        sb_s[...] = sb * dec[:, half_w:w] + _dot(vt[half_w:w, :], kt[:, half_w:w]) * pair_mask

    n_ch = t_len // ch
    n_cch = n_ctx // ch

    def chunk_step(i, carry):
        advance_chunk(i, qtf_s, ktf_s, dcf_s, sfa_s, sfb_s)
        jb = jnp.where(i < n_cch, n_cch - 1 - i, n_ch - 1 - (i - n_cch))
        advance_chunk(jb, qtb_s, ktb_s, dcb_s, sba_s, sbb_s)
        return carry

    lax.fori_loop(0, jnp.where(exact_needed, 0, n_ch), chunk_step, 0)

    n_tiles_x = jnp.where(exact_needed, n_tiles, 0)

    def tile_body(ti, carry):
        rows = pl.ds(pl.multiple_of(ti * rt, rt), rt)
        q = hg_ref[0, rows, 0:w] * (HG_KEY ** -0.5)
        v = hg_ref[0, rows, w:2 * w]
        kf, lf = gates(hg_ref[0, rows, 2 * w:3 * w], lbf)
        kb, lg = gates(hg_ref[0, rows, 3 * w:4 * w], lbb)
        b_f = _dot_exact_lhs(incl_lo, lf)
        tot_f = _dot_exact_lhs(blk_sum, lf)
        c_b = _dot_exact_lhs(incl_hi, lg)
        tot_b = _dot_exact_lhs(blk_sum, lg)
        qtf_s[rows, :] = (q * jnp.exp(b_f)).astype(BF16)
        ktf_s[rows, :] = (kf * jnp.exp(tot_f - b_f)).astype(BF16)
        qtb_s[rows, :] = (q * jnp.exp(c_b)).astype(BF16)
        ktb_s[rows, :] = (kb * jnp.exp(tot_b - c_b)).astype(BF16)
        drows = pl.ds(pl.multiple_of(ti * blk_per_tile, blk_per_tile), blk_per_tile)
        df_s[drows, :] = jnp.exp(_dot_exact_lhs(blk_sel, lf)[0:blk_per_tile])
        db_s[drows, :] = jnp.exp(_dot_exact_lhs(blk_sel, lg)[0:blk_per_tile])
        vt_s[:, rows] = v.T.astype(BF16)

        kf_s[pad:pad + rt, :] = kf
        bf_s[pad:pad + rt, :] = b_f
        kb_s[pad:pad + rt, :] = kb
        cb_s[pad:pad + rt, :] = c_b
        v_s[pad:pad + rt, :] = v
        od = jnp.zeros((rt, w), F32)
        for dl in range(blk):
            lo = pad - dl
            e = q * kf_s[lo:lo + rt, :] * jnp.exp(jnp.minimum(b_f - bf_s[lo:lo + rt, :], 0.0))
            e = jnp.where(pos >= dl, e, 0.0)
            od = od + _dot(e.astype(BF16), head_ones) * v_s[lo:lo + rt, :]
            hi = pad + dl
            e = q * kb_s[hi:hi + rt, :] * jnp.exp(jnp.minimum(c_b - cb_s[hi:hi + rt, :], 0.0))
            e = jnp.where(pos + dl < blk, e, 0.0)
            od = od + _dot(e.astype(BF16), head_ones) * v_s[hi:hi + rt, :]
        oacc_s[rows, :] = od
        return carry

    lax.fori_loop(0, n_tiles_x, tile_body, 0)

    stf_s[...] = jnp.zeros((w, w), F32)
    stb_s[...] = jnp.zeros((w, w), F32)
    n_blk = t_len // blk
    n_cblk = n_ctx // blk
    lane = lax.broadcasted_iota(jnp.int32, (1, LANES), 1)

    def advance(bi, qt_s, kt_s, d_s, st_s):
        r = pl.multiple_of(bi * blk, blk)
        st = st_s[...]
        rows = pl.ds(r, blk)
        oacc_s[rows, :] = oacc_s[rows, :] + _dot_nt(qt_s[rows, :], st.astype(BF16))
        slab = pl.multiple_of((r // LANES) * LANES, LANES)
        off = r - slab
        vt = vt_s[:, pl.ds(slab, LANES)]
        vt = vt * jnp.where((lane >= off) & (lane < off + blk), 1.0, 0.0).astype(BF16)
        upd = _dot(vt, kt_s[pl.ds(slab, LANES), :])
        st_s[...] = st * d_s[pl.ds(bi, 1), :] + upd * head_mask

    def step(i, carry):
        advance(i, qtf_s, ktf_s, df_s, stf_s)
        jb = jnp.where(i < n_cblk, n_cblk - 1 - i, n_blk - 1 - (i - n_cblk))
        advance(jb, qtb_s, ktb_s, db_s, stb_s)
        return carry

    lax.fori_loop(0, jnp.where(exact_needed, n_blk, 0), step, 0)

    gain = gain_ref[...]

    def out_body(ti, carry):
        rows = pl.ds(pl.multiple_of(ti * rt, rt), rt)
        o = oacc_s[rows, :]
        ms = _dot_exact_rhs(o * o, head_ones) * (1.0 / HG_KEY)
        y = o * lax.rsqrt(ms + 1e-6) * gain
        g = hg_ref[0, rows, 4 * w:5 * w]
        o_ref[0, rows, :] = (y * (g * _sigmoid(g))).astype(BF16)
        return carry

    lax.fori_loop(0, n_tiles, out_body, 0)


def _hgrn_call(hg, lbf, lbb, gain_t, n_ctx):
    b, t, _ = hg.shape
    w = HG_WIDTH
    tile_pad = pltpu.VMEM((HG_TILE + 2 * HG_BLOCK, w), F32)
    seq_bf16 = pltpu.VMEM((t, w), BF16)
    dec = pltpu.VMEM((t // HG_BLOCK, w), F32)
    return pl.pallas_call(
        functools.partial(_hgrn_kernel, t_len=t, n_ctx=n_ctx),
        grid=(b,),
        in_specs=[
            pl.BlockSpec((1, t, 5 * w), lambda bi: (bi, 0, 0)),
            pl.BlockSpec((1, w), lambda bi: (0, 0)),
            pl.BlockSpec((1, w), lambda bi: (0, 0)),
            pl.BlockSpec((1, w), lambda bi: (0, 0)),
        ],
        out_specs=pl.BlockSpec((1, t, w), lambda bi: (bi, 0, 0)),
        out_shape=jax.ShapeDtypeStruct((b, t, w), BF16),
        scratch_shapes=[tile_pad] * 5 + [seq_bf16] * 4 + [dec, dec,
                        pltpu.VMEM((w, t), BF16), pltpu.VMEM((t, w), F32),
                        pltpu.VMEM((w, w), F32), pltpu.VMEM((w, w), F32)]
        + [pltpu.VMEM((t // HG_TILE, 8, w), F32)] * 2 + [pltpu.VMEM((w // 2, w // 2), F32)] * 4,
        compiler_params=_cparams(("parallel",)),
        name="hgrn2",
    )(hg, lbf, lbb, gain_t)


def _mod_rows(ml_ref, mc_ref, tm, n_ctx, idx):
    i = pl.program_id(1)
    row = i * tm + lax.broadcasted_iota(jnp.int32, (tm, 1), 0)
    isctx = row < n_ctx
    ml = ml_ref[0]
    mc = mc_ref[0]
    return [jnp.where(isctx, mc[k:k + 1], ml[k:k + 1]) for k in idx]


def _out_kernel(oa_ref, ob_ref, og_ref, wo_ref, x_ref, ml_ref, mc_ref, gain_ref, *rest,
                tm, n_ctx, routed, d_ff):
    if routed:
        router_ref, xo_ref, h_ref, gates_ref = rest
    else:
        wg_ref, wu_ref, wd_ref, xo_ref, acc_ref = rest
    gate, shift, scale, ffn_gate = _mod_rows(ml_ref, mc_ref, tm, n_ctx, (2, 3, 4, 5))
    mix = (_dot(oa_ref[0], wo_ref[0:512, :]) + _dot(ob_ref[0], wo_ref[512:768, :])
           + _dot(og_ref[0], wo_ref[768:1024, :]))
    x = x_ref[0] + gate * mix
    y = x * lax.rsqrt(jnp.mean(x * x, axis=-1, keepdims=True) + 1e-6) * gain_ref[...]
    h = y * (1.0 + scale) + shift
    if not routed:
        acc_ref[...] = jnp.zeros_like(acc_ref)
        _swiglu_into(acc_ref, h.astype(BF16), wg_ref, wu_ref, wd_ref, None, d_ff)
        xo_ref[0] = x + ffn_gate * acc_ref[...]
    else:
        xo_ref[0] = x
        h_ref[0] = h
    if routed:
        lane = lax.broadcasted_iota(jnp.int32, (1, LANES), 1)
        logits = _dot_f32(h, router_ref[...])
        logits = jnp.where(lane < N_EXPERTS, logits, -jnp.inf)
        m1 = jnp.max(logits, axis=-1, keepdims=True)
        i1 = jnp.min(jnp.where(logits == m1, lane, LANES), axis=-1, keepdims=True)
        rest_l = jnp.where(lane == i1, -jnp.inf, logits)
        m2 = jnp.max(rest_l, axis=-1, keepdims=True)
        i2 = jnp.min(jnp.where(rest_l == m2, lane, LANES), axis=-1, keepdims=True)
        e2 = jnp.exp(m2 - m1)
        w1 = 1.0 / (1.0 + e2)
        route = jnp.where(lane == 0, i1.astype(F32), 0.0) + jnp.where(lane == 1, i2.astype(F32), 0.0)
        gates_ref[0] = route + jnp.where(lane == 2, w1, 0.0) + jnp.where(lane == 3, e2 * w1, 0.0)


def _out_call(oa, ob, og, wo, xa, mods, gain, n_ctx, router_p=None, ffn=None, skip_rows=0):
    b, t, d = xa.shape
    routed = router_p is not None
    tm = _row_tile(t) if skip_rows == 0 else ATT_SUB
    skip = skip_rows // tm
    t_out = t - skip_rows
    ctx_rows = n_ctx - skip_rows
    in_blk = lambda width: pl.BlockSpec((1, tm, width), lambda bi, i: (bi, i + skip, 0))
    out_blk = lambda width: pl.BlockSpec((1, tm, width), lambda bi, i: (bi, i, 0))
    in_specs = [
        in_blk(oa.shape[-1]), in_blk(ob.shape[-1]), in_blk(og.shape[-1]),
        _resident(wo.shape),
        in_blk(d),
        pl.BlockSpec((1, 6, d), lambda bi, i: (bi, 0, 0)),
        pl.BlockSpec((1, 6, d), lambda bi, i: (b, 0, 0)),
        _resident((1, d)),
    ]
    args = [oa, ob, og, wo, xa, mods, mods, gain]
    scratch = []
    d_ff = 0
    if routed:
        in_specs.append(_resident(router_p.shape))
        args.append(router_p)
        out_specs = [out_blk(d), out_blk(d), out_blk(LANES)]
        out_shape = [jax.ShapeDtypeStruct((b, t_out, d), F32), jax.ShapeDtypeStruct((b, t_out, d), F32),
                     jax.ShapeDtypeStruct((b, t_out, LANES), F32)]
    else:
        wg, wu, wd = ffn
        d_ff = wg.shape[-1]
        in_specs += [_resident((1, d, d_ff)), _resident((1, d, d_ff)), _resident((1, d_ff, d))]
        args += [wg[None], wu[None], wd[None]]
        out_specs = out_blk(d)
        out_shape = jax.ShapeDtypeStruct((b, t_out, d), F32)
        scratch = [pltpu.VMEM((tm, d), F32)]
    return pl.pallas_call(
        functools.partial(_out_kernel, tm=tm, n_ctx=ctx_rows, routed=routed, d_ff=d_ff),
        grid=(b, t_out // tm),
        in_specs=in_specs,
        out_specs=out_specs,
        out_shape=out_shape,
        scratch_shapes=scratch,
        compiler_params=_cparams(("parallel", "parallel")),
        name="out_proj_routed" if routed else "out_proj_ffn",
    )(*args)


def _swiglu_into(acc_ref, h, wg_ref, wu_ref, wd_ref, row_gate, d_ff):
    for c in range(d_ff // FF_CHUNK):
        cs = slice(c * FF_CHUNK, (c + 1) * FF_CHUNK)
        g = _dot(h, wg_ref[0, :, cs])
        u = _dot(h, wu_ref[0, :, cs])
        a = g * _sigmoid(g) * u
        if row_gate is not None:
            a = a * row_gate
        acc_ref[...] += _dot(a.astype(BF16), wd_ref[0, cs, :])


def _row_copy(src, src_row, dst, dst_row, sem):
    return pltpu.make_async_copy(src.at[pl.ds(src_row, 1), :], dst.at[pl.ds(dst_row, 1), :], sem)


def _dispatch_kernel(pos_ref, pad_ref, h_ref, xs_ref, zero_s, sem, zsem, *, tm, n_exp, n_tiles):
    step = pl.program_id(0)
    base = step * tm

    def pad_copies(act):
        for e in range(n_exp):
            start = pad_ref[e]

            def tail_row(r, carry):
                act(_row_copy(zero_s, 0, xs_ref, start + r, zsem))
                return carry

            lax.fori_loop(0, pad_ref[n_exp + e], tail_row, 0)
        for j in range(n_exp):
            tile = pad_ref[2 * n_exp] + j

            @pl.when(tile < n_tiles)
            def _():
                row0 = pl.multiple_of(tile * tm, tm)
                act(pltpu.make_async_copy(zero_s, xs_ref.at[pl.ds(row0, tm), :], zsem))

    @pl.when(step == 0)
    def _():
        zero_s[...] = jnp.zeros_like(zero_s)
        pad_copies(lambda c: c.start())

    def issue(r, carry):
        for s in range(TOP_K):
            _row_copy(h_ref, r, xs_ref, pos_ref[TOP_K * (base + r) + s], sem).start()
        return carry

    lax.fori_loop(0, tm, issue, 0, unroll=8)

    def drain(r, carry):
        for s in range(TOP_K):
            _row_copy(h_ref, r, xs_ref, pos_ref[TOP_K * (base + r) + s], sem).wait()
        return carry

    lax.fori_loop(0, tm, drain, 0, unroll=8)

    @pl.when(step == 0)
    def _():
        pad_copies(lambda c: c.wait())


def _dispatch_call(pos, pad_info, h2d, n_tiles, n_exp):
    n, d = h2d.shape
    tm = MOE_TILE
    return pl.pallas_call(
        functools.partial(_dispatch_kernel, tm=tm, n_exp=n_exp, n_tiles=n_tiles),
        grid_spec=pltpu.PrefetchScalarGridSpec(
            num_scalar_prefetch=2,
            grid=(n // tm,),
            in_specs=[pl.BlockSpec((tm, d), lambda i, pos, pad: (i, 0))],
            out_specs=pl.BlockSpec(memory_space=pl.ANY),
            scratch_shapes=[pltpu.VMEM((tm, d), F32), pltpu.SemaphoreType.DMA(()),
                            pltpu.SemaphoreType.DMA(())],
        ),
        out_shape=jax.ShapeDtypeStruct((n_tiles * tm, d), F32),
        compiler_params=_cparams(("arbitrary",)),
        name="moe_dispatch",
    )(pos, pad_info, h2d)


def _expert_kernel(te_ref, nv_ref, xs_ref, wg_ref, wu_ref, wd_ref, ys_ref, acc_ref, *, tm, d_ff):
    n_valid = nv_ref[pl.program_id(0)]

    @pl.when(n_valid > 0)
    def _():
        acc_ref[...] = jnp.zeros_like(acc_ref)
        _swiglu_into(acc_ref, xs_ref[...].astype(BF16), wg_ref, wu_ref, wd_ref, None, d_ff)
        ys_ref[...] = acc_ref[...]

    @pl.when(n_valid == 0)
    def _():
        ys_ref[...] = jnp.zeros_like(ys_ref)


def _expert_call(tile_expert, tile_valid, xs, wg, wu, wd):
    n_slots, d = xs.shape
    d_ff = wg.shape[-1]
    tm = MOE_TILE
    w_blk = lambda shape: pl.BlockSpec(shape, lambda t, te, nv: (te[t], 0, 0))
    return pl.pallas_call(
        functools.partial(_expert_kernel, tm=tm, d_ff=d_ff),
        grid_spec=pltpu.PrefetchScalarGridSpec(
            num_scalar_prefetch=2,
            grid=(n_slots // tm,),
            in_specs=[
                pl.BlockSpec((tm, d), lambda t, te, nv: (t, 0)),
                w_blk((1, d, d_ff)), w_blk((1, d, d_ff)), w_blk((1, d_ff, d)),
            ],
            out_specs=pl.BlockSpec((tm, d), lambda t, te, nv: (t, 0)),
            scratch_shapes=[pltpu.VMEM((tm, d), F32)],
        ),
        out_shape=jax.ShapeDtypeStruct((n_slots, d), F32),
        compiler_params=_cparams(("arbitrary",)),
        name="moe_experts",
    )(tile_expert, tile_valid, xs, wg, wu, wd)


def _combine_kernel(pos_ref, ys_ref, x_ref, route_ref, ml_ref, mc_ref, fgain_ref, xo_ref, y1_s, y2_s, sem,
                    *, tm, n_ctx, final_norm):
    base = (pl.program_id(0) * pl.num_programs(1) + pl.program_id(1)) * tm
    bufs = (y1_s, y2_s)

    def issue(r, carry):
        for s in range(TOP_K):
            _row_copy(ys_ref, pos_ref[TOP_K * (base + r) + s], bufs[s], r, sem).start()
        return carry

    lax.fori_loop(0, tm, issue, 0, unroll=8)

    def drain(r, carry):
        for s in range(TOP_K):
            _row_copy(ys_ref, pos_ref[TOP_K * (base + r) + s], bufs[s], r, sem).wait()
        return carry

    lax.fori_loop(0, tm, drain, 0, unroll=8)

    lane = lax.broadcasted_iota(jnp.int32, (1, LANES), 1)
    route = route_ref[0]
    w1 = jnp.sum(jnp.where(lane == 2, route, 0.0), axis=-1, keepdims=True)
    w2 = jnp.sum(jnp.where(lane == 3, route, 0.0), axis=-1, keepdims=True)
    (gate,) = _mod_rows(ml_ref, mc_ref, tm, n_ctx, (5,))
    x = x_ref[0] + gate * (w1 * y1_s[...] + w2 * y2_s[...])
    if final_norm:
        x = x * lax.rsqrt(jnp.mean(x * x, axis=-1, keepdims=True) + 1e-6) * fgain_ref[...]
    xo_ref[0] = x


def _combine_call(pos, ys, xa, route, mods, n_ctx, final_gain, final_norm):
    b, t, d = xa.shape
    tm = _row_tile(t)
    row_blk = lambda width: pl.BlockSpec((1, tm, width), lambda bi, i, pos: (bi, i, 0))
    return pl.pallas_call(
        functools.partial(_combine_kernel, tm=tm, n_ctx=n_ctx, final_norm=final_norm),
        grid_spec=pltpu.PrefetchScalarGridSpec(
            num_scalar_prefetch=1,
            grid=(b, t // tm),
            in_specs=[
                pl.BlockSpec(memory_space=pl.ANY),
                row_blk(d), row_blk(LANES),
                pl.BlockSpec((1, 6, d), lambda bi, i, pos: (bi, 0, 0)),
                pl.BlockSpec((1, 6, d), lambda bi, i, pos: (b, 0, 0)),
                pl.BlockSpec((1, d), lambda bi, i, pos: (0, 0)),
            ],
            out_specs=row_blk(d),
            scratch_shapes=[pltpu.VMEM((tm, d), F32), pltpu.VMEM((tm, d), F32),
                            pltpu.SemaphoreType.DMA(())],
        ),
        out_shape=jax.ShapeDtypeStruct((b, t, d), F32),
        compiler_params=_cparams(("arbitrary", "arbitrary")),
        name="moe_combine",
    )(pos, ys, xa, route, mods, mods, final_gain)


def _moe_call(h, xa, route, mods, wg, wu, wd, n_ctx, final_gain, final_norm):
    b, t, d = xa.shape
    n = b * t
    tm = MOE_TILE
    n_exp = wg.shape[0]
    n_tiles = (TOP_K * n) // tm + n_exp
    choice = route.reshape(n, LANES)[:, :TOP_K].astype(jnp.int32).reshape(-1)
    onehot = (choice[:, None] == jnp.arange(n_exp, dtype=jnp.int32)[None, :]).astype(jnp.int32)
    csum = jnp.cumsum(onehot, axis=0)
    counts = csum[-1]
    rank = jnp.sum((csum - onehot) * onehot, axis=1)
    group_tiles = (counts + tm - 1) // tm
    tile_end = jnp.cumsum(group_tiles)
    group_row0 = (tile_end - group_tiles) * tm
    pos = jnp.sum(group_row0[None, :] * onehot, axis=1) + rank
    tile_id = jnp.arange(n_tiles, dtype=jnp.int32)
    tile_group = jnp.sum((tile_id[:, None] >= tile_end[None, :]).astype(jnp.int32), axis=1)
    last_group = jnp.max(jnp.where(counts > 0, jnp.arange(n_exp, dtype=jnp.int32), 0))
    tile_expert = jnp.minimum(tile_group, last_group)
    in_group = tile_group < n_exp
    g = jnp.minimum(tile_group, n_exp - 1)
    tile_valid = jnp.where(in_group, jnp.clip(counts[g] - (tile_id * tm - group_row0[g]), 0, tm), 0)

    pad_info = jnp.concatenate([group_row0 + counts, group_tiles * tm - counts, tile_end[-1:]])
    xs = _dispatch_call(pos, pad_info.astype(jnp.int32), h.reshape(n, d), n_tiles, n_exp)
    ys = _expert_call(tile_expert.astype(jnp.int32), tile_valid.astype(jnp.int32), xs, wg, wu, wd)
    return _combine_call(pos, ys, xa, route, mods, n_ctx, final_gain, final_norm)


def _final_kernel(x_ref, g_ref, o_ref):
    x = x_ref[0]
    o_ref[0] = x * lax.rsqrt(jnp.mean(x * x, axis=-1, keepdims=True) + 1e-6) * g_ref[...]


def _final_call(xa, gain, n_ctx):
    b, t, d = xa.shape
    tm = 256
    skip = n_ctx // tm
    return pl.pallas_call(
        _final_kernel,
        grid=(b, (t - n_ctx) // tm),
        in_specs=[pl.BlockSpec((1, tm, d), lambda bi, i: (bi, i + skip, 0)), _resident((1, d))],
        out_specs=pl.BlockSpec((1, tm, d), lambda bi, i: (bi, i, 0)),
        out_shape=jax.ShapeDtypeStruct((b, t - n_ctx, d), F32),
        compiler_params=_cparams(("parallel", "parallel")),
        name="final_norm",
    )(xa, gain)


def _rope_tables(seq, n_ctx):
    rows = seq // GRID_W
    row, col = jnp.meshgrid(jnp.arange(rows, dtype=F32), jnp.arange(GRID_W, dtype=F32), indexing="ij")
    row, col = row.reshape(-1), col.reshape(-1)

    def angles(rot_dim):
        n_freq = rot_dim // 4
        inv = ROPE_BASE ** (-jnp.arange(n_freq, dtype=F32) / n_freq)
        ang = jnp.concatenate([row[:, None] * inv, col[:, None] * inv], axis=-1)
        return jnp.cos(ang), jnp.sin(ang)

    def with_ctx(a, fill):
        return jnp.concatenate([jnp.full((n_ctx, a.shape[1]), fill, F32), a], axis=0)

    cos, sin = angles(DA_HEAD_DIM)
    cos_da = with_ctx(jnp.tile(cos, (1, 4)), 1.0)
    sin_da = with_ctx(jnp.tile(jnp.concatenate([-sin, sin], axis=1), (1, 2)), 0.0)
    cos, sin = angles(MLA_ROPE)
    ones = jnp.ones((seq, MLA_NOPE), F32)
    tail = jnp.ones((seq, LANES - MLA_NOPE - MLA_ROPE), F32)
    cos_m = with_ctx(jnp.concatenate([ones, cos, cos, tail], axis=1), 1.0)
    sin_m = with_ctx(jnp.concatenate([0 * ones, -sin, sin, 0 * tail], axis=1), 0.0)
    return cos_da, sin_da, cos_m, sin_m


def _pad_in_proj(w_in_l):
    d = w_in_l.shape[0]
    cut = 1536 + MLA_Q_RANK + MLA_KV_RANK + MLA_ROPE
    padw = 2048 - cut
    return jnp.concatenate([w_in_l[:, :cut], jnp.zeros((d, padw), w_in_l.dtype), w_in_l[:, cut:]],
                           axis=1).astype(BF16)


def _pad_uq(w_uq_l):
    qk = MLA_NOPE + MLA_ROPE
    w = w_uq_l.reshape(MLA_Q_RANK, MLA_HEADS, qk)
    w = jnp.pad(w, ((0, 0), (0, 0), (0, LANES - qk)))
    return w.reshape(MLA_Q_RANK, MLA_HEADS * LANES).astype(BF16)


def _pad_ukv(w_ukv_l):
    w = w_ukv_l.reshape(MLA_KV_RANK, MLA_HEADS, MLA_NOPE + MLA_V)
    k_part = jnp.pad(w[:, :, :MLA_NOPE], ((0, 0), (0, 0), (0, LANES - MLA_NOPE)))
    k_part = k_part.reshape(MLA_KV_RANK, MLA_HEADS * LANES)
    v_part = w[:, :, MLA_NOPE:].reshape(MLA_KV_RANK, MLA_HEADS * MLA_V)
    place = jnp.zeros((MLA_ROPE, MLA_HEADS, LANES), F32)
    eye = jnp.eye(MLA_ROPE, dtype=F32)
    place = place.at[:, :, MLA_NOPE:MLA_NOPE + MLA_ROPE].set(eye[:, None, :])
    place = place.reshape(MLA_ROPE, MLA_HEADS * LANES)
    top = jnp.concatenate([k_part, v_part], axis=1)
    mid = jnp.concatenate([place, jnp.zeros((MLA_ROPE, MLA_HEADS * MLA_V), F32)], axis=1)
    bot = jnp.zeros((MXU_COLS - MLA_KV_RANK - MLA_ROPE, top.shape[1]), F32)
    return jnp.concatenate([top, mid, bot], axis=0).astype(BF16)


def kernel(x, c, ctx, c_ctx, w_mod, b_mod, norm_mix, norm_ffn, norm_final, w_in, w_out, diff_lambda, diff_norm, mla_q_norm, mla_w_uq, mla_kv_norm, mla_w_ukv, hgrn_lb_logits, hgrn_norm, ffn_w_gate, ffn_w_up, ffn_w_down, moe_router, moe_w_gate, moe_w_up, moe_w_down):
    b, seq, d = x.shape
    n_ctx = ctx.shape[1]
    depth = w_mod.shape[0]
    assert d == 1024 and n_ctx % ATT_SUB == 0 and seq % ATT_SUB == 0

    xa = jnp.concatenate([ctx, x], axis=1)
    bp = -(-(b + 1) // 8) * 8
    c_all = jnp.concatenate([c, c_ctx[None, :], jnp.zeros((bp - b - 1, d), F32)], axis=0)
    mods = _mod_call(c_all, w_mod, b_mod).reshape(depth, bp, 6, d)

    lb = jnp.cumsum(jax.nn.softmax(hgrn_lb_logits.astype(F32), axis=1), axis=1)
    lb = lb - lb[:, :1]
    tabs = _rope_tables(seq, n_ctx)

    for l in range(depth):
        lam_init = 0.8 - 0.6 * math.exp(-0.3 * l)
        q_da, k_da, v_da, q_m, k_m, v_m, hg = _in_call(
            xa, norm_mix[l][None], mods[l], _pad_in_proj(w_in[l]), tabs,
            mla_q_norm[l][None], mla_kv_norm[l][None], _pad_uq(mla_w_uq[l]), _pad_ukv(mla_w_ukv[l]),
            n_ctx)
        oa = _da_call(q_da, k_da, v_da, diff_lambda[l], jnp.full((1, 1), lam_init, F32),
                      (diff_norm[l] * (1.0 - lam_init))[None], n_ctx)
        ob = _mla_call(q_m, k_m, v_m, n_ctx)
        og = _hgrn_call(hg, lb[0, l][None], lb[1, l][None], jnp.tile(hgrn_norm[l], HG_HEADS)[None], n_ctx)

        wo = w_out[l].astype(BF16)
        skip = n_ctx if l == depth - 1 else 0
        if l % 2 == 0:
            ffn = (ffn_w_gate[l // 2].astype(BF16), ffn_w_up[l // 2].astype(BF16),
                   ffn_w_down[l // 2].astype(BF16))
            xa = _out_call(oa, ob, og, wo, xa, mods[l], norm_ffn[l][None], n_ctx, ffn=ffn, skip_rows=skip)
        else:
            router_p = jnp.pad(moe_router[l // 2], ((0, 0), (0, LANES - N_EXPERTS)))
            xa, h, route = _out_call(oa, ob, og, wo, xa, mods[l], norm_ffn[l][None], n_ctx,
                                     router_p=router_p, skip_rows=skip)
            xa = _moe_call(h, xa, route, mods[l], moe_w_gate[l // 2].astype(BF16),
                           moe_w_up[l // 2].astype(BF16), moe_w_down[l // 2].astype(BF16), n_ctx - skip,
                           norm_final[None], l == depth - 1)
    if depth % 2 == 0:
        return xa
    return _final_call(xa, norm_final[None], 0)
```

```python
import functools
import math

import jax
import jax.numpy as jnp
from jax import lax
from jax.experimental import pallas as pl
from jax.experimental.pallas import tpu as pltpu

F32 = jnp.float32
BF16 = jnp.bfloat16

GRID_W = 64
ROPE_BASE = 10000.0
DA_HEADS = 4
DA_HEAD_DIM = 64
MLA_HEADS = 4
MLA_Q_RANK = 256
MLA_KV_RANK = 128
MLA_NOPE = 64
MLA_ROPE = 32
MLA_V = 64
HG_HEADS = 4
HG_KEY = 64
HG_WIDTH = HG_HEADS * HG_KEY
N_EXPERTS = 8
TOP_K = 2
LOG2_E = 1.4426950408889634

LANES = 128
SUBLANES = 8
MXU_COLS = 256
VMEM_LIMIT = 56 * 1024 * 1024

HG_BLOCK = 16
HG_CHUNK = 64
HG_CHUNK_RANGE = 180.0
HG_TILE = 128
ATT_SUB = 256
IN_SPLIT = 1
DA_HEADS_PER_STEP = 2
FF_CHUNK = 256
MOE_TILE = 512


def _dot(a, b):
    return jnp.dot(a, b, preferred_element_type=F32)


def _dot_nt(a, b):
    return lax.dot_general(a, b, (((1,), (1,)), ((), ())), preferred_element_type=F32)


def _split3(x):
    x0 = x.astype(BF16)
    r = x - x0.astype(F32)
    x1 = r.astype(BF16)
    x2 = (r - x1.astype(F32)).astype(BF16)
    return x0, x1, x2


def _dot_exact_lhs(a_bf16, x):
    x0, x1, x2 = _split3(x)
    return _dot(a_bf16, x0) + _dot(a_bf16, x1) + _dot(a_bf16, x2)


def _dot_exact_rhs(x, b_bf16):
    x0, x1, x2 = _split3(x)
    return _dot(x0, b_bf16) + _dot(x1, b_bf16) + _dot(x2, b_bf16)


def _dot_f32(a, b):
    a0 = a.astype(BF16)
    a1 = (a - a0.astype(F32)).astype(BF16)
    b0 = b.astype(BF16)
    b1 = (b - b0.astype(F32)).astype(BF16)
    return _dot(a0, b0) + _dot(a0, b1) + _dot(a1, b0)


def _sigmoid(z):
    return 1.0 / (1.0 + jnp.exp(-z))


def _row_tile(t):
    return next(tm for tm in (768, 512, 256) if t % tm == 0)


def _cparams(sem):
    return pltpu.CompilerParams(dimension_semantics=sem, vmem_limit_bytes=VMEM_LIMIT)


def _resident(shape):
    nd = len(shape)
    return pl.BlockSpec(shape, lambda *_: (0,) * nd, pipeline_mode=pl.Buffered(1))


def _mod_kernel(c_ref, w_ref, b_ref, o_ref):
    c = c_ref[...]
    o_ref[0] = _dot_f32(c * _sigmoid(c), w_ref[0]) + b_ref[0]


def _mod_call(c_all, w_mod, b_mod):
    depth, d, n6 = w_mod.shape
    bp = c_all.shape[0]
    tn = 1536
    return pl.pallas_call(
        _mod_kernel,
        grid=(depth, n6 // tn),
        in_specs=[
            pl.BlockSpec((bp, d), lambda l, j: (0, 0)),
            pl.BlockSpec((1, d, tn), lambda l, j: (l, 0, j)),
            pl.BlockSpec((1, 1, tn), lambda l, j: (l, 0, j)),
        ],
        out_specs=pl.BlockSpec((1, bp, tn), lambda l, j: (l, 0, j)),
        out_shape=jax.ShapeDtypeStruct((depth, bp, n6), F32),
        compiler_params=_cparams(("parallel", "parallel")),
        name="modulation",
    )(c_all, w_mod, b_mod.reshape(depth, 1, n6))


def _rope128(seg, cos, sin, mla):
    lane = lax.broadcasted_iota(jnp.int32, (1, LANES), 1)
    if mla:
        half = MLA_ROPE // 2
        take_next = lane < MLA_NOPE + half
    else:
        half = DA_HEAD_DIM // 2
        take_next = (lane % DA_HEAD_DIM) < half
    sw = jnp.where(take_next, pltpu.roll(seg, LANES - half, 1), pltpu.roll(seg, half, 1))
    return seg * cos + sw * sin


def _in_kernel(x_ref, gain_ref, ml_ref, mc_ref, w_ref, cd_ref, sd_ref, cm_ref, sm_ref,
               qn_ref, kvn_ref, wuq_ref, wkv_ref,
               oq_ref, ok_ref, ov_ref, mq_ref, mk_ref, mv_ref, hg_ref, *, tm, n_ctx):
    i = pl.program_id(1)
    ml = ml_ref[0]
    mc = mc_ref[0]
    da_scale = DA_HEAD_DIM ** -0.5 * LOG2_E
    mla_scale = (MLA_NOPE + MLA_ROPE) ** -0.5 * LOG2_E
    th = tm // IN_SPLIT

    for part in range(IN_SPLIT):
        rows = slice(part * th, (part + 1) * th)
        x = x_ref[0, rows, :]
        y = x * lax.rsqrt(jnp.mean(x * x, axis=-1, keepdims=True) + 1e-6) * gain_ref[...]
        row = i * tm + part * th + lax.broadcasted_iota(jnp.int32, (th, 1), 0)
        isctx = row < n_ctx
        shift = jnp.where(isctx, mc[0:1], ml[0:1])
        scale = jnp.where(isctx, mc[1:2], ml[1:2])
        h = (y * (1.0 + scale) + shift).astype(BF16)

        cd, sd = cd_ref[rows, :], sd_ref[rows, :]
        cm, sm = cm_ref[rows, :], sm_ref[rows, :]

        aq = _dot(h, w_ref[:, 0:512])
        for g in range(4):
            sl = slice(g * LANES, (g + 1) * LANES)
            oq_ref[0, rows, sl] = (_rope128(aq[:, sl], cd, sd, False) * da_scale).astype(BF16)
        ak = _dot(h, w_ref[:, 512:1024])
        for g in range(4):
            sl = slice(g * LANES, (g + 1) * LANES)
            ok_ref[0, rows, sl] = _rope128(ak[:, sl], cd, sd, False).astype(BF16)
        ov_ref[0, rows, :] = _dot(h, w_ref[:, 1024:1536]).astype(BF16)

        cc = _dot(h, w_ref[:, 1536:2048])
        cq = cc[:, 0:MLA_Q_RANK]
        qn = (cq * lax.rsqrt(jnp.mean(cq * cq, axis=-1, keepdims=True) + 1e-6) * qn_ref[...]).astype(BF16)
        qu = _dot(qn, wuq_ref[...])
        for g in range(4):
            sl = slice(g * LANES, (g + 1) * LANES)
            mq_ref[0, rows, sl] = (_rope128(qu[:, sl], cm, sm, True) * mla_scale).astype(BF16)
        ckv = cc[:, MLA_Q_RANK:MLA_Q_RANK + MLA_KV_RANK]
        rest = cc[:, MLA_Q_RANK + MLA_KV_RANK:]
        kvn = ckv * lax.rsqrt(jnp.mean(ckv * ckv, axis=-1, keepdims=True) + 1e-6) * kvn_ref[...]
        t2 = jnp.concatenate([kvn, rest], axis=1).astype(BF16)
        kv = _dot(t2, wkv_ref[...])
        for g in range(4):
            sl = slice(g * LANES, (g + 1) * LANES)
            mk_ref[0, rows, sl] = _rope128(kv[:, sl], cm, sm, True).astype(BF16)
        mv_ref[0, rows, :] = kv[:, 512:768].astype(BF16)

        hg_ref[0, rows, :] = _dot(h, w_ref[:, 2048:3328])


def _in_call(xa, gain, mods, w_in_p, tabs, qn, kvn, wuq_p, wkv_p, n_ctx):
    b, t, d = xa.shape
    tm = _row_tile(t)
    cd, sd, cm, sm = tabs
    row_blk = lambda width: pl.BlockSpec((1, tm, width), lambda bi, i: (bi, i, 0))
    tab_blk = pl.BlockSpec((tm, LANES), lambda bi, i: (i, 0))
    out_w = (512, 512, 512, 512, 512, 256)
    return pl.pallas_call(
        functools.partial(_in_kernel, tm=tm, n_ctx=n_ctx),
        grid=(b, t // tm),
        in_specs=[
            row_blk(d),
            _resident((1, d)),
            pl.BlockSpec((1, 6, d), lambda bi, i: (bi, 0, 0)),
            pl.BlockSpec((1, 6, d), lambda bi, i: (b, 0, 0)),
            _resident(w_in_p.shape),
            tab_blk, tab_blk, tab_blk, tab_blk,
            _resident((1, MLA_Q_RANK)),
            _resident((1, MLA_KV_RANK)),
            _resident(wuq_p.shape),
            _resident(wkv_p.shape),
        ],
        out_specs=[row_blk(w) for w in out_w] + [row_blk(5 * HG_WIDTH)],
        out_shape=[jax.ShapeDtypeStruct((b, t, w), BF16) for w in out_w]
        + [jax.ShapeDtypeStruct((b, t, 5 * HG_WIDTH), F32)],
        compiler_params=_cparams(("parallel", "parallel")),
        name="in_proj",
    )(xa, gain, mods, mods, w_in_p, cd, sd, cm, sm, qn, kvn, wuq_p, wkv_p)


def _softmax_parts(s):
    m = jnp.max(s, axis=-1, keepdims=True)
    p = jnp.exp2(s - m)
    return p.astype(BF16), 1.0 / jnp.sum(p, axis=-1, keepdims=True)


def _attend_tile(attend, qi, q_ref, k_ref, v_ref, o_ref, tq, n_ctx):
    n_sub = tq // ATT_SUB
    ctx_sub = n_ctx // ATT_SUB

    def tile(first_latent):
        for j in range(n_sub):
            rows = slice(j * ATT_SUB, (j + 1) * ATT_SUB)
            if j < first_latent:
                o_ref[0, rows, :] = attend(q_ref[0, rows, :], k_ref[0, 0:n_ctx, :], v_ref[0, 0:n_ctx, :])
            else:
                o_ref[0, rows, :] = attend(q_ref[0, rows, :], k_ref[0], v_ref[0])

    @pl.when(qi == 0)
    def _():
        tile(ctx_sub)

    @pl.when(qi != 0)
    def _():
        tile(0)


def _da_kernel(q_ref, k_ref, v_ref, lamv_ref, lami_ref, gain_ref, o_ref, *, tq, n_ctx):
    qi = pl.program_id(2)
    lane = lax.broadcasted_iota(jnp.int32, (1, LANES), 1)
    map1 = jnp.where(lane < DA_HEAD_DIM, 1.0, 0.0).astype(BF16)
    map2 = jnp.where(lane < DA_HEAD_DIM, 0.0, 1.0).astype(BF16)
    lv = lamv_ref[...]
    lam = (jnp.exp(jnp.sum(lv[0:1] * lv[1:2], axis=-1, keepdims=True))
           - jnp.exp(jnp.sum(lv[2:3] * lv[3:4], axis=-1, keepdims=True)) + lami_ref[...])
    gain = gain_ref[...]

    def attend_head(q, k, v):
        p1, r1 = _softmax_parts(_dot_nt(q * map1, k))
        p2, r2 = _softmax_parts(_dot_nt(q * map2, k))
        o = _dot(p1, v) * r1 - _dot(p2, v) * (lam * r2)
        return (o * lax.rsqrt(jnp.mean(o * o, axis=-1, keepdims=True) + 1e-5) * gain).astype(BF16)

    def attend(q, k, v):
        heads = [slice(h * LANES, (h + 1) * LANES) for h in range(DA_HEADS_PER_STEP)]
        return jnp.concatenate([attend_head(q[:, sl], k[:, sl], v[:, sl]) for sl in heads], axis=1)

    _attend_tile(attend, qi, q_ref, k_ref, v_ref, o_ref, tq, n_ctx)


def _da_call(q, k, v, lam_vec, lam_init, gain_eff, n_ctx):
    b, t, _ = q.shape
    tq = _row_tile(t)
    wb = DA_HEADS_PER_STEP * LANES
    return pl.pallas_call(
        functools.partial(_da_kernel, tq=tq, n_ctx=n_ctx),
        grid=(b, DA_HEADS // DA_HEADS_PER_STEP, t // tq),
        in_specs=[
            pl.BlockSpec((1, tq, wb), lambda bi, h, i: (bi, i, h)),
            pl.BlockSpec((1, t, wb), lambda bi, h, i: (bi, 0, h)),
            pl.BlockSpec((1, t, wb), lambda bi, h, i: (bi, 0, h)),
            pl.BlockSpec((4, DA_HEAD_DIM), lambda bi, h, i: (0, 0)),
            pl.BlockSpec((1, 1), lambda bi, h, i: (0, 0)),
            pl.BlockSpec((1, LANES), lambda bi, h, i: (0, 0)),
        ],
        out_specs=pl.BlockSpec((1, tq, wb), lambda bi, h, i: (bi, i, h)),
        out_shape=jax.ShapeDtypeStruct((b, t, DA_HEADS * LANES), BF16),
        compiler_params=_cparams(("parallel", "parallel", "parallel")),
        name="diff_attention",
    )(q, k, v, lam_vec, lam_init, gain_eff)


def _mla_kernel(q_ref, k_ref, v_ref, o_ref, *, tq, n_ctx):
    qi = pl.program_id(1)
    lane = lax.broadcasted_iota(jnp.int32, (1, LANES), 1)
    low_half = lane < MLA_V

    def attend(q, k, v):
        pairs = []
        for pr in range(MLA_HEADS // 2):
            vp = v[:, pr * LANES:(pr + 1) * LANES]
            outs = []
            for h in (2 * pr, 2 * pr + 1):
                sl = slice(h * LANES, (h + 1) * LANES)
                p, r = _softmax_parts(_dot_nt(q[:, sl], k[:, sl]))
                outs.append(_dot(p, vp) * r)
            pairs.append(jnp.where(low_half, outs[0], outs[1]))
        return jnp.concatenate(pairs, axis=1).astype(BF16)

    _attend_tile(attend, qi, q_ref, k_ref, v_ref, o_ref, tq, n_ctx)


def _mla_call(q, k, v, n_ctx):
    b, t, _ = q.shape
    tq = _row_tile(t)
    wv = MLA_HEADS * MLA_V
    return pl.pallas_call(
        functools.partial(_mla_kernel, tq=tq, n_ctx=n_ctx),
        grid=(b, t // tq),
        in_specs=[
            pl.BlockSpec((1, tq, MLA_HEADS * LANES), lambda bi, i: (bi, i, 0)),
            pl.BlockSpec((1, t, MLA_HEADS * LANES), lambda bi, i: (bi, 0, 0)),
            pl.BlockSpec((1, t, wv), lambda bi, i: (bi, 0, 0)),
        ],
        out_specs=pl.BlockSpec((1, tq, wv), lambda bi, i: (bi, i, 0)),
        out_shape=jax.ShapeDtypeStruct((b, t, wv), BF16),
        compiler_params=_cparams(("parallel", "parallel")),
        name="latent_attention",
    )(q, k, v)


def _hgrn_kernel(hg_ref, lbf_ref, lbb_ref, gain_ref, o_ref,
                 kf_s, bf_s, kb_s, cb_s, v_s, qtf_s, ktf_s, qtb_s, ktb_s,
                 df_s, db_s, vt_s, oacc_s, stf_s, stb_s,
                 dcf_s, dcb_s, sfa_s, sfb_s, sba_s, sbb_s, *, t_len, n_ctx):
    w = HG_WIDTH
    rt = HG_TILE
    blk = HG_BLOCK
    pad = blk
    n_tiles = t_len // rt
    blk_per_tile = rt // blk

    r_i = lax.broadcasted_iota(jnp.int32, (rt, rt), 0)
    c_i = lax.broadcasted_iota(jnp.int32, (rt, rt), 1)
    same = (r_i // blk) == (c_i // blk)
    incl_lo = jnp.where(same & (c_i <= r_i), 1.0, 0.0).astype(BF16)
    incl_hi = jnp.where(same & (c_i >= r_i), 1.0, 0.0).astype(BF16)
    blk_sum = jnp.where(same, 1.0, 0.0).astype(BF16)
    sel_rows = 16
    g_r = lax.broadcasted_iota(jnp.int32, (sel_rows, rt), 0)
    g_c = lax.broadcasted_iota(jnp.int32, (sel_rows, rt), 1)
    blk_sel = jnp.where((g_c // blk) == g_r, 1.0, 0.0).astype(BF16)
    h_r = lax.broadcasted_iota(jnp.int32, (w, w), 0) // HG_KEY
    h_c = lax.broadcasted_iota(jnp.int32, (w, w), 1) // HG_KEY
    head_ones = jnp.where(h_r == h_c, 1.0, 0.0).astype(BF16)
    head_mask = jnp.where(h_r == h_c, 1.0, 0.0).astype(F32)
    pos = lax.broadcasted_iota(jnp.int32, (rt, 1), 0) % blk

    lbf = lbf_ref[...]
    lbb = lbb_ref[...]

    zpad = jnp.zeros((pad, w), F32)
    for s in (kf_s, bf_s, kb_s, cb_s, v_s):
        s[0:pad, :] = zpad
        s[pad + rt:pad + rt + pad, :] = zpad

    def gates(z, lb):
        f = lb + (1.0 - lb) * _sigmoid(z)
        return (1.0 - lb) * _sigmoid(-z), jnp.log(f)

    ch = HG_CHUNK
    ch_per_tile = rt // ch
    same_c = (r_i // ch) == (c_i // ch)
    lo_c = same_c & (c_i <= r_i)
    hi_c = same_c & (c_i >= r_i)
    cum_lo = jnp.where(lo_c, 1.0, 0.0).astype(BF16)
    cum_hi = jnp.where(hi_c, 1.0, 0.0).astype(BF16)
    ch_sum = jnp.where(same_c, 1.0, 0.0).astype(BF16)
    ch_sel = jnp.where((g_c // ch) == g_r, 1.0, 0.0).astype(BF16)
    head_of_lane = lax.broadcasted_iota(jnp.int32, (1, w), 1) // HG_KEY
    head_cols = [jnp.where(head_of_lane == h, 1.0, 0.0).astype(BF16) for h in range(HG_HEADS)]

    def chunk_tile(ti, worst):
        rows = pl.ds(pl.multiple_of(ti * rt, rt), rt)
        q = hg_ref[0, rows, 0:w] * (HG_KEY ** -0.5)
        v = hg_ref[0, rows, w:2 * w]
        vb = v.astype(BF16)
        od = jnp.zeros((rt, w), F32)
        for col, lb, cum, keep, qt_s, kt_s, dc_s in (
                (2, lbf, cum_lo, lo_c, qtf_s, ktf_s, dcf_s), (3, lbb, cum_hi, hi_c, qtb_s, ktb_s, dcb_s)):
            k, logf = gates(hg_ref[0, rows, col * w:(col + 1) * w], lb)
            l2 = logf * LOG2_E
            b = _dot_exact_lhs(cum, l2)
            tot = _dot_exact_lhs(ch_sum, l2)
            half = 0.5 * tot
            qs = (q * jnp.exp2(b - half)).astype(BF16)
            ks = (k * jnp.exp2(half - b)).astype(BF16)
            qt_s[rows, :] = (q * jnp.exp2(b)).astype(BF16)
            kt_s[rows, :] = (k * jnp.exp2(tot - b)).astype(BF16)
            dc_s[ti] = jnp.exp2(_dot_exact_lhs(ch_sel, l2)[0:8])
            worst = jnp.maximum(worst, jnp.max(-tot, axis=0, keepdims=True))
            for h in range(HG_HEADS):
                p = jnp.where(keep, _dot_nt(qs * head_cols[h], ks), 0.0)
                od = od + _dot(p.astype(BF16), vb * head_cols[h])
        vt_s[:, rows] = v.T.astype(BF16)
        oacc_s[rows, :] = od
        return worst

    worst = lax.fori_loop(0, n_tiles, chunk_tile, jnp.zeros((1, w), F32), unroll=3)
    exact_needed = jnp.max(worst) > HG_CHUNK_RANGE

    half_w = w // 2
    p_r = lax.broadcasted_iota(jnp.int32, (half_w, half_w), 0) // HG_KEY
    p_c = lax.broadcasted_iota(jnp.int32, (half_w, half_w), 1) // HG_KEY
    pair_mask = jnp.where(p_r == p_c, 1.0, 0.0).astype(F32)
    for s in (sfa_s, sfb_s, sba_s, sbb_s):
        s[...] = jnp.zeros((half_w, half_w), F32)
    lane_c = lax.broadcasted_iota(jnp.int32, (1, LANES), 1)

    def advance_chunk(ci, qt_s, kt_s, dc_s, sa_s, sb_s):
        r = pl.multiple_of(ci * ch, ch)
        rows = pl.ds(r, ch)
        sa = sa_s[...]
        sb = sb_s[...]
        oi = jnp.concatenate([_dot_nt(qt_s[rows, 0:half_w], sa.astype(BF16)),
                              _dot_nt(qt_s[rows, half_w:w], sb.astype(BF16))], axis=1)
        oacc_s[rows, :] = oacc_s[rows, :] + oi
        slab = pl.multiple_of((r // LANES) * LANES, LANES)
        off = r - slab
        vt = vt_s[:, pl.ds(slab, LANES)]
        vt = vt * jnp.where((lane_c >= off) & (lane_c < off + ch), 1.0, 0.0).astype(BF16)
        kt = kt_s[pl.ds(slab, LANES), :]
        dec = dc_s[ci // ch_per_tile, pl.ds(ci % ch_per_tile, 1), :]
        sa_s[...] = sa * dec[:, 0:half_w] + _dot(vt[0:half_w, :], kt[:, 0:half_w]) * pair_mask
        sb_s[...] = sb * dec[:, half_w:w] + _dot(vt[half_w:w, :], kt[:, half_w:w]) * pair_mask

    n_ch = t_len // ch
    n_cch = n_ctx // ch

    def chunk_step(i, carry):
        advance_chunk(i, qtf_s, ktf_s, dcf_s, sfa_s, sfb_s)
        jb = jnp.where(i < n_cch, n_cch - 1 - i, n_ch - 1 - (i - n_cch))
        advance_chunk(jb, qtb_s, ktb_s, dcb_s, sba_s, sbb_s)
        return carry

    lax.fori_loop(0, jnp.where(exact_needed, 0, n_ch), chunk_step, 0)

    n_tiles_x = jnp.where(exact_needed, n_tiles, 0)

    def tile_body(ti, carry):
        rows = pl.ds(pl.multiple_of(ti * rt, rt), rt)
        q = hg_ref[0, rows, 0:w] * (HG_KEY ** -0.5)
        v = hg_ref[0, rows, w:2 * w]
        kf, lf = gates(hg_ref[0, rows, 2 * w:3 * w], lbf)
        kb, lg = gates(hg_ref[0, rows, 3 * w:4 * w], lbb)
        b_f = _dot_exact_lhs(incl_lo, lf)
        tot_f = _dot_exact_lhs(blk_sum, lf)
        c_b = _dot_exact_lhs(incl_hi, lg)
        tot_b = _dot_exact_lhs(blk_sum, lg)
        qtf_s[rows, :] = (q * jnp.exp(b_f)).astype(BF16)
        ktf_s[rows, :] = (kf * jnp.exp(tot_f - b_f)).astype(BF16)
        qtb_s[rows, :] = (q * jnp.exp(c_b)).astype(BF16)
        ktb_s[rows, :] = (kb * jnp.exp(tot_b - c_b)).astype(BF16)
        drows = pl.ds(pl.multiple_of(ti * blk_per_tile, blk_per_tile), blk_per_tile)
        df_s[drows, :] = jnp.exp(_dot_exact_lhs(blk_sel, lf)[0:blk_per_tile])
        db_s[drows, :] = jnp.exp(_dot_exact_lhs(blk_sel, lg)[0:blk_per_tile])
        vt_s[:, rows] = v.T.astype(BF16)

        kf_s[pad:pad + rt, :] = kf
        bf_s[pad:pad + rt, :] = b_f
        kb_s[pad:pad + rt, :] = kb
        cb_s[pad:pad + rt, :] = c_b
        v_s[pad:pad + rt, :] = v
        od = jnp.zeros((rt, w), F32)
        for dl in range(blk):
            lo = pad - dl
            e = q * kf_s[lo:lo + rt, :] * jnp.exp(jnp.minimum(b_f - bf_s[lo:lo + rt, :], 0.0))
            e = jnp.where(pos >= dl, e, 0.0)
            od = od + _dot(e.astype(BF16), head_ones) * v_s[lo:lo + rt, :]
            hi = pad + dl
            e = q * kb_s[hi:hi + rt, :] * jnp.exp(jnp.minimum(c_b - cb_s[hi:hi + rt, :], 0.0))
            e = jnp.where(pos + dl < blk, e, 0.0)
            od = od + _dot(e.astype(BF16), head_ones) * v_s[hi:hi + rt, :]
        oacc_s[rows, :] = od
        return carry

    lax.fori_loop(0, n_tiles_x, tile_body, 0)

    stf_s[...] = jnp.zeros((w, w), F32)
    stb_s[...] = jnp.zeros((w, w), F32)
    n_blk = t_len // blk
    n_cblk = n_ctx // blk
    lane = lax.broadcasted_iota(jnp.int32, (1, LANES), 1)

    def advance(bi, qt_s, kt_s, d_s, st_s):
        r = pl.multiple_of(bi * blk, blk)
        st = st_s[...]
        rows = pl.ds(r, blk)
        oacc_s[rows, :] = oacc_s[rows, :] + _dot_nt(qt_s[rows, :], st.astype(BF16))
        slab = pl.multiple_of((r // LANES) * LANES, LANES)
        off = r - slab
        vt = vt_s[:, pl.ds(slab, LANES)]
        vt = vt * jnp.where((lane >= off) & (lane < off + blk), 1.0, 0.0).astype(BF16)
        upd = _dot(vt, kt_s[pl.ds(slab, LANES), :])
        st_s[...] = st * d_s[pl.ds(bi, 1), :] + upd * head_mask

    def step(i, carry):
        advance(i, qtf_s, ktf_s, df_s, stf_s)
        jb = jnp.where(i < n_cblk, n_cblk - 1 - i, n_blk - 1 - (i - n_cblk))
        advance(jb, qtb_s, ktb_s, db_s, stb_s)
        return carry

    lax.fori_loop(0, jnp.where(exact_needed, n_blk, 0), step, 0)

    gain = gain_ref[...]

    def out_body(ti, carry):
        rows = pl.ds(pl.multiple_of(ti * rt, rt), rt)
        o = oacc_s[rows, :]
        ms = _dot_exact_rhs(o * o, head_ones) * (1.0 / HG_KEY)
        y = o * lax.rsqrt(ms + 1e-6) * gain
        g = hg_ref[0, rows, 4 * w:5 * w]
        o_ref[0, rows, :] = (y * (g * _sigmoid(g))).astype(BF16)
        return carry

    lax.fori_loop(0, n_tiles, out_body, 0)


def _hgrn_call(hg, lbf, lbb, gain_t, n_ctx):
    b, t, _ = hg.shape
    w = HG_WIDTH
    tile_pad = pltpu.VMEM((HG_TILE + 2 * HG_BLOCK, w), F32)
    seq_bf16 = pltpu.VMEM((t, w), BF16)
    dec = pltpu.VMEM((t // HG_BLOCK, w), F32)
    return pl.pallas_call(
        functools.partial(_hgrn_kernel, t_len=t, n_ctx=n_ctx),
        grid=(b,),
        in_specs=[
            pl.BlockSpec((1, t, 5 * w), lambda bi: (bi, 0, 0)),
            pl.BlockSpec((1, w), lambda bi: (0, 0)),
            pl.BlockSpec((1, w), lambda bi: (0, 0)),
            pl.BlockSpec((1, w), lambda bi: (0, 0)),
        ],
        out_specs=pl.BlockSpec((1, t, w), lambda bi: (bi, 0, 0)),
        out_shape=jax.ShapeDtypeStruct((b, t, w), BF16),
        scratch_shapes=[tile_pad] * 5 + [seq_bf16] * 4 + [dec, dec,
                        pltpu.VMEM((w, t), BF16), pltpu.VMEM((t, w), F32),
                        pltpu.VMEM((w, w), F32), pltpu.VMEM((w, w), F32)]
        + [pltpu.VMEM((t // HG_TILE, 8, w), F32)] * 2 + [pltpu.VMEM((w // 2, w // 2), F32)] * 4,
        compiler_params=_cparams(("parallel",)),
        name="hgrn2",
    )(hg, lbf, lbb, gain_t)


def _mod_rows(ml_ref, mc_ref, tm, n_ctx, idx):
    i = pl.program_id(1)
    row = i * tm + lax.broadcasted_iota(jnp.int32, (tm, 1), 0)
    isctx = row < n_ctx
    ml = ml_ref[0]
    mc = mc_ref[0]
    return [jnp.where(isctx, mc[k:k + 1], ml[k:k + 1]) for k in idx]


def _out_kernel(oa_ref, ob_ref, og_ref, wo_ref, x_ref, ml_ref, mc_ref, gain_ref, *rest,
                tm, n_ctx, routed, d_ff):
    if routed:
        router_ref, xo_ref, h_ref, gates_ref = rest
    else:
        wg_ref, wu_ref, wd_ref, xo_ref, acc_ref = rest
    gate, shift, scale, ffn_gate = _mod_rows(ml_ref, mc_ref, tm, n_ctx, (2, 3, 4, 5))
    mix = (_dot(oa_ref[0], wo_ref[0:512, :]) + _dot(ob_ref[0], wo_ref[512:768, :])
           + _dot(og_ref[0], wo_ref[768:1024, :]))
    x = x_ref[0] + gate * mix
    y = x * lax.rsqrt(jnp.mean(x * x, axis=-1, keepdims=True) + 1e-6) * gain_ref[...]
    h = y * (1.0 + scale) + shift
    if not routed:
        acc_ref[...] = jnp.zeros_like(acc_ref)
        _swiglu_into(acc_ref, h.astype(BF16), wg_ref, wu_ref, wd_ref, None, d_ff)
        xo_ref[0] = x + ffn_gate * acc_ref[...]
    else:
        xo_ref[0] = x
        h_ref[0] = h
    if routed:
        lane = lax.broadcasted_iota(jnp.int32, (1, LANES), 1)
        logits = _dot_f32(h, router_ref[...])
        logits = jnp.where(lane < N_EXPERTS, logits, -jnp.inf)
        m1 = jnp.max(logits, axis=-1, keepdims=True)
        i1 = jnp.min(jnp.where(logits == m1, lane, LANES), axis=-1, keepdims=True)
        rest_l = jnp.where(lane == i1, -jnp.inf, logits)
        m2 = jnp.max(rest_l, axis=-1, keepdims=True)
        i2 = jnp.min(jnp.where(rest_l == m2, lane, LANES), axis=-1, keepdims=True)
        e2 = jnp.exp(m2 - m1)
        w1 = 1.0 / (1.0 + e2)
        route = jnp.where(lane == 0, i1.astype(F32), 0.0) + jnp.where(lane == 1, i2.astype(F32), 0.0)
        gates_ref[0] = route + jnp.where(lane == 2, w1, 0.0) + jnp.where(lane == 3, e2 * w1, 0.0)


def _out_call(oa, ob, og, wo, xa, mods, gain, n_ctx, router_p=None, ffn=None, skip_rows=0):
    b, t, d = xa.shape
    routed = router_p is not None
    tm = _row_tile(t) if skip_rows == 0 else ATT_SUB
    skip = skip_rows // tm
    t_out = t - skip_rows
    ctx_rows = n_ctx - skip_rows
    in_blk = lambda width: pl.BlockSpec((1, tm, width), lambda bi, i: (bi, i + skip, 0))
    out_blk = lambda width: pl.BlockSpec((1, tm, width), lambda bi, i: (bi, i, 0))
    in_specs = [
        in_blk(oa.shape[-1]), in_blk(ob.shape[-1]), in_blk(og.shape[-1]),
        _resident(wo.shape),
        in_blk(d),
        pl.BlockSpec((1, 6, d), lambda bi, i: (bi, 0, 0)),
        pl.BlockSpec((1, 6, d), lambda bi, i: (b, 0, 0)),
        _resident((1, d)),
    ]
    args = [oa, ob, og, wo, xa, mods, mods, gain]
    scratch = []
    d_ff = 0
    if routed:
        in_specs.append(_resident(router_p.shape))
        args.append(router_p)
        out_specs = [out_blk(d), out_blk(d), out_blk(LANES)]
        out_shape = [jax.ShapeDtypeStruct((b, t_out, d), F32), jax.ShapeDtypeStruct((b, t_out, d), F32),
                     jax.ShapeDtypeStruct((b, t_out, LANES), F32)]
    else:
        wg, wu, wd = ffn
        d_ff = wg.shape[-1]
        in_specs += [_resident((1, d, d_ff)), _resident((1, d, d_ff)), _resident((1, d_ff, d))]
        args += [wg[None], wu[None], wd[None]]
        out_specs = out_blk(d)
        out_shape = jax.ShapeDtypeStruct((b, t_out, d), F32)
        scratch = [pltpu.VMEM((tm, d), F32)]
    return pl.pallas_call(
        functools.partial(_out_kernel, tm=tm, n_ctx=ctx_rows, routed=routed, d_ff=d_ff),
        grid=(b, t_out // tm),
        in_specs=in_specs,
        out_specs=out_specs,
        out_shape=out_shape,
        scratch_shapes=scratch,
        compiler_params=_cparams(("parallel", "parallel")),
        name="out_proj_routed" if routed else "out_proj_ffn",
    )(*args)


def _swiglu_into(acc_ref, h, wg_ref, wu_ref, wd_ref, row_gate, d_ff):
    for c in range(d_ff // FF_CHUNK):
        cs = slice(c * FF_CHUNK, (c + 1) * FF_CHUNK)
        g = _dot(h, wg_ref[0, :, cs])
        u = _dot(h, wu_ref[0, :, cs])
        a = g * _sigmoid(g) * u
        if row_gate is not None:
            a = a * row_gate
        acc_ref[...] += _dot(a.astype(BF16), wd_ref[0, cs, :])


def _row_copy(src, src_row, dst, dst_row, sem):
    return pltpu.make_async_copy(src.at[pl.ds(src_row, 1), :], dst.at[pl.ds(dst_row, 1), :], sem)


def _dispatch_kernel(pos_ref, pad_ref, h_ref, xs_ref, zero_s, sem, zsem, *, tm, n_exp, n_tiles):
    step = pl.program_id(0)
    base = step * tm

    def pad_copies(act):
        for e in range(n_exp):
            start = pad_ref[e]

            def tail_row(r, carry):
                act(_row_copy(zero_s, 0, xs_ref, start + r, zsem))
                return carry

            lax.fori_loop(0, pad_ref[n_exp + e], tail_row, 0)
        for j in range(n_exp):
            tile = pad_ref[2 * n_exp] + j

            @pl.when(tile < n_tiles)
            def _():
                row0 = pl.multiple_of(tile * tm, tm)
                act(pltpu.make_async_copy(zero_s, xs_ref.at[pl.ds(row0, tm), :], zsem))

    @pl.when(step == 0)
    def _():
        zero_s[...] = jnp.zeros_like(zero_s)
        pad_copies(lambda c: c.start())

    def token_copies(g, act):
        for u in range(SUBLANES):
            for s in range(TOP_K):
                slot = pos_ref[TOP_K * (base + g * SUBLANES + u) + s]
                act(pltpu.make_async_copy(h_ref.at[g, pl.ds(u, 1), :], xs_ref.at[pl.ds(slot, 1), :], sem))

    def issue(g, carry):
        token_copies(g, lambda c: c.start())
        return carry

    lax.fori_loop(0, tm // SUBLANES, issue, 0)

    def drain(g, carry):
        token_copies(g, lambda c: c.wait())
        return carry

    lax.fori_loop(0, tm // SUBLANES, drain, 0)

    @pl.when(step == 0)
    def _():
        pad_copies(lambda c: c.wait())


def _dispatch_call(pos, pad_info, h2d, n_tiles, n_exp):
    n, d = h2d.shape
    tm = MOE_TILE
    return pl.pallas_call(
        functools.partial(_dispatch_kernel, tm=tm, n_exp=n_exp, n_tiles=n_tiles),
        grid_spec=pltpu.PrefetchScalarGridSpec(
            num_scalar_prefetch=2,
            grid=(n // tm,),
            in_specs=[pl.BlockSpec((tm // SUBLANES, SUBLANES, d), lambda i, pos, pad: (i, 0, 0))],
            out_specs=pl.BlockSpec(memory_space=pl.ANY),
            scratch_shapes=[pltpu.VMEM((tm, d), F32), pltpu.SemaphoreType.DMA(()),
                            pltpu.SemaphoreType.DMA(())],
        ),
        out_shape=jax.ShapeDtypeStruct((n_tiles * tm, d), F32),
        compiler_params=_cparams(("arbitrary",)),
        name="moe_dispatch",
    )(pos, pad_info, h2d.reshape(n // SUBLANES, SUBLANES, d))


def _expert_kernel(te_ref, nv_ref, xs_ref, wg_ref, wu_ref, wd_ref, ys_ref, acc_ref, *, tm, d_ff):
    n_valid = nv_ref[pl.program_id(0)]

    @pl.when(n_valid > 0)
    def _():
        acc_ref[...] = jnp.zeros_like(acc_ref)
        _swiglu_into(acc_ref, xs_ref[...].astype(BF16), wg_ref, wu_ref, wd_ref, None, d_ff)
        ys_ref[...] = acc_ref[...]

    @pl.when(n_valid == 0)
    def _():
        ys_ref[...] = jnp.zeros_like(ys_ref)


def _expert_call(tile_expert, tile_valid, xs, wg, wu, wd):
    n_slots, d = xs.shape
    d_ff = wg.shape[-1]
    tm = MOE_TILE
    w_blk = lambda shape: pl.BlockSpec(shape, lambda t, te, nv: (te[t], 0, 0))
    return pl.pallas_call(
        functools.partial(_expert_kernel, tm=tm, d_ff=d_ff),
        grid_spec=pltpu.PrefetchScalarGridSpec(
            num_scalar_prefetch=2,
            grid=(n_slots // tm,),
            in_specs=[
                pl.BlockSpec((tm, d), lambda t, te, nv: (t, 0)),
                w_blk((1, d, d_ff)), w_blk((1, d, d_ff)), w_blk((1, d_ff, d)),
            ],
            out_specs=pl.BlockSpec((tm, d), lambda t, te, nv: (t, 0)),
            scratch_shapes=[pltpu.VMEM((tm, d), F32)],
        ),
        out_shape=jax.ShapeDtypeStruct((n_slots, d), F32),
        compiler_params=_cparams(("arbitrary",)),
        name="moe_experts",
    )(tile_expert, tile_valid, xs, wg, wu, wd)


def _combine_kernel(pos_ref, ys_ref, x_ref, route_ref, ml_ref, mc_ref, fgain_ref, xo_ref, y1_s, y2_s, sem,
                    *, tm, n_ctx, final_norm):
    base = (pl.program_id(0) * pl.num_programs(1) + pl.program_id(1)) * tm
    bufs = (y1_s, y2_s)

    def token_copies(g, act):
        for u in range(SUBLANES):
            for s in range(TOP_K):
                slot = pos_ref[TOP_K * (base + g * SUBLANES + u) + s]
                act(pltpu.make_async_copy(ys_ref.at[pl.ds(slot, 1), :], bufs[s].at[g, pl.ds(u, 1), :], sem))

    def issue(g, carry):
        token_copies(g, lambda c: c.start())
        return carry

    lax.fori_loop(0, tm // SUBLANES, issue, 0)

    def drain(g, carry):
        token_copies(g, lambda c: c.wait())
        return carry

    lax.fori_loop(0, tm // SUBLANES, drain, 0)

    d = x_ref.shape[-1]
    y1 = y1_s[...].reshape(tm, d)
    y2 = y2_s[...].reshape(tm, d)
    lane = lax.broadcasted_iota(jnp.int32, (1, LANES), 1)
    route = route_ref[0]
    w1 = jnp.sum(jnp.where(lane == 2, route, 0.0), axis=-1, keepdims=True)
    w2 = jnp.sum(jnp.where(lane == 3, route, 0.0), axis=-1, keepdims=True)
    (gate,) = _mod_rows(ml_ref, mc_ref, tm, n_ctx, (5,))
    x = x_ref[0] + gate * (w1 * y1 + w2 * y2)
    if final_norm:
        x = x * lax.rsqrt(jnp.mean(x * x, axis=-1, keepdims=True) + 1e-6) * fgain_ref[...]
    xo_ref[0] = x


def _combine_call(pos, ys, xa, route, mods, n_ctx, final_gain, final_norm):
    b, t, d = xa.shape
    tm = _row_tile(t)
    row_blk = lambda width: pl.BlockSpec((1, tm, width), lambda bi, i, pos: (bi, i, 0))
    return pl.pallas_call(
        functools.partial(_combine_kernel, tm=tm, n_ctx=n_ctx, final_norm=final_norm),
        grid_spec=pltpu.PrefetchScalarGridSpec(
            num_scalar_prefetch=1,
            grid=(b, t // tm),
            in_specs=[
                pl.BlockSpec(memory_space=pl.ANY),
                row_blk(d), row_blk(LANES),
                pl.BlockSpec((1, 6, d), lambda bi, i, pos: (bi, 0, 0)),
                pl.BlockSpec((1, 6, d), lambda bi, i, pos: (b, 0, 0)),
                pl.BlockSpec((1, d), lambda bi, i, pos: (0, 0)),
            ],
            out_specs=row_blk(d),
            scratch_shapes=[pltpu.VMEM((tm // SUBLANES, SUBLANES, d), F32)] * 2
            + [pltpu.SemaphoreType.DMA(())],
        ),
        out_shape=jax.ShapeDtypeStruct((b, t, d), F32),
        compiler_params=_cparams(("arbitrary", "arbitrary")),
        name="moe_combine",
    )(pos, ys, xa, route, mods, mods, final_gain)


def _moe_call(h, xa, route, mods, wg, wu, wd, n_ctx, final_gain, final_norm):
    b, t, d = xa.shape
    n = b * t
    tm = MOE_TILE
    n_exp = wg.shape[0]
    n_tiles = (TOP_K * n) // tm + n_exp
    choice = route.reshape(n, LANES)[:, :TOP_K].astype(jnp.int32).reshape(-1)
    onehot = (choice[:, None] == jnp.arange(n_exp, dtype=jnp.int32)[None, :]).astype(jnp.int32)
    csum = jnp.cumsum(onehot, axis=0)
    counts = csum[-1]
    rank = jnp.sum((csum - onehot) * onehot, axis=1)
    group_tiles = (counts + tm - 1) // tm
    tile_end = jnp.cumsum(group_tiles)
    group_row0 = (tile_end - group_tiles) * tm
    pos = jnp.sum(group_row0[None, :] * onehot, axis=1) + rank
    tile_id = jnp.arange(n_tiles, dtype=jnp.int32)
    tile_group = jnp.sum((tile_id[:, None] >= tile_end[None, :]).astype(jnp.int32), axis=1)
    last_group = jnp.max(jnp.where(counts > 0, jnp.arange(n_exp, dtype=jnp.int32), 0))
    tile_expert = jnp.minimum(tile_group, last_group)
    in_group = tile_group < n_exp
    g = jnp.minimum(tile_group, n_exp - 1)
    tile_valid = jnp.where(in_group, jnp.clip(counts[g] - (tile_id * tm - group_row0[g]), 0, tm), 0)

    pad_info = jnp.concatenate([group_row0 + counts, group_tiles * tm - counts, tile_end[-1:]])
    xs = _dispatch_call(pos, pad_info.astype(jnp.int32), h.reshape(n, d), n_tiles, n_exp)
    ys = _expert_call(tile_expert.astype(jnp.int32), tile_valid.astype(jnp.int32), xs, wg, wu, wd)
    return _combine_call(pos, ys, xa, route, mods, n_ctx, final_gain, final_norm)


def _final_kernel(x_ref, g_ref, o_ref):
    x = x_ref[0]
    o_ref[0] = x * lax.rsqrt(jnp.mean(x * x, axis=-1, keepdims=True) + 1e-6) * g_ref[...]


def _final_call(xa, gain, n_ctx):
    b, t, d = xa.shape
    tm = 256
    skip = n_ctx // tm
    return pl.pallas_call(
        _final_kernel,
        grid=(b, (t - n_ctx) // tm),
        in_specs=[pl.BlockSpec((1, tm, d), lambda bi, i: (bi, i + skip, 0)), _resident((1, d))],
        out_specs=pl.BlockSpec((1, tm, d), lambda bi, i: (bi, i, 0)),
        out_shape=jax.ShapeDtypeStruct((b, t - n_ctx, d), F32),
        compiler_params=_cparams(("parallel", "parallel")),
        name="final_norm",
    )(xa, gain)


def _rope_tables(seq, n_ctx):
    rows = seq // GRID_W
    row, col = jnp.meshgrid(jnp.arange(rows, dtype=F32), jnp.arange(GRID_W, dtype=F32), indexing="ij")
    row, col = row.reshape(-1), col.reshape(-1)

    def angles(rot_dim):
        n_freq = rot_dim // 4
        inv = ROPE_BASE ** (-jnp.arange(n_freq, dtype=F32) / n_freq)
        ang = jnp.concatenate([row[:, None] * inv, col[:, None] * inv], axis=-1)
        return jnp.cos(ang), jnp.sin(ang)

    def with_ctx(a, fill):
        return jnp.concatenate([jnp.full((n_ctx, a.shape[1]), fill, F32), a], axis=0)

    cos, sin = angles(DA_HEAD_DIM)
    cos_da = with_ctx(jnp.tile(cos, (1, 4)), 1.0)
    sin_da = with_ctx(jnp.tile(jnp.concatenate([-sin, sin], axis=1), (1, 2)), 0.0)
    cos, sin = angles(MLA_ROPE)
    ones = jnp.ones((seq, MLA_NOPE), F32)
    tail = jnp.ones((seq, LANES - MLA_NOPE - MLA_ROPE), F32)
    cos_m = with_ctx(jnp.concatenate([ones, cos, cos, tail], axis=1), 1.0)
    sin_m = with_ctx(jnp.concatenate([0 * ones, -sin, sin, 0 * tail], axis=1), 0.0)
    return cos_da, sin_da, cos_m, sin_m


def _pad_in_proj(w_in_l):
    d = w_in_l.shape[0]
    cut = 1536 + MLA_Q_RANK + MLA_KV_RANK + MLA_ROPE
    padw = 2048 - cut
    return jnp.concatenate([w_in_l[:, :cut], jnp.zeros((d, padw), w_in_l.dtype), w_in_l[:, cut:]],
                           axis=1).astype(BF16)


def _pad_uq(w_uq_l):
    qk = MLA_NOPE + MLA_ROPE
    w = w_uq_l.reshape(MLA_Q_RANK, MLA_HEADS, qk)
    w = jnp.pad(w, ((0, 0), (0, 0), (0, LANES - qk)))
    return w.reshape(MLA_Q_RANK, MLA_HEADS * LANES).astype(BF16)


def _pad_ukv(w_ukv_l):
    w = w_ukv_l.reshape(MLA_KV_RANK, MLA_HEADS, MLA_NOPE + MLA_V)
    k_part = jnp.pad(w[:, :, :MLA_NOPE], ((0, 0), (0, 0), (0, LANES - MLA_NOPE)))
    k_part = k_part.reshape(MLA_KV_RANK, MLA_HEADS * LANES)
    v_part = w[:, :, MLA_NOPE:].reshape(MLA_KV_RANK, MLA_HEADS * MLA_V)
    place = jnp.zeros((MLA_ROPE, MLA_HEADS, LANES), F32)
    eye = jnp.eye(MLA_ROPE, dtype=F32)
    place = place.at[:, :, MLA_NOPE:MLA_NOPE + MLA_ROPE].set(eye[:, None, :])
    place = place.reshape(MLA_ROPE, MLA_HEADS * LANES)
    top = jnp.concatenate([k_part, v_part], axis=1)
    mid = jnp.concatenate([place, jnp.zeros((MLA_ROPE, MLA_HEADS * MLA_V), F32)], axis=1)
    bot = jnp.zeros((MXU_COLS - MLA_KV_RANK - MLA_ROPE, top.shape[1]), F32)
    return jnp.concatenate([top, mid, bot], axis=0).astype(BF16)


def kernel(x, c, ctx, c_ctx, w_mod, b_mod, norm_mix, norm_ffn, norm_final, w_in, w_out, diff_lambda, diff_norm, mla_q_norm, mla_w_uq, mla_kv_norm, mla_w_ukv, hgrn_lb_logits, hgrn_norm, ffn_w_gate, ffn_w_up, ffn_w_down, moe_router, moe_w_gate, moe_w_up, moe_w_down):
    b, seq, d = x.shape
    n_ctx = ctx.shape[1]
    depth = w_mod.shape[0]
    assert d == 1024 and n_ctx % ATT_SUB == 0 and seq % ATT_SUB == 0

    xa = jnp.concatenate([ctx, x], axis=1)
    bp = -(-(b + 1) // 8) * 8
    c_all = jnp.concatenate([c, c_ctx[None, :], jnp.zeros((bp - b - 1, d), F32)], axis=0)
    mods = _mod_call(c_all, w_mod, b_mod).reshape(depth, bp, 6, d)

    lb = jnp.cumsum(jax.nn.softmax(hgrn_lb_logits.astype(F32), axis=1), axis=1)
    lb = lb - lb[:, :1]
    tabs = _rope_tables(seq, n_ctx)

    for l in range(depth):
        lam_init = 0.8 - 0.6 * math.exp(-0.3 * l)
        q_da, k_da, v_da, q_m, k_m, v_m, hg = _in_call(
            xa, norm_mix[l][None], mods[l], _pad_in_proj(w_in[l]), tabs,
            mla_q_norm[l][None], mla_kv_norm[l][None], _pad_uq(mla_w_uq[l]), _pad_ukv(mla_w_ukv[l]),
            n_ctx)
        oa = _da_call(q_da, k_da, v_da, diff_lambda[l], jnp.full((1, 1), lam_init, F32),
                      (diff_norm[l] * (1.0 - lam_init))[None], n_ctx)
        ob = _mla_call(q_m, k_m, v_m, n_ctx)
        og = _hgrn_call(hg, lb[0, l][None], lb[1, l][None], jnp.tile(hgrn_norm[l], HG_HEADS)[None], n_ctx)

        wo = w_out[l].astype(BF16)
        skip = n_ctx if l == depth - 1 else 0
        if l % 2 == 0:
            ffn = (ffn_w_gate[l // 2].astype(BF16), ffn_w_up[l // 2].astype(BF16),
                   ffn_w_down[l // 2].astype(BF16))
            xa = _out_call(oa, ob, og, wo, xa, mods[l], norm_ffn[l][None], n_ctx, ffn=ffn, skip_rows=skip)
        else:
            router_p = jnp.pad(moe_router[l // 2], ((0, 0), (0, LANES - N_EXPERTS)))
            xa, h, route = _out_call(oa, ob, og, wo, xa, mods[l], norm_ffn[l][None], n_ctx,
                                     router_p=router_p, skip_rows=skip)
            xa = _moe_call(h, xa, route, mods[l], moe_w_gate[l // 2].astype(BF16),
                           moe_w_up[l // 2].astype(BF16), moe_w_down[l // 2].astype(BF16), n_ctx - skip,
                           norm_final[None], l == depth - 1)
    if depth % 2 == 0:
        return xa
    return _final_call(xa, norm_final[None], 0)
```
